```python
import math
import jax, jax.numpy as jnp
from jax import lax
import numpy as np

D_MODEL = 1024
BATCH = 8
SEQ = 8192
DEPTH = 2

N_META = 16
M_HEADS = 4
M_DK = 128
M_DV = 128
M_WIDTH = M_HEADS * M_DV
CONV_W = 4
CHUNK = 64
A_HEADS = 4
A_DK = 64
A_DV = 2 * A_DK
A_WIDTH = A_HEADS * A_DV
Q_BLOCK = 128
D_FF = 2816
ALPHA = (2 * DEPTH) ** 0.25
BETA = (8 * DEPTH) ** -0.25
LN_EPS = 1e-5
NEG = -1e30
PROJ_SPLITS = (M_HEADS * M_DK, M_HEADS * M_DK, M_WIDTH, M_WIDTH, 2 * M_HEADS,
               A_HEADS * 2 * A_DK, A_HEADS * 2 * A_DK, A_WIDTH, 2 * D_MODEL)
D_PROJ = sum(PROJ_SPLITS)

kernel_name = "hybrid_mlstm_diffattn_macaron_deepnorm"


def layer_norm(x, g, b):
    xf = x.astype(jnp.float32)
    mu = jnp.mean(xf, -1, keepdims=True)
    var = jnp.mean(jnp.square(xf - mu), -1, keepdims=True)
    return ((xf - mu) * lax.rsqrt(var + LN_EPS) * g + b).astype(x.dtype)


def swiglu(x, w_gate, w_up, w_down):
    return (jax.nn.silu(x @ w_gate) * (x @ w_up)) @ w_down


def causal_dwconv(u, w, b):
    L = u.shape[1]
    up = jnp.pad(u, ((0, 0), (CONV_W - 1, 0), (0, 0)))
    return b + sum(up[:, j:j + L] * w[j] for j in range(CONV_W))


def alibi_slopes():
    return jnp.array([2.0 ** (-8.0 * (h + 1) / A_HEADS) for h in range(A_HEADS)], jnp.float32)


def mlstm_chunk(carry, inp):
    C, n, m = carry
    q, k, v, log_i, log_f = inp
    T = q.shape[2]
    b = jnp.cumsum(log_f, axis=-1)
    a = b + m[..., None]
    D = b[..., :, None] - b[..., None, :] + log_i[..., None, :]
    causal = jnp.tril(jnp.ones((T, T), dtype=bool))
    D = jnp.where(causal, D, NEG)
    m_t = jnp.maximum(a, jnp.max(D, -1))
    w_state = jnp.exp(a - m_t)
    S = jnp.einsum('bhtk,bhsk->bhts', q, k) * jnp.exp(D - m_t[..., None])
    num = jnp.einsum('bhts,bhsv->bhtv', S, v) + w_state[..., None] * jnp.einsum('bhtk,bhkv->bhtv', q, C)
    den = jnp.sum(S, -1) + w_state * jnp.einsum('bhtk,bhk->bht', q, n)
    h = num / jnp.maximum(jnp.abs(den), jnp.exp(-m_t))[..., None]
    g_end = b[..., -1]
    a_end = g_end + m
    d_end = g_end[..., None] - b + log_i
    m_new = jnp.maximum(a_end, jnp.max(d_end, -1))
    ws = jnp.exp(d_end - m_new[..., None])
    decay = jnp.exp(a_end - m_new)
    C_new = decay[..., None, None] * C + jnp.einsum('bhs,bhsk,bhsv->bhkv', ws, k, v)
    n_new = decay[..., None] * n + jnp.einsum('bhs,bhsk->bhk', ws, k)
    return (C_new, n_new, m_new), h


def mlstm_chunkwise(q, k, v, log_i, log_f):
    B, H, L, _ = q.shape
    nc = (L - N_META) // CHUNK
    carry = (jnp.zeros((B, H, M_DK, M_DV), jnp.float32),
             jnp.zeros((B, H, M_DK), jnp.float32),
             jnp.full((B, H), NEG, jnp.float32))
    ins = (q, k, v, log_i, log_f)
    carry, h_meta = mlstm_chunk(carry, tuple(t[:, :, :N_META] for t in ins))

    def to_chunks(t):
        r = t[:, :, N_META:]
        r = r.reshape((B, H, nc, CHUNK) + r.shape[3:])
        return jnp.moveaxis(r, 2, 0)

    _, h_real = lax.scan(mlstm_chunk, carry, tuple(to_chunks(t) for t in ins))
    h_real = jnp.moveaxis(h_real, 0, 2).reshape(B, H, L - N_META, M_DV)
    return jnp.concatenate([h_meta, h_real], axis=2)


def mlstm_branch(mq, mk, mv, mo, mif, conv_w, conv_b, b_if, norm_g):
    B, L, _ = mq.shape
    qk = jax.nn.silu(causal_dwconv(jnp.concatenate([mq, mk], -1), conv_w, conv_b))
    q, k = jnp.split(qk, 2, axis=-1)

    def heads(t, d):
        return t.reshape(B, L, M_HEADS, d).transpose(0, 2, 1, 3).astype(jnp.float32)

    q = heads(q, M_DK)
    k = heads(k, M_DK) * (M_DK ** -0.5)
    v = heads(mv, M_DV)
    gates = (mif + b_if).astype(jnp.float32).transpose(0, 2, 1)
    log_i = gates[:, :M_HEADS]
    log_f = jax.nn.log_sigmoid(gates[:, M_HEADS:])
    h = mlstm_chunkwise(q, k, v, log_i, log_f)
    mu = jnp.mean(h, -1, keepdims=True)
    var = jnp.mean(jnp.square(h - mu), -1, keepdims=True)
    h = ((h - mu) * lax.rsqrt(var + LN_EPS)).transpose(0, 2, 1, 3).reshape(B, L, M_WIDTH)
    return (h * norm_g * jax.nn.sigmoid(mo.astype(jnp.float32))).astype(mq.dtype)


def diff_attn_branch(aq, ak, av, lam_q1, lam_k1, lam_q2, lam_k2, norm_g, lam_init):
    B, L, _ = aq.shape
    q = aq.reshape(B, L, A_HEADS, 2, A_DK).transpose(0, 2, 3, 1, 4).astype(jnp.float32) * (A_DK ** -0.5)
    k = ak.reshape(B, L, A_HEADS, 2, A_DK).transpose(0, 2, 3, 1, 4).astype(jnp.float32)
    v = av.reshape(B, L, A_HEADS, A_DV).transpose(0, 2, 1, 3).astype(jnp.float32)
    f32 = jnp.float32
    lam = (jnp.exp(jnp.sum(lam_q1.astype(f32) * lam_k1.astype(f32)))
           - jnp.exp(jnp.sum(lam_q2.astype(f32) * lam_k2.astype(f32))) + lam_init)
    slopes = alibi_slopes()

    def attend(qb, tpos, kb, vb):
        spos = jnp.arange(kb.shape[3])
        dist = (tpos[:, None] - spos[None, :]).astype(f32)
        s = jnp.einsum('bhcqd,bhckd->bhcqk', qb, kb) - (slopes[:, None, None] * dist)[None, :, None]
        s = jnp.where(spos[None, :] <= tpos[:, None], s, NEG)
        p = jax.nn.softmax(s, axis=-1)
        return jnp.einsum('bhqk,bhkv->bhqv', p[:, :, 0] - lam * p[:, :, 1], vb)

    o_meta = attend(q[:, :, :, :N_META], jnp.arange(N_META), k[:, :, :, :N_META], v[:, :, :N_META])

    def real_block(j):
        start = N_META + j * Q_BLOCK
        qb = lax.dynamic_slice_in_dim(q, start, Q_BLOCK, axis=3)
        return attend(qb, start + jnp.arange(Q_BLOCK), k, v)

    nb = (L - N_META) // Q_BLOCK
    o_real = lax.map(real_block, jnp.arange(nb))
    o_real = jnp.moveaxis(o_real, 0, 2).reshape(B, A_HEADS, L - N_META, A_DV)
    o = jnp.concatenate([o_meta, o_real], axis=2)
    o = o * lax.rsqrt(jnp.mean(o * o, -1, keepdims=True) + LN_EPS) * (1.0 - lam_init)
    o = o.transpose(0, 2, 1, 3).reshape(B, L, A_WIDTH) * norm_g
    return o.astype(aq.dtype)


def hybrid_mixer(x, w_in, conv_w, conv_b, b_if, m_norm_g, lam_q1, lam_k1, lam_q2, lam_k2,
                 a_norm_g, w_bm, w_ba, b_gate, w_out, lam_init):
    proj = x @ w_in
    offs = np.cumsum(PROJ_SPLITS)[:-1].tolist()
    mq, mk, mv, mo, mif, aq, ak, av, g = jnp.split(proj, offs, axis=-1)
    y_m = mlstm_branch(mq, mk, mv, mo, mif, conv_w, conv_b, b_if, m_norm_g) @ w_bm
    y_a = diff_attn_branch(aq, ak, av, lam_q1, lam_k1, lam_q2, lam_k2, a_norm_g, lam_init) @ w_ba
    g = jax.nn.sigmoid(g + b_gate)
    g_m, g_a = jnp.split(g, 2, axis=-1)
    return (g_m * y_m + g_a * y_a) @ w_out


def setup_inputs(seed: int = 0) -> dict:
    key = jax.random.key(seed)
    ks = jax.random.split(key, 32)
    f32 = jnp.float32

    def nrm(k, shape, scale):
        return jax.random.normal(k, shape, f32) * scale

    Dp = DEPTH
    b_if = jnp.concatenate([
        nrm(ks[10], (Dp, M_HEADS), 0.1),
        jnp.broadcast_to(jnp.linspace(3.0, 6.0, M_HEADS, dtype=f32), (Dp, M_HEADS)) + nrm(ks[11], (Dp, M_HEADS), 0.1),
    ], axis=-1)
    return {
        "x": nrm(ks[0], (BATCH, SEQ, D_MODEL), 1.0),
        "meta": nrm(ks[1], (N_META, D_MODEL), 1.0),
        "ffn1_w_gate": nrm(ks[2], (Dp, D_MODEL, D_FF), D_MODEL ** -0.5),
        "ffn1_w_up": nrm(ks[3], (Dp, D_MODEL, D_FF), D_MODEL ** -0.5),
        "ffn1_w_down": nrm(ks[4], (Dp, D_FF, D_MODEL), BETA * D_FF ** -0.5),
        "ln1_g": 1.0 + nrm(ks[5], (Dp, D_MODEL), 0.02),
        "ln1_b": nrm(ks[6], (Dp, D_MODEL), 0.02),
        "w_in": nrm(ks[7], (Dp, D_MODEL, D_PROJ), D_MODEL ** -0.5),
        "conv_w": nrm(ks[8], (Dp, CONV_W, 2 * M_HEADS * M_DK), CONV_W ** -0.5),
        "conv_b": nrm(ks[9], (Dp, 2 * M_HEADS * M_DK), 0.01),
        "b_if": b_if,
        "m_norm_g": 1.0 + nrm(ks[12], (Dp, M_WIDTH), 0.02),
        "lam_q1": nrm(ks[13], (Dp, A_DK), 0.1),
        "lam_k1": nrm(ks[14], (Dp, A_DK), 0.1),
        "lam_q2": nrm(ks[15], (Dp, A_DK), 0.1),
        "lam_k2": nrm(ks[16], (Dp, A_DK), 0.1),
        "a_norm_g": 1.0 + nrm(ks[17], (Dp, A_WIDTH), 0.02),
        "w_bm": nrm(ks[18], (Dp, M_WIDTH, D_MODEL), BETA * M_WIDTH ** -0.5),
        "w_ba": nrm(ks[19], (Dp, A_WIDTH, D_MODEL), BETA * A_WIDTH ** -0.5),
        "b_gate": nrm(ks[20], (Dp, 2 * D_MODEL), 0.01),
        "w_out": nrm(ks[21], (Dp, D_MODEL, D_MODEL), BETA * D_MODEL ** -0.5),
        "ln2_g": 1.0 + nrm(ks[22], (Dp, D_MODEL), 0.02),
        "ln2_b": nrm(ks[23], (Dp, D_MODEL), 0.02),
        "ffn2_w_gate": nrm(ks[24], (Dp, D_MODEL, D_FF), D_MODEL ** -0.5),
        "ffn2_w_up": nrm(ks[25], (Dp, D_MODEL, D_FF), D_MODEL ** -0.5),
        "ffn2_w_down": nrm(ks[26], (Dp, D_FF, D_MODEL), BETA * D_FF ** -0.5),
        "ln3_g": 1.0 + nrm(ks[27], (Dp, D_MODEL), 0.02),
        "ln3_b": nrm(ks[28], (Dp, D_MODEL), 0.02),
    }


def reference(x, meta, ffn1_w_gate, ffn1_w_up, ffn1_w_down, ln1_g, ln1_b, w_in, conv_w, conv_b,
              b_if, m_norm_g, lam_q1, lam_k1, lam_q2, lam_k2, a_norm_g, w_bm, w_ba, b_gate, w_out,
              ln2_g, ln2_b, ffn2_w_gate, ffn2_w_up, ffn2_w_down, ln3_g, ln3_b):
    B = x.shape[0]
    h = jnp.concatenate([jnp.broadcast_to(meta.astype(x.dtype), (B, N_META, D_MODEL)), x], axis=1)
    for i in range(DEPTH):
        lam_init = 0.8 - 0.6 * math.exp(-0.3 * i)
        h = layer_norm(ALPHA * h + 0.5 * swiglu(h, ffn1_w_gate[i], ffn1_w_up[i], ffn1_w_down[i]), ln1_g[i], ln1_b[i])
        mix = hybrid_mixer(h, w_in[i], conv_w[i], conv_b[i], b_if[i], m_norm_g[i], lam_q1[i], lam_k1[i],
                           lam_q2[i], lam_k2[i], a_norm_g[i], w_bm[i], w_ba[i], b_gate[i], w_out[i], lam_init)
        h = layer_norm(ALPHA * h + mix, ln2_g[i], ln2_b[i])
        h = layer_norm(ALPHA * h + 0.5 * swiglu(h, ffn2_w_gate[i], ffn2_w_up[i], ffn2_w_down[i]), ln3_g[i], ln3_b[i])
    return h[:, N_META:]
```

```python
import functools
import math

import jax
import jax.numpy as jnp
from jax import lax
from jax.experimental import pallas as pl
from jax.experimental.pallas import tpu as pltpu

F32 = jnp.float32
BF16 = jnp.bfloat16

D_MODEL = 1024
N_META = 16
META_ROWS = 128
M_HEADS = 4
M_DK = 128
M_WIDTH = 512
CONV_W = 4
A_HEADS = 4
A_DK = 64
A_DV = 128
A_WIDTH = 512
D_FF = 2816
FF_CHUNK = 256
N_FF_CHUNKS = D_FF // FF_CHUNK
DEPTH = 2
ALPHA = (2 * DEPTH) ** 0.25
LN_EPS = 1e-5
NEG = -1e30

SEG_QK = (0, 1024)
SEG_V = (1024, 1536)
SEG_O = (1536, 2048)
SEG_IF = (2048, 2176)
SEG_AQ = (2176, 2688)
SEG_AK = (2688, 3200)
SEG_AV = (3200, 3712)
SEG_G = (3712, 5760)
PROJ_COLS = 5760

ROW_TILE = 512
MLSTM_CHUNK = 256
ATTN_BLOCK = 256
VMEM_LIMIT = 56 * 1024 * 1024


def _nt_dot(a, b):
    return lax.dot_general(a, b, (((1,), (1,)), ((), ())), preferred_element_type=F32)


def _layer_norm(y, g, b):
    mu = jnp.mean(y, -1, keepdims=True)
    d = y - mu
    var = jnp.mean(d * d, -1, keepdims=True)
    return d * lax.rsqrt(var + LN_EPS) * g + b


def _ffn_ln_kernel(x_ref, wg_ref, wu_ref, wd_ref, g_ref, b_ref, o_ref):
    x = x_ref[...]
    xb = x.astype(BF16)
    acc = jnp.zeros(x.shape, F32)
    for j in range(N_FF_CHUNKS):
        g = jnp.dot(xb, wg_ref[j], preferred_element_type=F32)
        u = jnp.dot(xb, wu_ref[j], preferred_element_type=F32)
        a = (g * jax.nn.sigmoid(g) * u).astype(BF16)
        acc = acc + jnp.dot(a, wd_ref[j], preferred_element_type=F32)
    o_ref[...] = _layer_norm(ALPHA * x + 0.5 * acc, g_ref[...], b_ref[...])


def _ffn_ln(x, wg3, wu3, wd3, g, b):
    m = x.shape[0]
    tm = min(ROW_TILE, m)
    full3 = lambda i: (0, 0, 0)
    return pl.pallas_call(
        _ffn_ln_kernel,
        out_shape=jax.ShapeDtypeStruct((m, D_MODEL), F32),
        grid=(m // tm,),
        in_specs=[
            pl.BlockSpec((tm, D_MODEL), lambda i: (i, 0)),
            pl.BlockSpec((N_FF_CHUNKS, D_MODEL, FF_CHUNK), full3),
            pl.BlockSpec((N_FF_CHUNKS, D_MODEL, FF_CHUNK), full3),
            pl.BlockSpec((N_FF_CHUNKS, FF_CHUNK, D_MODEL), full3),
            pl.BlockSpec((1, D_MODEL), lambda i: (0, 0)),
            pl.BlockSpec((1, D_MODEL), lambda i: (0, 0)),
        ],
        out_specs=pl.BlockSpec((tm, D_MODEL), lambda i: (i, 0)),
        compiler_params=pltpu.CompilerParams(
            dimension_semantics=("arbitrary",), vmem_limit_bytes=VMEM_LIMIT),
    )(x, wg3, wu3, wd3, g, b)


def _proj_kernel(h_ref, w_ref, bif_ref, bg_ref,
                 qk_ref, v_ref, og_ref, gt_ref, aq_ref, ak_ref, av_ref, gate_ref):
    hb = h_ref[...].astype(BF16)

    def seg(s):
        return jnp.dot(hb, w_ref[:, s[0]:s[1]], preferred_element_type=F32)

    qk_ref[...] = seg(SEG_QK)
    v_ref[...] = seg(SEG_V).astype(BF16)
    og_ref[...] = jax.nn.sigmoid(seg(SEG_O)).astype(BF16)
    z = seg(SEG_IF) + bif_ref[...]
    col = lax.broadcasted_iota(jnp.int32, z.shape, 1)
    log_f = jnp.minimum(z, 0.0) - jnp.log1p(jnp.exp(-jnp.abs(z)))
    gt_ref[...] = jnp.where(col < M_HEADS, z, jnp.where(col < 2 * M_HEADS, log_f, 0.0))
    aq_ref[...] = (seg(SEG_AQ) * (A_DK ** -0.5)).astype(BF16)
    ak_ref[...] = seg(SEG_AK).astype(BF16)
    av_ref[...] = seg(SEG_AV).astype(BF16)
    gate_ref[...] = jax.nn.sigmoid(seg(SEG_G) + bg_ref[...]).astype(BF16)


def _proj(h, w_all, b_if, b_gate):
    m = h.shape[0]
    tm = min(ROW_TILE, m)
    row = lambda n: pl.BlockSpec((tm, n), lambda i: (i, 0))
    const = lambda r, n: pl.BlockSpec((r, n), lambda i: (0, 0))
    out_shape = (
        jax.ShapeDtypeStruct((m, 1024), F32),
        jax.ShapeDtypeStruct((m, 512), BF16),
        jax.ShapeDtypeStruct((m, 512), BF16),
        jax.ShapeDtypeStruct((m, 128), F32),
        jax.ShapeDtypeStruct((m, 512), BF16),
        jax.ShapeDtypeStruct((m, 512), BF16),
        jax.ShapeDtypeStruct((m, 512), BF16),
        jax.ShapeDtypeStruct((m, 2048), BF16),
    )
    return pl.pallas_call(
        _proj_kernel,
        out_shape=out_shape,
        grid=(m // tm,),
        in_specs=[row(D_MODEL), const(D_MODEL, PROJ_COLS), const(1, 128), const(1, 2048)],
        out_specs=(row(1024), row(512), row(512), row(128), row(512), row(512), row(512), row(2048)),
        compiler_params=pltpu.CompilerParams(
            dimension_semantics=("arbitrary",), vmem_limit_bytes=VMEM_LIMIT),
    )(h, w_all, b_if, b_gate)


def _mlstm_kernel(qk_ref, v_ref, og_ref, gt_ref, cw_ref, cb_ref, ng_ref,
                  c0_ref, n0_ref, m0_ref, t0_ref,
                  h_ref, c_ref, n_ref, m_ref, tail_ref, buf_ref, *, T):
    @pl.when(pl.program_id(1) == 0)
    def _():
        c_ref[...] = c0_ref[...]
        n_ref[...] = n0_ref[...]
        m_ref[...] = m0_ref[...]
        tail_ref[...] = t0_ref[...]

    buf_ref[0:8, :] = tail_ref[0]
    buf_ref[8:8 + T, :] = qk_ref[...]
    y = cb_ref[...]
    for j in range(CONV_W):
        y = y + cw_ref[j:j + 1, :] * buf_ref[5 + j:5 + j + T, :]
    tail_ref[0] = buf_ref[T:T + 8, :]
    qk = y * jax.nn.sigmoid(y)

    gates = gt_ref[...]
    r = lax.broadcasted_iota(jnp.int32, (T, T), 0)
    c = lax.broadcasted_iota(jnp.int32, (T, T), 1)
    causal = c <= r
    tri = jnp.where(causal, 1.0, 0.0).astype(F32)
    cums = jnp.dot(tri, gates, preferred_element_type=F32, precision=lax.Precision.HIGHEST)
    gates_t = gates.T
    cums_t = cums.T

    for h in range(M_HEADS):
        sl = slice(h * M_DK, (h + 1) * M_DK)
        q_h = qk[:, sl]
        k_h = qk[:, M_WIDTH + h * M_DK:M_WIDTH + (h + 1) * M_DK] * (M_DK ** -0.5)
        qb = q_h.astype(BF16)
        kb = k_h.astype(BF16)
        v_h = v_ref[:, sl]
        li_row = gates_t[h:h + 1, :]
        b_row = cums_t[M_HEADS + h:M_HEADS + h + 1, :]
        li_col = gates[:, h:h + 1]
        b_col = cums[:, M_HEADS + h:M_HEADS + h + 1]
        m_prev = m_ref[0, h:h + 1, 0:1]
        c_prev = c_ref[0, h]
        n_prev = n_ref[0, h:h + 1, :]

        a = b_col + m_prev
        dmat = jnp.where(causal, (b_col - b_row) + li_row, NEG)
        m_t = jnp.maximum(a, jnp.max(dmat, -1, keepdims=True))
        s = _nt_dot(qb, kb) * jnp.exp(dmat - m_t)
        w_state = jnp.exp(a - m_t)
        num = (jnp.dot(s.astype(BF16), v_h, preferred_element_type=F32)
               + w_state * jnp.dot(qb, c_prev.astype(BF16), preferred_element_type=F32))
        den = (jnp.sum(s, -1, keepdims=True)
               + w_state * jnp.sum(q_h * n_prev, -1, keepdims=True))
        hh = num / jnp.maximum(jnp.abs(den), jnp.exp(-m_t))
        mu = jnp.mean(hh, -1, keepdims=True)
        d = hh - mu
        var = jnp.mean(d * d, -1, keepdims=True)
        hn = d * lax.rsqrt(var + LN_EPS)
        h_ref[:, sl] = (hn * ng_ref[:, sl] * og_ref[:, sl].astype(F32)).astype(BF16)

        g_end = b_col[T - 1:T, :]
        a_end = g_end + m_prev
        d_end = (g_end - b_col) + li_col
        m_new = jnp.maximum(a_end, jnp.max(d_end, 0, keepdims=True))
        ws = jnp.exp(d_end - m_new)
        decay = jnp.exp(a_end - m_new)
        kw = k_h * ws
        c_ref[0, h] = decay * c_prev + jnp.dot(kw.T.astype(BF16), v_h, preferred_element_type=F32)
        n_ref[0, h:h + 1, :] = decay * n_prev + jnp.sum(kw, 0, keepdims=True)
        m_ref[0, h:h + 1, :] = jnp.broadcast_to(m_new, (1, 128))


def _mlstm(qk, v, og, gates, conv_w, conv_b, norm_g, state, *, batch, chunk):
    m = qk.shape[0]
    nc = m // (batch * chunk)
    c0, n0, m0, t0 = state
    row = lambda n: pl.BlockSpec((chunk, n), lambda b, c: (b * nc + c, 0))
    const2 = lambda r, n: pl.BlockSpec((r, n), lambda b, c: (0, 0))
    init3 = lambda r, n: pl.BlockSpec((1, r, n), lambda b, c: (0, 0, 0))
    out3 = lambda r, n: pl.BlockSpec((1, r, n), lambda b, c: (b, 0, 0))
    out_shape = (
        jax.ShapeDtypeStruct((m, M_WIDTH), BF16),
        jax.ShapeDtypeStruct((batch, M_HEADS, M_DK, M_DK), F32),
        jax.ShapeDtypeStruct((batch, 8, 128), F32),
        jax.ShapeDtypeStruct((batch, 8, 128), F32),
        jax.ShapeDtypeStruct((batch, 8, 1024), F32),
    )
    return pl.pallas_call(
        functools.partial(_mlstm_kernel, T=chunk),
        out_shape=out_shape,
        grid=(batch, nc),
        in_specs=[
            row(1024), row(512), row(512), row(128),
            const2(8, 1024), const2(1, 1024), const2(1, 512),
            pl.BlockSpec((1, M_HEADS, M_DK, M_DK), lambda b, c: (0, 0, 0, 0)),
            init3(8, 128), init3(8, 128), init3(8, 1024),
        ],
        out_specs=(
            row(512),
            pl.BlockSpec((1, M_HEADS, M_DK, M_DK), lambda b, c: (b, 0, 0, 0)),
            out3(8, 128), out3(8, 128), out3(8, 1024),
        ),
        scratch_shapes=[pltpu.VMEM((chunk + 8, 1024), F32)],
        compiler_params=pltpu.CompilerParams(
            dimension_semantics=("arbitrary", "arbitrary"), vmem_limit_bytes=VMEM_LIMIT),
    )(qk, v, og, gates, conv_w, conv_b, norm_g, c0, n0, m0, t0)


def _stack_maps(q):
    lane = lax.broadcasted_iota(jnp.int32, q.shape, 1)
    zero = jnp.zeros_like(q)
    return jnp.concatenate([jnp.where(lane < A_DK, q, zero), jnp.where(lane >= A_DK, q, zero)], axis=0)


def _diff_finish(acc, l, lam_ref, ng, lam_init, t):
    lam = (jnp.exp(jnp.sum(lam_ref[0:1, :] * lam_ref[1:2, :], -1, keepdims=True))
           - jnp.exp(jnp.sum(lam_ref[2:3, :] * lam_ref[3:4, :], -1, keepdims=True)) + lam_init)
    o = acc[:t] / l[:t] - lam * (acc[t:] / l[t:])
    o = o * lax.rsqrt(jnp.mean(o * o, -1, keepdims=True) + LN_EPS) * (1.0 - lam_init)
    return (o * ng).astype(BF16)


def _attn_kernel(q_ref, k_ref, v_ref, km_ref, vm_ref, lam_ref, sl_ref, ng_ref, o_ref,
                 m_s, l_s, acc_s, *, T, lam_init):
    i = pl.program_id(2)
    slope = sl_ref[:, 0:1]
    qs = _stack_maps(q_ref[...])
    q0 = i * T

    cm = lax.broadcasted_iota(jnp.int32, (1, META_ROWS), 1)
    bias_m = jnp.where(cm >= META_ROWS - N_META, slope * (cm - (META_ROWS + q0)).astype(F32), NEG)
    s = _nt_dot(qs, km_ref[...]) + bias_m
    m0 = jnp.max(s, -1, keepdims=True)
    p = jnp.exp(s - m0)
    m_s[...] = m0
    l_s[...] = jnp.sum(p, -1, keepdims=True)
    acc_s[...] = jnp.dot(p.astype(BF16), vm_ref[...], preferred_element_type=F32)

    ck = lax.broadcasted_iota(jnp.int32, (1, T), 1)

    def block(j, masked):
        start = pl.multiple_of(j * T, T)
        kb = k_ref[pl.ds(start, T), :]
        vb = v_ref[pl.ds(start, T), :]
        s = _nt_dot(qs, kb) + slope * (ck + (j * T - q0)).astype(F32)
        if masked:
            r = lax.broadcasted_iota(jnp.int32, (T, T), 0)
            c = lax.broadcasted_iota(jnp.int32, (T, T), 1)
            keep = c <= r
            s = jnp.where(jnp.concatenate([keep, keep], axis=0), s, NEG)
        m_old = m_s[...]
        m_new = jnp.maximum(m_old, jnp.max(s, -1, keepdims=True))
        alpha = jnp.exp(m_old - m_new)
        p = jnp.exp(s - m_new)
        l_s[...] = alpha * l_s[...] + jnp.sum(p, -1, keepdims=True)
        acc_s[...] = alpha * acc_s[...] + jnp.dot(p.astype(BF16), vb, preferred_element_type=F32)
        m_s[...] = m_new

    def body(j, carry):
        block(j, False)
        return carry

    lax.fori_loop(0, i, body, 0)
    block(i, True)
    o_ref[...] = _diff_finish(acc_s[...], l_s[...], lam_ref, ng_ref[...], lam_init, T)


def _attn(aq, ak, av, ak_meta, av_meta, lam_rows, slopes, norm_g, *, batch, lam_init):
    m = aq.shape[0]
    seq = m // batch
    t = min(ATTN_BLOCK, seq)
    nq = seq // t
    qspec = pl.BlockSpec((t, 128), lambda b, h, i: (b * nq + i, h))
    kvspec = pl.BlockSpec((seq, 128), lambda b, h, i: (b, h))
    mspec = pl.BlockSpec((META_ROWS, 128), lambda b, h, i: (0, h))
    hrow = pl.BlockSpec((1, 128), lambda b, h, i: (0, h))
    return pl.pallas_call(
        functools.partial(_attn_kernel, T=t, lam_init=lam_init),
        out_shape=jax.ShapeDtypeStruct((m, A_WIDTH), BF16),
        grid=(batch, A_HEADS, nq),
        in_specs=[qspec, kvspec, kvspec, mspec, mspec,
                  pl.BlockSpec((8, 128), lambda b, h, i: (0, 0)), hrow, hrow],
        out_specs=qspec,
        scratch_shapes=[pltpu.VMEM((2 * t, 1), F32), pltpu.VMEM((2 * t, 1), F32),
                        pltpu.VMEM((2 * t, A_DV), F32)],
        compiler_params=pltpu.CompilerParams(
            dimension_semantics=("arbitrary", "arbitrary", "arbitrary"),
            vmem_limit_bytes=VMEM_LIMIT),
    )(aq, ak, av, ak_meta, av_meta, lam_rows, slopes, norm_g)


def _attn_meta_kernel(q_ref, k_ref, v_ref, lam_ref, sl_ref, ng_ref, o_ref, *, lam_init):
    t = META_ROWS
    slope = sl_ref[:, 0:1]
    qs = _stack_maps(q_ref[...])
    r = lax.broadcasted_iota(jnp.int32, (t, t), 0)
    c = lax.broadcasted_iota(jnp.int32, (t, t), 1)
    bias = jnp.where((c >= t - N_META) & (c <= r), slope * (c - r).astype(F32), NEG)
    s = _nt_dot(qs, k_ref[...]) + jnp.concatenate([bias, bias], axis=0)
    p = jnp.exp(s - jnp.max(s, -1, keepdims=True))
    l = jnp.sum(p, -1, keepdims=True)
    acc = jnp.dot(p.astype(BF16), v_ref[...], preferred_element_type=F32)
    o_ref[...] = _diff_finish(acc, l, lam_ref, ng_ref[...], lam_init, t)


def _attn_meta(aq, ak, av, lam_rows, slopes, norm_g, *, lam_init):
    blk = pl.BlockSpec((META_ROWS, 128), lambda h: (0, h))
    hrow = pl.BlockSpec((1, 128), lambda h: (0, h))
    return pl.pallas_call(
        functools.partial(_attn_meta_kernel, lam_init=lam_init),
        out_shape=jax.ShapeDtypeStruct((META_ROWS, A_WIDTH), BF16),
        grid=(A_HEADS,),
        in_specs=[blk, blk, blk, pl.BlockSpec((8, 128), lambda h: (0, 0)), hrow, hrow],
        out_specs=blk,
        compiler_params=pltpu.CompilerParams(dimension_semantics=("arbitrary",)),
    )(aq, ak, av, lam_rows, slopes, norm_g)


def _merge_kernel(h_ref, hm_ref, ha_ref, gate_ref, wbm_ref, wba_ref, wo_ref, g_ref, b_ref, o_ref):
    ym = jnp.dot(hm_ref[...], wbm_ref[...], preferred_element_type=F32)
    ya = jnp.dot(ha_ref[...], wba_ref[...], preferred_element_type=F32)
    gm = gate_ref[:, 0:D_MODEL].astype(F32)
    ga = gate_ref[:, D_MODEL:2 * D_MODEL].astype(F32)
    mixed = (gm * ym + ga * ya).astype(BF16)
    mix = jnp.dot(mixed, wo_ref[...], preferred_element_type=F32)
    o_ref[...] = _layer_norm(ALPHA * h_ref[...] + mix, g_ref[...], b_ref[...])


def _merge(h, hm, ha, gate, w_bm, w_ba, w_out, g, b):
    m = h.shape[0]
    tm = min(ROW_TILE, m)
    row = lambda n: pl.BlockSpec((tm, n), lambda i: (i, 0))
    const = lambda r, n: pl.BlockSpec((r, n), lambda i: (0, 0))
    return pl.pallas_call(
        _merge_kernel,
        out_shape=jax.ShapeDtypeStruct((m, D_MODEL), F32),
        grid=(m // tm,),
        in_specs=[row(D_MODEL), row(512), row(512), row(2048),
                  const(512, D_MODEL), const(512, D_MODEL), const(D_MODEL, D_MODEL),
                  const(1, D_MODEL), const(1, D_MODEL)],
        out_specs=row(D_MODEL),
        compiler_params=pltpu.CompilerParams(
            dimension_semantics=("arbitrary",), vmem_limit_bytes=VMEM_LIMIT),
    )(h, hm, ha, gate, w_bm, w_ba, w_out, g, b)


def _ffn_weights(w_gate, w_up, w_down):
    wg3 = w_gate.astype(BF16).reshape(D_MODEL, N_FF_CHUNKS, FF_CHUNK).transpose(1, 0, 2)
    wu3 = w_up.astype(BF16).reshape(D_MODEL, N_FF_CHUNKS, FF_CHUNK).transpose(1, 0, 2)
    wd3 = w_down.astype(BF16).reshape(N_FF_CHUNKS, FF_CHUNK, D_MODEL)
    return wg3, wu3, wd3


def _proj_weights(w_in, b_if, b_gate):
    w = w_in.astype(BF16)
    w_all = jnp.concatenate(
        [w[:, :2048], jnp.pad(w[:, 2048:2056], ((0, 0), (0, 120))), w[:, 2056:]], axis=1)
    return w_all, jnp.pad(b_if, (0, 120)).reshape(1, 128), b_gate.reshape(1, 2 * D_MODEL)


def kernel(x, meta, ffn1_w_gate, ffn1_w_up, ffn1_w_down, ln1_g, ln1_b, w_in, conv_w, conv_b, b_if, m_norm_g, lam_q1, lam_k1, lam_q2, lam_k2, a_norm_g, w_bm, w_ba, b_gate, w_out, ln2_g, ln2_b, ffn2_w_gate, ffn2_w_up, ffn2_w_down, ln3_g, ln3_b):
    batch, seq, _ = x.shape
    rows = batch * seq
    hr = x.reshape(rows, D_MODEL)
    hm = jnp.pad(meta.astype(x.dtype), ((META_ROWS - N_META, 0), (0, 0)))

    slopes = jnp.repeat(
        jnp.array([2.0 ** (-8.0 * (h + 1) / A_HEADS) for h in range(A_HEADS)], F32), 128).reshape(1, 512)
    is_meta = (jnp.arange(META_ROWS) >= META_ROWS - N_META)[:, None]
    null_gate = jnp.where(jnp.arange(128) < M_HEADS, NEG, 0.0).astype(F32)[None, :]
    state0 = (jnp.zeros((1, M_HEADS, M_DK, M_DK), F32), jnp.zeros((1, 8, 128), F32),
              jnp.full((1, 8, 128), NEG, F32), jnp.zeros((1, 8, 1024), F32))
    vec = lambda a: a.reshape(1, -1)

    for i in range(DEPTH):
        lam_init = 0.8 - 0.6 * math.exp(-0.3 * i)
        f1 = _ffn_weights(ffn1_w_gate[i], ffn1_w_up[i], ffn1_w_down[i])
        f2 = _ffn_weights(ffn2_w_gate[i], ffn2_w_up[i], ffn2_w_down[i])
        w_all, bif, bg = _proj_weights(w_in[i], b_if[i], b_gate[i])
        cw = jnp.pad(conv_w[i], ((0, 8 - CONV_W), (0, 0)))
        cb = vec(conv_b[i])
        lam_rows = jnp.pad(jnp.stack([lam_q1[i], lam_k1[i], lam_q2[i], lam_k2[i]]).astype(F32),
                           ((0, 4), (0, 128 - A_DK)))
        wbm, wba, wo = w_bm[i].astype(BF16), w_ba[i].astype(BF16), w_out[i].astype(BF16)

        hr = _ffn_ln(hr, *f1, vec(ln1_g[i]), vec(ln1_b[i]))
        hm = _ffn_ln(hm, *f1, vec(ln1_g[i]), vec(ln1_b[i]))

        qk_r, v_r, og_r, gt_r, aq_r, ak_r, av_r, gate_r = _proj(hr, w_all, bif, bg)
        qk_m, v_m, og_m, gt_m, aq_m, ak_m, av_m, gate_m = _proj(hm, w_all, bif, bg)
        qk_m = jnp.where(is_meta, qk_m, 0.0)
        gt_m = jnp.where(is_meta, gt_m, null_gate)

        xm_m, *state = _mlstm(qk_m, v_m, og_m, gt_m, cw, cb, vec(m_norm_g[i]), state0,
                              batch=1, chunk=META_ROWS)
        xm_r = _mlstm(qk_r, v_r, og_r, gt_r, cw, cb, vec(m_norm_g[i]), tuple(state),
                      batch=batch, chunk=min(MLSTM_CHUNK, seq))[0]

        xa_m = _attn_meta(aq_m, ak_m, av_m, lam_rows, slopes, vec(a_norm_g[i]), lam_init=lam_init)
        xa_r = _attn(aq_r, ak_r, av_r, ak_m, av_m, lam_rows, slopes, vec(a_norm_g[i]),
                     batch=batch, lam_init=lam_init)

        hr = _merge(hr, xm_r, xa_r, gate_r, wbm, wba, wo, vec(ln2_g[i]), vec(ln2_b[i]))
        hm = _merge(hm, xm_m, xa_m, gate_m, wbm, wba, wo, vec(ln2_g[i]), vec(ln2_b[i]))

        hr = _ffn_ln(hr, *f2, vec(ln3_g[i]), vec(ln3_b[i]))
        hm = _ffn_ln(hm, *f2, vec(ln3_g[i]), vec(ln3_b[i]))

    return hr.reshape(batch, seq, D_MODEL)
```

```python
import functools
import math

import jax
import jax.numpy as jnp
from jax import lax
from jax.experimental import pallas as pl
from jax.experimental.pallas import tpu as pltpu

F32 = jnp.float32
BF16 = jnp.bfloat16

D_MODEL = 1024
N_META = 16
META_ROWS = 128
M_HEADS = 4
M_DK = 128
M_WIDTH = 512
CONV_W = 4
A_HEADS = 4
A_DK = 64
A_DV = 128
A_WIDTH = 512
D_FF = 2816
FF_CHUNK = 256
N_FF_CHUNKS = D_FF // FF_CHUNK
DEPTH = 2
ALPHA = (2 * DEPTH) ** 0.25
LN_EPS = 1e-5
NEG = -1e30

SEG_QK = (0, 1024)
SEG_V = (1024, 1536)
SEG_O = (1536, 2048)
SEG_IF = (2048, 2176)
SEG_AQ = (2176, 2688)
SEG_AK = (2688, 3200)
SEG_AV = (3200, 3712)
SEG_G = (3712, 5760)
PROJ_COLS = 5760

ROW_TILE = 512
MLSTM_CHUNK = 256
ATTN_BLOCK = 512
LOG2E = math.log2(math.e)
VMEM_LIMIT = 56 * 1024 * 1024


def _nt_dot(a, b):
    return lax.dot_general(a, b, (((1,), (1,)), ((), ())), preferred_element_type=F32)


def _layer_norm(y, g, b):
    mu = jnp.mean(y, -1, keepdims=True)
    d = y - mu
    var = jnp.mean(d * d, -1, keepdims=True)
    return d * lax.rsqrt(var + LN_EPS) * g + b


def _ffn_ln_kernel(x_ref, wg_ref, wu_ref, wd_ref, g_ref, b_ref, o_ref):
    x = x_ref[...]
    xb = x.astype(BF16)
    acc = jnp.zeros(x.shape, F32)
    for j in range(N_FF_CHUNKS):
        g = jnp.dot(xb, wg_ref[j], preferred_element_type=F32)
        u = jnp.dot(xb, wu_ref[j], preferred_element_type=F32)
        a = (g * jax.nn.sigmoid(g) * u).astype(BF16)
        acc = acc + jnp.dot(a, wd_ref[j], preferred_element_type=F32)
    o_ref[...] = _layer_norm(ALPHA * x + 0.5 * acc, g_ref[...], b_ref[...])


def _ffn_ln(x, wg3, wu3, wd3, g, b):
    m = x.shape[0]
    tm = min(ROW_TILE, m)
    full3 = lambda i: (0, 0, 0)
    return pl.pallas_call(
        _ffn_ln_kernel,
        out_shape=jax.ShapeDtypeStruct((m, D_MODEL), F32),
        grid=(m // tm,),
        in_specs=[
            pl.BlockSpec((tm, D_MODEL), lambda i: (i, 0)),
            pl.BlockSpec((N_FF_CHUNKS, D_MODEL, FF_CHUNK), full3),
            pl.BlockSpec((N_FF_CHUNKS, D_MODEL, FF_CHUNK), full3),
            pl.BlockSpec((N_FF_CHUNKS, FF_CHUNK, D_MODEL), full3),
            pl.BlockSpec((1, D_MODEL), lambda i: (0, 0)),
            pl.BlockSpec((1, D_MODEL), lambda i: (0, 0)),
        ],
        out_specs=pl.BlockSpec((tm, D_MODEL), lambda i: (i, 0)),
        compiler_params=pltpu.CompilerParams(
            dimension_semantics=("arbitrary",), vmem_limit_bytes=VMEM_LIMIT),
    )(x, wg3, wu3, wd3, g, b)


def _proj_kernel(h_ref, w_ref, bif_ref, bg_ref,
                 qk_ref, v_ref, og_ref, gt_ref, aq_ref, ak_ref, av_ref, gate_ref):
    hb = h_ref[...].astype(BF16)

    def seg(s):
        return jnp.dot(hb, w_ref[:, s[0]:s[1]], preferred_element_type=F32)

    qk_ref[...] = seg(SEG_QK)
    v_ref[...] = seg(SEG_V).astype(BF16)
    og_ref[...] = jax.nn.sigmoid(seg(SEG_O)).astype(BF16)
    z = seg(SEG_IF) + bif_ref[...]
    col = lax.broadcasted_iota(jnp.int32, z.shape, 1)
    log_f = jnp.minimum(z, 0.0) - jnp.log1p(jnp.exp(-jnp.abs(z)))
    gt_ref[...] = jnp.where(col < M_HEADS, z, jnp.where(col < 2 * M_HEADS, log_f, 0.0))
    aq_ref[...] = (seg(SEG_AQ) * (LOG2E * A_DK ** -0.5)).astype(BF16)
    ak_ref[...] = seg(SEG_AK).astype(BF16)
    av_ref[...] = seg(SEG_AV).astype(BF16)
    gate_ref[...] = jax.nn.sigmoid(seg(SEG_G) + bg_ref[...]).astype(BF16)


def _proj(h, w_all, b_if, b_gate):
    m = h.shape[0]
    tm = min(ROW_TILE, m)
    row = lambda n: pl.BlockSpec((tm, n), lambda i: (i, 0))
    const = lambda r, n: pl.BlockSpec((r, n), lambda i: (0, 0))
    out_shape = (
        jax.ShapeDtypeStruct((m, 1024), F32),
        jax.ShapeDtypeStruct((m, 512), BF16),
        jax.ShapeDtypeStruct((m, 512), BF16),
        jax.ShapeDtypeStruct((m, 128), F32),
        jax.ShapeDtypeStruct((m, 512), BF16),
        jax.ShapeDtypeStruct((m, 512), BF16),
        jax.ShapeDtypeStruct((m, 512), BF16),
        jax.ShapeDtypeStruct((m, 2048), BF16),
    )
    return pl.pallas_call(
        _proj_kernel,
        out_shape=out_shape,
        grid=(m // tm,),
        in_specs=[row(D_MODEL), const(D_MODEL, PROJ_COLS), const(1, 128), const(1, 2048)],
        out_specs=(row(1024), row(512), row(512), row(128), row(512), row(512), row(512), row(2048)),
        compiler_params=pltpu.CompilerParams(
            dimension_semantics=("arbitrary",), vmem_limit_bytes=VMEM_LIMIT),
    )(h, w_all, b_if, b_gate)


def _mlstm_kernel(qk_ref, v_ref, og_ref, gt_ref, cw_ref, cb_ref, ng_ref,
                  c0_ref, n0_ref, m0_ref, t0_ref,
                  h_ref, c_ref, n_ref, m_ref, tail_ref, buf_ref, *, T):
    @pl.when(pl.program_id(1) == 0)
    def _():
        c_ref[...] = c0_ref[...]
        n_ref[...] = n0_ref[...]
        m_ref[...] = m0_ref[...]
        tail_ref[...] = t0_ref[...]

    buf_ref[0:8, :] = tail_ref[0]
    buf_ref[8:8 + T, :] = qk_ref[...]
    y = cb_ref[...]
    for j in range(CONV_W):
        y = y + cw_ref[j:j + 1, :] * buf_ref[5 + j:5 + j + T, :]
    tail_ref[0] = buf_ref[T:T + 8, :]
    qk = y * jax.nn.sigmoid(y)

    gates = gt_ref[...]
    r = lax.broadcasted_iota(jnp.int32, (T, T), 0)
    c = lax.broadcasted_iota(jnp.int32, (T, T), 1)
    causal = c <= r
    tri = jnp.where(causal, 1.0, 0.0).astype(F32)
    cums = jnp.dot(tri, gates, preferred_element_type=F32, precision=lax.Precision.HIGHEST)
    gates_t = gates.T
    cums_t = cums.T

    for h in range(M_HEADS):
        sl = slice(h * M_DK, (h + 1) * M_DK)
        q_h = qk[:, sl]
        k_h = qk[:, M_WIDTH + h * M_DK:M_WIDTH + (h + 1) * M_DK] * (M_DK ** -0.5)
        qb = q_h.astype(BF16)
        kb = k_h.astype(BF16)
        v_h = v_ref[:, sl]
        li_row = gates_t[h:h + 1, :]
        b_row = cums_t[M_HEADS + h:M_HEADS + h + 1, :]
        li_col = gates[:, h:h + 1]
        b_col = cums[:, M_HEADS + h:M_HEADS + h + 1]
        m_prev = m_ref[0, h:h + 1, 0:1]
        c_prev = c_ref[0, h]
        n_prev = n_ref[0, h:h + 1, :]

        a = b_col + m_prev
        dmat = jnp.where(causal, (b_col - b_row) + li_row, NEG)
        m_t = jnp.maximum(a, jnp.max(dmat, -1, keepdims=True))
        s = _nt_dot(qb, kb) * jnp.exp(dmat - m_t)
        w_state = jnp.exp(a - m_t)
        num = (jnp.dot(s.astype(BF16), v_h, preferred_element_type=F32)
               + w_state * jnp.dot(qb, c_prev.astype(BF16), preferred_element_type=F32))
        den = (jnp.sum(s, -1, keepdims=True)
               + w_state * jnp.sum(q_h * n_prev, -1, keepdims=True))
        hh = num / jnp.maximum(jnp.abs(den), jnp.exp(-m_t))
        mu = jnp.mean(hh, -1, keepdims=True)
        d = hh - mu
        var = jnp.mean(d * d, -1, keepdims=True)
        hn = d * lax.rsqrt(var + LN_EPS)
        h_ref[:, sl] = (hn * ng_ref[:, sl] * og_ref[:, sl].astype(F32)).astype(BF16)

        g_end = b_col[T - 1:T, :]
        a_end = g_end + m_prev
        d_end = (g_end - b_col) + li_col
        m_new = jnp.maximum(a_end, jnp.max(d_end, 0, keepdims=True))
        ws = jnp.exp(d_end - m_new)
        decay = jnp.exp(a_end - m_new)
        kw = k_h * ws
        c_ref[0, h] = decay * c_prev + jnp.dot(kw.T.astype(BF16), v_h, preferred_element_type=F32)
        n_ref[0, h:h + 1, :] = decay * n_prev + jnp.sum(kw, 0, keepdims=True)
        m_ref[0, h:h + 1, :] = jnp.broadcast_to(m_new, (1, 128))


def _mlstm(qk, v, og, gates, conv_w, conv_b, norm_g, state, *, batch, chunk):
    m = qk.shape[0]
    nc = m // (batch * chunk)
    c0, n0, m0, t0 = state
    row = lambda n: pl.BlockSpec((chunk, n), lambda b, c: (b * nc + c, 0))
    const2 = lambda r, n: pl.BlockSpec((r, n), lambda b, c: (0, 0))
    init3 = lambda r, n: pl.BlockSpec((1, r, n), lambda b, c: (0, 0, 0))
    out3 = lambda r, n: pl.BlockSpec((1, r, n), lambda b, c: (b, 0, 0))
    out_shape = (
        jax.ShapeDtypeStruct((m, M_WIDTH), BF16),
        jax.ShapeDtypeStruct((batch, M_HEADS, M_DK, M_DK), F32),
        jax.ShapeDtypeStruct((batch, 8, 128), F32),
        jax.ShapeDtypeStruct((batch, 8, 128), F32),
        jax.ShapeDtypeStruct((batch, 8, 1024), F32),
    )
    return pl.pallas_call(
        functools.partial(_mlstm_kernel, T=chunk),
        out_shape=out_shape,
        grid=(batch, nc),
        in_specs=[
            row(1024), row(512), row(512), row(128),
            const2(8, 1024), const2(1, 1024), const2(1, 512),
            pl.BlockSpec((1, M_HEADS, M_DK, M_DK), lambda b, c: (0, 0, 0, 0)),
            init3(8, 128), init3(8, 128), init3(8, 1024),
        ],
        out_specs=(
            row(512),
            pl.BlockSpec((1, M_HEADS, M_DK, M_DK), lambda b, c: (b, 0, 0, 0)),
            out3(8, 128), out3(8, 128), out3(8, 1024),
        ),
        scratch_shapes=[pltpu.VMEM((chunk + 8, 1024), F32)],
        compiler_params=pltpu.CompilerParams(
            dimension_semantics=("arbitrary", "arbitrary"), vmem_limit_bytes=VMEM_LIMIT),
    )(qk, v, og, gates, conv_w, conv_b, norm_g, c0, n0, m0, t0)


def _stack_maps(q):
    lane = lax.broadcasted_iota(jnp.int32, q.shape, 1)
    zero = jnp.zeros_like(q)
    return jnp.concatenate([jnp.where(lane < A_DK, q, zero), jnp.where(lane >= A_DK, q, zero)], axis=0)


def _diff_finish(acc, l, lam_ref, ng, lam_init, t):
    lam = (jnp.exp(jnp.sum(lam_ref[0:1, :] * lam_ref[1:2, :], -1, keepdims=True))
           - jnp.exp(jnp.sum(lam_ref[2:3, :] * lam_ref[3:4, :], -1, keepdims=True)) + lam_init)
    o = acc[:t] / l[:t] - lam * (acc[t:] / l[t:])
    o = o * lax.rsqrt(jnp.mean(o * o, -1, keepdims=True) + LN_EPS) * (1.0 - lam_init)
    return (o * ng).astype(BF16)


def _online_softmax_update(s, vb, m_s, l_s, acc_s):
    tiles = [s[:, t * 128:(t + 1) * 128] for t in range(s.shape[1] // 128)]
    part = functools.reduce(jnp.maximum, tiles)
    m_old = m_s[...]
    m_new = jnp.maximum(m_old, jnp.max(part, -1, keepdims=True))
    alpha = jnp.exp2(m_old - m_new)
    ps = [jnp.exp2(t - m_new) for t in tiles]
    l_s[...] = alpha * l_s[...] + functools.reduce(jnp.add, ps)
    p = jnp.concatenate([x.astype(BF16) for x in ps], axis=1)
    acc_s[...] = alpha * acc_s[...] + jnp.dot(p, vb, preferred_element_type=F32)
    m_s[...] = m_new


def _attn_kernel(q_ref, k_ref, v_ref, km_ref, vm_ref, lam_ref, sl_ref, ng_ref, o_ref,
                 m_s, l_s, acc_s, *, T, lam_init):
    i = pl.program_id(2)
    slope = sl_ref[:, 0:1]
    qs = _stack_maps(q_ref[...])
    q0 = i * T

    m_s[...] = jnp.full(m_s.shape, NEG, F32)
    l_s[...] = jnp.zeros(l_s.shape, F32)
    acc_s[...] = jnp.zeros(acc_s.shape, F32)

    cm = lax.broadcasted_iota(jnp.int32, (1, META_ROWS), 1)
    bias_m = jnp.where(cm >= META_ROWS - N_META, slope * (cm - (META_ROWS + q0)).astype(F32), NEG)
    _online_softmax_update(_nt_dot(qs, km_ref[...]) + bias_m, vm_ref[...], m_s, l_s, acc_s)

    ck = lax.broadcasted_iota(jnp.int32, (1, T), 1)

    def logits(j):
        kb = k_ref[pl.ds(pl.multiple_of(j * T, T), T), :]
        return _nt_dot(qs, kb) + slope * (ck + (j * T - q0)).astype(F32)

    def values(j):
        return v_ref[pl.ds(pl.multiple_of(j * T, T), T), :]

    def pair(jj, carry):
        s_a = logits(2 * jj)
        s_b = logits(2 * jj + 1)
        _online_softmax_update(s_a, values(2 * jj), m_s, l_s, acc_s)
        _online_softmax_update(s_b, values(2 * jj + 1), m_s, l_s, acc_s)
        return carry

    lax.fori_loop(0, i // 2, pair, 0)

    @pl.when(i % 2 == 1)
    def _():
        _online_softmax_update(logits(i - 1), values(i - 1), m_s, l_s, acc_s)

    r = lax.broadcasted_iota(jnp.int32, (T, T), 0)
    c = lax.broadcasted_iota(jnp.int32, (T, T), 1)
    keep = c <= r
    s = jnp.where(jnp.concatenate([keep, keep], axis=0), logits(i), NEG)
    _online_softmax_update(s, values(i), m_s, l_s, acc_s)
    l = jnp.sum(l_s[...], -1, keepdims=True)
    o_ref[...] = _diff_finish(acc_s[...], l, lam_ref, ng_ref[...], lam_init, T)


def _attn(aq, ak, av, ak_meta, av_meta, lam_rows, slopes, norm_g, *, batch, lam_init):
    m = aq.shape[0]
    seq = m // batch
    t = min(ATTN_BLOCK, seq)
    nq = seq // t
    qspec = pl.BlockSpec((t, 128), lambda b, h, i: (b * nq + i, h))
    kvspec = pl.BlockSpec((seq, 128), lambda b, h, i: (b, h))
    mspec = pl.BlockSpec((META_ROWS, 128), lambda b, h, i: (0, h))
    hrow = pl.BlockSpec((1, 128), lambda b, h, i: (0, h))
    return pl.pallas_call(
        functools.partial(_attn_kernel, T=t, lam_init=lam_init),
        out_shape=jax.ShapeDtypeStruct((m, A_WIDTH), BF16),
        grid=(batch, A_HEADS, nq),
        in_specs=[qspec, kvspec, kvspec, mspec, mspec,
                  pl.BlockSpec((8, 128), lambda b, h, i: (0, 0)), hrow, hrow],
        out_specs=qspec,
        scratch_shapes=[pltpu.VMEM((2 * t, 128), F32), pltpu.VMEM((2 * t, 128), F32),
                        pltpu.VMEM((2 * t, A_DV), F32)],
        compiler_params=pltpu.CompilerParams(
            dimension_semantics=("arbitrary", "arbitrary", "arbitrary"),
            vmem_limit_bytes=VMEM_LIMIT),
    )(aq, ak, av, ak_meta, av_meta, lam_rows, slopes, norm_g)


def _attn_meta_kernel(q_ref, k_ref, v_ref, lam_ref, sl_ref, ng_ref, o_ref, *, lam_init):
    t = META_ROWS
    slope = sl_ref[:, 0:1]
    qs = _stack_maps(q_ref[...])
    r = lax.broadcasted_iota(jnp.int32, (t, t), 0)
    c = lax.broadcasted_iota(jnp.int32, (t, t), 1)
    bias = jnp.where((c >= t - N_META) & (c <= r), slope * (c - r).astype(F32), NEG)
    s = _nt_dot(qs, k_ref[...]) + jnp.concatenate([bias, bias], axis=0)
    p = jnp.exp2(s - jnp.max(s, -1, keepdims=True))
    l = jnp.sum(p, -1, keepdims=True)
    acc = jnp.dot(p.astype(BF16), v_ref[...], preferred_element_type=F32)
    o_ref[...] = _diff_finish(acc, l, lam_ref, ng_ref[...], lam_init, t)


def _attn_meta(aq, ak, av, lam_rows, slopes, norm_g, *, lam_init):
    blk = pl.BlockSpec((META_ROWS, 128), lambda h: (0, h))
    hrow = pl.BlockSpec((1, 128), lambda h: (0, h))
    return pl.pallas_call(
        functools.partial(_attn_meta_kernel, lam_init=lam_init),
        out_shape=jax.ShapeDtypeStruct((META_ROWS, A_WIDTH), BF16),
        grid=(A_HEADS,),
        in_specs=[blk, blk, blk, pl.BlockSpec((8, 128), lambda h: (0, 0)), hrow, hrow],
        out_specs=blk,
        compiler_params=pltpu.CompilerParams(dimension_semantics=("arbitrary",)),
    )(aq, ak, av, lam_rows, slopes, norm_g)


def _merge_kernel(h_ref, hm_ref, ha_ref, gate_ref, wbm_ref, wba_ref, wo_ref, g_ref, b_ref, o_ref):
    ym = jnp.dot(hm_ref[...], wbm_ref[...], preferred_element_type=F32)
    ya = jnp.dot(ha_ref[...], wba_ref[...], preferred_element_type=F32)
    gm = gate_ref[:, 0:D_MODEL].astype(F32)
    ga = gate_ref[:, D_MODEL:2 * D_MODEL].astype(F32)
    mixed = (gm * ym + ga * ya).astype(BF16)
    mix = jnp.dot(mixed, wo_ref[...], preferred_element_type=F32)
    o_ref[...] = _layer_norm(ALPHA * h_ref[...] + mix, g_ref[...], b_ref[...])


def _merge(h, hm, ha, gate, w_bm, w_ba, w_out, g, b):
    m = h.shape[0]
    tm = min(ROW_TILE, m)
    row = lambda n: pl.BlockSpec((tm, n), lambda i: (i, 0))
    const = lambda r, n: pl.BlockSpec((r, n), lambda i: (0, 0))
    return pl.pallas_call(
        _merge_kernel,
        out_shape=jax.ShapeDtypeStruct((m, D_MODEL), F32),
        grid=(m // tm,),
        in_specs=[row(D_MODEL), row(512), row(512), row(2048),
                  const(512, D_MODEL), const(512, D_MODEL), const(D_MODEL, D_MODEL),
                  const(1, D_MODEL), const(1, D_MODEL)],
        out_specs=row(D_MODEL),
        compiler_params=pltpu.CompilerParams(
            dimension_semantics=("arbitrary",), vmem_limit_bytes=VMEM_LIMIT),
    )(h, hm, ha, gate, w_bm, w_ba, w_out, g, b)


def _ffn_weights(w_gate, w_up, w_down):
    wg3 = w_gate.astype(BF16).reshape(D_MODEL, N_FF_CHUNKS, FF_CHUNK).transpose(1, 0, 2)
    wu3 = w_up.astype(BF16).reshape(D_MODEL, N_FF_CHUNKS, FF_CHUNK).transpose(1, 0, 2)
    wd3 = w_down.astype(BF16).reshape(N_FF_CHUNKS, FF_CHUNK, D_MODEL)
    return wg3, wu3, wd3


def _proj_weights(w_in, b_if, b_gate):
    w = w_in.astype(BF16)
    w_all = jnp.concatenate(
        [w[:, :2048], jnp.pad(w[:, 2048:2056], ((0, 0), (0, 120))), w[:, 2056:]], axis=1)
    return w_all, jnp.pad(b_if, (0, 120)).reshape(1, 128), b_gate.reshape(1, 2 * D_MODEL)


def kernel(x, meta, ffn1_w_gate, ffn1_w_up, ffn1_w_down, ln1_g, ln1_b, w_in, conv_w, conv_b, b_if, m_norm_g, lam_q1, lam_k1, lam_q2, lam_k2, a_norm_g, w_bm, w_ba, b_gate, w_out, ln2_g, ln2_b, ffn2_w_gate, ffn2_w_up, ffn2_w_down, ln3_g, ln3_b):
    batch, seq, _ = x.shape
    rows = batch * seq
    hr = x.reshape(rows, D_MODEL)
    hm = jnp.pad(meta.astype(x.dtype), ((META_ROWS - N_META, 0), (0, 0)))

    slopes = jnp.repeat(
        jnp.array([LOG2E * 2.0 ** (-8.0 * (h + 1) / A_HEADS) for h in range(A_HEADS)], F32), 128).reshape(1, 512)
    is_meta = (jnp.arange(META_ROWS) >= META_ROWS - N_META)[:, None]
    null_gate = jnp.where(jnp.arange(128) < M_HEADS, NEG, 0.0).astype(F32)[None, :]
    state0 = (jnp.zeros((1, M_HEADS, M_DK, M_DK), F32), jnp.zeros((1, 8, 128), F32),
              jnp.full((1, 8, 128), NEG, F32), jnp.zeros((1, 8, 1024), F32))
    vec = lambda a: a.reshape(1, -1)

    for i in range(DEPTH):
        lam_init = 0.8 - 0.6 * math.exp(-0.3 * i)
        f1 = _ffn_weights(ffn1_w_gate[i], ffn1_w_up[i], ffn1_w_down[i])
        f2 = _ffn_weights(ffn2_w_gate[i], ffn2_w_up[i], ffn2_w_down[i])
        w_all, bif, bg = _proj_weights(w_in[i], b_if[i], b_gate[i])
        cw = jnp.pad(conv_w[i], ((0, 8 - CONV_W), (0, 0)))
        cb = vec(conv_b[i])
        lam_rows = jnp.pad(jnp.stack([lam_q1[i], lam_k1[i], lam_q2[i], lam_k2[i]]).astype(F32),
                           ((0, 4), (0, 128 - A_DK)))
        wbm, wba, wo = w_bm[i].astype(BF16), w_ba[i].astype(BF16), w_out[i].astype(BF16)

        hr = _ffn_ln(hr, *f1, vec(ln1_g[i]), vec(ln1_b[i]))
        hm = _ffn_ln(hm, *f1, vec(ln1_g[i]), vec(ln1_b[i]))

        qk_r, v_r, og_r, gt_r, aq_r, ak_r, av_r, gate_r = _proj(hr, w_all, bif, bg)
        qk_m, v_m, og_m, gt_m, aq_m, ak_m, av_m, gate_m = _proj(hm, w_all, bif, bg)
        qk_m = jnp.where(is_meta, qk_m, 0.0)
        gt_m = jnp.where(is_meta, gt_m, null_gate)

        xm_m, *state = _mlstm(qk_m, v_m, og_m, gt_m, cw, cb, vec(m_norm_g[i]), state0,
                              batch=1, chunk=META_ROWS)
        xm_r = _mlstm(qk_r, v_r, og_r, gt_r, cw, cb, vec(m_norm_g[i]), tuple(state),
                      batch=batch, chunk=min(MLSTM_CHUNK, seq))[0]

        xa_m = _attn_meta(aq_m, ak_m, av_m, lam_rows, slopes, vec(a_norm_g[i]), lam_init=lam_init)
        xa_r = _attn(aq_r, ak_r, av_r, ak_m, av_m, lam_rows, slopes, vec(a_norm_g[i]),
                     batch=batch, lam_init=lam_init)

        hr = _merge(hr, xm_r, xa_r, gate_r, wbm, wba, wo, vec(ln2_g[i]), vec(ln2_b[i]))
        hm = _merge(hm, xm_m, xa_m, gate_m, wbm, wba, wo, vec(ln2_g[i]), vec(ln2_b[i]))

        hr = _ffn_ln(hr, *f2, vec(ln3_g[i]), vec(ln3_b[i]))
        hm = _ffn_ln(hm, *f2, vec(ln3_g[i]), vec(ln3_b[i]))

    return hr.reshape(batch, seq, D_MODEL)
```

```python
import functools
import math

import jax
import jax.numpy as jnp
from jax import lax
from jax.experimental import pallas as pl
from jax.experimental.pallas import tpu as pltpu

F32 = jnp.float32
BF16 = jnp.bfloat16

D_MODEL = 1024
N_META = 16
META_ROWS = 128
M_HEADS = 4
M_DK = 128
M_WIDTH = 512
CONV_W = 4
A_HEADS = 4
A_DK = 64
A_DV = 128
A_WIDTH = 512
D_FF = 2816
FF_CHUNK = 256
N_FF_CHUNKS = D_FF // FF_CHUNK
DEPTH = 2
ALPHA = (2 * DEPTH) ** 0.25
LN_EPS = 1e-5
NEG = -1e30

SEG_QK = (0, 1024)
SEG_V = (1024, 1536)
SEG_O = (1536, 2048)
SEG_IF = (2048, 2176)
SEG_AQ = (2176, 2688)
SEG_AK = (2688, 3200)
SEG_AV = (3200, 3712)
SEG_G = (3712, 5760)
PROJ_COLS = 5760

ROW_TILE = 512
FFN_ROW_TILE = 1024
MLSTM_CHUNK = 256
ATTN_BLOCK = 512
LOG2E = math.log2(math.e)
VMEM_LIMIT = 56 * 1024 * 1024


def _nt_dot(a, b):
    return lax.dot_general(a, b, (((1,), (1,)), ((), ())), preferred_element_type=F32)


def _layer_norm(y, g, b):
    mu = jnp.mean(y, -1, keepdims=True)
    d = y - mu
    var = jnp.mean(d * d, -1, keepdims=True)
    return d * lax.rsqrt(var + LN_EPS) * g + b


def _swiglu_ln(x, wg_ref, wu_ref, wd_ref, g_ref, b_ref):
    xb = x.astype(BF16)
    acc = jnp.zeros(x.shape, F32)
    for j in range(N_FF_CHUNKS):
        cols = slice(j * FF_CHUNK, (j + 1) * FF_CHUNK)
        g = jnp.dot(xb, wg_ref[:, cols], preferred_element_type=F32)
        u = jnp.dot(xb, wu_ref[:, cols], preferred_element_type=F32)
        a = (g * jax.nn.sigmoid(g) * u).astype(BF16)
        acc = acc + jnp.dot(a, wd_ref[cols, :], preferred_element_type=F32)
    return _layer_norm(ALPHA * x + 0.5 * acc, g_ref[...], b_ref[...])


def _ffn_ln_kernel(x_ref, wg_ref, wu_ref, wd_ref, g_ref, b_ref, o_ref):
    o_ref[...] = _swiglu_ln(x_ref[...], wg_ref, wu_ref, wd_ref, g_ref, b_ref)


def _resident(shape):
    return pl.BlockSpec(shape, lambda *_: (0,) * len(shape), pipeline_mode=pl.Buffered(1))


def _ffn_ln(x, w_gate, w_up, w_down, g, b):
    m = x.shape[0]
    tm = min(FFN_ROW_TILE, m)
    return pl.pallas_call(
        _ffn_ln_kernel,
        out_shape=jax.ShapeDtypeStruct((m, D_MODEL), F32),
        grid=(m // tm,),
        in_specs=[
            pl.BlockSpec((tm, D_MODEL), lambda i: (i, 0)),
            _resident((D_MODEL, D_FF)), _resident((D_MODEL, D_FF)), _resident((D_FF, D_MODEL)),
            _resident((1, D_MODEL)), _resident((1, D_MODEL)),
        ],
        out_specs=pl.BlockSpec((tm, D_MODEL), lambda i: (i, 0)),
        compiler_params=pltpu.CompilerParams(
            dimension_semantics=("arbitrary",), vmem_limit_bytes=VMEM_LIMIT),
    )(x, w_gate, w_up, w_down, g, b)


def _proj_kernel(h_ref, w_ref, bif_ref, bg_ref,
                 qk_ref, v_ref, og_ref, gt_ref, aq_ref, ak_ref, av_ref, gate_ref):
    hb = h_ref[...].astype(BF16)

    def seg(s):
        return jnp.dot(hb, w_ref[:, s[0]:s[1]], preferred_element_type=F32)

    qk_ref[...] = seg(SEG_QK)
    v_ref[...] = seg(SEG_V).astype(BF16)
    og_ref[...] = jax.nn.sigmoid(seg(SEG_O)).astype(BF16)
    z = seg(SEG_IF) + bif_ref[...]
    col = lax.broadcasted_iota(jnp.int32, z.shape, 1)
    log_f = jnp.minimum(z, 0.0) - jnp.log1p(jnp.exp(-jnp.abs(z)))
    gt_ref[...] = jnp.where(col < M_HEADS, z, jnp.where(col < 2 * M_HEADS, log_f, 0.0))
    aq_ref[...] = (seg(SEG_AQ) * (LOG2E * A_DK ** -0.5)).astype(BF16)
    ak_ref[...] = seg(SEG_AK).astype(BF16)
    av_ref[...] = seg(SEG_AV).astype(BF16)
    gate_ref[...] = jax.nn.sigmoid(seg(SEG_G) + bg_ref[...]).astype(BF16)


def _proj(h, w_all, b_if, b_gate):
    m = h.shape[0]
    tm = min(ROW_TILE, m)
    row = lambda n: pl.BlockSpec((tm, n), lambda i: (i, 0))
    const = lambda r, n: pl.BlockSpec((r, n), lambda i: (0, 0))
    out_shape = (
        jax.ShapeDtypeStruct((m, 1024), F32),
        jax.ShapeDtypeStruct((m, 512), BF16),
        jax.ShapeDtypeStruct((m, 512), BF16),
        jax.ShapeDtypeStruct((m, 128), F32),
        jax.ShapeDtypeStruct((m, 512), BF16),
        jax.ShapeDtypeStruct((m, 512), BF16),
        jax.ShapeDtypeStruct((m, 512), BF16),
        jax.ShapeDtypeStruct((m, 2048), BF16),
    )
    return pl.pallas_call(
        _proj_kernel,
        out_shape=out_shape,
        grid=(m // tm,),
        in_specs=[row(D_MODEL), const(D_MODEL, PROJ_COLS), const(1, 128), const(1, 2048)],
        out_specs=(row(1024), row(512), row(512), row(128), row(512), row(512), row(512), row(2048)),
        compiler_params=pltpu.CompilerParams(
            dimension_semantics=("arbitrary",), vmem_limit_bytes=VMEM_LIMIT),
    )(h, w_all, b_if, b_gate)


def _mlstm_kernel(qk_ref, v_ref, og_ref, gt_ref, cw_ref, cb_ref, ng_ref,
                  c0_ref, m0_ref, t0_ref,
                  h_ref, c_ref, m_ref, tail_ref, *, T):
    @pl.when(pl.program_id(1) == 0)
    def _():
        c_ref[...] = c0_ref[...]
        m_ref[...] = m0_ref[...]
        tail_ref[...] = t0_ref[...]

    u = jnp.concatenate([tail_ref[0], qk_ref[...]], axis=0)
    tail_ref[0] = u[T:T + 8, :]
    y = cb_ref[...] + cw_ref[CONV_W - 1:CONV_W, :] * u
    for j in range(1, CONV_W):
        y = y + cw_ref[CONV_W - 1 - j:CONV_W - j, :] * pltpu.roll(u, j, 0)
    y = y[8:8 + T, :]
    qk = y * jax.nn.sigmoid(y)

    gates = gt_ref[...]
    r = lax.broadcasted_iota(jnp.int32, (T, T), 0)
    c = lax.broadcasted_iota(jnp.int32, (T, T), 1)
    causal = c <= r
    tri = jnp.where(causal, 1.0, 0.0).astype(F32)
    cums = jnp.dot(tri, gates, preferred_element_type=F32, precision=lax.Precision.HIGHEST)
    u_rows = (gates.T)[0:M_HEADS, :] - (cums.T)[M_HEADS:2 * M_HEADS, :]
    eye = c == r
    ones = jnp.ones((T, M_DK), BF16)

    for h in range(M_HEADS):
        sl = slice(h * M_DK, (h + 1) * M_DK)
        qb = qk[:, sl].astype(BF16)
        k_h = qk[:, M_WIDTH + h * M_DK:M_WIDTH + (h + 1) * M_DK] * (M_DK ** -0.5)
        kb = k_h.astype(BF16)
        v_aug = jnp.concatenate([v_ref[:, sl], ones], axis=1)
        b_col = cums[:, M_HEADS + h:M_HEADS + h + 1]
        m_prev = m_ref[0, h:h + 1, 0:1]
        cn_prev = c_ref[0, h]

        um = jnp.where(causal, u_rows[h:h + 1, :], NEG)
        big_m = jnp.maximum(jnp.max(um, -1, keepdims=True), m_prev)
        p = jnp.exp(um - big_m)
        s = (_nt_dot(qb, kb) * p).astype(BF16)
        w_state = jnp.exp(m_prev - big_m)
        tot = (jnp.dot(s, v_aug, preferred_element_type=F32)
               + w_state * jnp.dot(qb, cn_prev.astype(BF16), preferred_element_type=F32))
        num = tot[:, 0:M_DK]
        den = tot[:, M_DK:2 * M_DK]
        hh = num / jnp.maximum(jnp.abs(den), jnp.exp(-(b_col + big_m)))
        mu = jnp.mean(hh, -1, keepdims=True)
        d = hh - mu
        hn = d * lax.rsqrt(jnp.mean(d * d, -1, keepdims=True) + LN_EPS)
        h_ref[:, sl] = (hn * ng_ref[:, sl] * og_ref[:, sl].astype(F32)).astype(BF16)

        w_diag = jnp.where(eye, p[T - 1:T, :], 0.0).astype(BF16)
        vw = jnp.dot(w_diag, v_aug, preferred_element_type=F32)
        c_ref[0, h] = (w_state[T - 1:T, :] * cn_prev
                       + jnp.dot(k_h.T.astype(BF16), vw.astype(BF16), preferred_element_type=F32))
        m_ref[0, h:h + 1, :] = jnp.broadcast_to(b_col[T - 1:T, :] + big_m[T - 1:T, :], (1, 128))


def _mlstm(qk, v, og, gates, conv_w, conv_b, norm_g, state, *, batch, chunk):
    m = qk.shape[0]
    nc = m // (batch * chunk)
    c0, m0, t0 = state
    row = lambda n: pl.BlockSpec((chunk, n), lambda b, c: (b * nc + c, 0))
    const2 = lambda r, n: pl.BlockSpec((r, n), lambda b, c: (0, 0))
    init3 = lambda r, n: pl.BlockSpec((1, r, n), lambda b, c: (0, 0, 0))
    out3 = lambda r, n: pl.BlockSpec((1, r, n), lambda b, c: (b, 0, 0))
    out_shape = (
        jax.ShapeDtypeStruct((m, M_WIDTH), BF16),
        jax.ShapeDtypeStruct((batch, M_HEADS, M_DK, 2 * M_DK), F32),
        jax.ShapeDtypeStruct((batch, 8, 128), F32),
        jax.ShapeDtypeStruct((batch, 8, 1024), F32),
    )
    return pl.pallas_call(
        functools.partial(_mlstm_kernel, T=chunk),
        out_shape=out_shape,
        grid=(batch, nc),
        in_specs=[
            row(1024), row(512), row(512), row(128),
            const2(8, 1024), const2(1, 1024), const2(1, 512),
            pl.BlockSpec((1, M_HEADS, M_DK, 2 * M_DK), lambda b, c: (0, 0, 0, 0)),
            init3(8, 128), init3(8, 1024),
        ],
        out_specs=(
            row(512),
            pl.BlockSpec((1, M_HEADS, M_DK, 2 * M_DK), lambda b, c: (b, 0, 0, 0)),
            out3(8, 128), out3(8, 1024),
        ),
        compiler_params=pltpu.CompilerParams(
            dimension_semantics=("arbitrary", "arbitrary"), vmem_limit_bytes=VMEM_LIMIT),
    )(qk, v, og, gates, conv_w, conv_b, norm_g, c0, m0, t0)


def _stack_maps(q):
    lane = lax.broadcasted_iota(jnp.int32, q.shape, 1)
    zero = jnp.zeros_like(q)
    return jnp.concatenate([jnp.where(lane < A_DK, q, zero), jnp.where(lane >= A_DK, q, zero)], axis=0)


def _diff_finish(acc, l, lam_ref, ng, lam_init, t):
    lam = (jnp.exp(jnp.sum(lam_ref[0:1, :] * lam_ref[1:2, :], -1, keepdims=True))
           - jnp.exp(jnp.sum(lam_ref[2:3, :] * lam_ref[3:4, :], -1, keepdims=True)) + lam_init)
    o = acc[:t] / l[:t] - lam * (acc[t:] / l[t:])
    o = o * lax.rsqrt(jnp.mean(o * o, -1, keepdims=True) + LN_EPS) * (1.0 - lam_init)
    return (o * ng).astype(BF16)


def _online_softmax_update(s, vb, m_s, l_s, acc_s):
    tiles = [s[:, t * 128:(t + 1) * 128] for t in range(s.shape[1] // 128)]
    part = functools.reduce(jnp.maximum, tiles)
    m_old = m_s[...]
    m_new = jnp.maximum(m_old, jnp.max(part, -1, keepdims=True))
    alpha = jnp.exp2(m_old - m_new)
    ps = [jnp.exp2(t - m_new) for t in tiles]
    l_s[...] = alpha * l_s[...] + functools.reduce(jnp.add, ps)
    p = jnp.concatenate([x.astype(BF16) for x in ps], axis=1)
    acc_s[...] = alpha * acc_s[...] + jnp.dot(p, vb, preferred_element_type=F32)
    m_s[...] = m_new


def _attn_kernel(q_ref, k_ref, v_ref, km_ref, vm_ref, lam_ref, sl_ref, ng_ref, o_ref,
                 m_s, l_s, acc_s, *, T, lam_init):
    i = pl.program_id(2)
    slope = sl_ref[:, 0:1]
    qs = _stack_maps(q_ref[...])
    q0 = i * T

    m_s[...] = jnp.full(m_s.shape, NEG, F32)
    l_s[...] = jnp.zeros(l_s.shape, F32)
    acc_s[...] = jnp.zeros(acc_s.shape, F32)

    cm = lax.broadcasted_iota(jnp.int32, (1, META_ROWS), 1)
    bias_m = jnp.where(cm >= META_ROWS - N_META, slope * (cm - (META_ROWS + q0)).astype(F32), NEG)
    _online_softmax_update(_nt_dot(qs, km_ref[...]) + bias_m, vm_ref[...], m_s, l_s, acc_s)

    ck = lax.broadcasted_iota(jnp.int32, (1, T), 1)

    def logits(j):
        kb = k_ref[pl.ds(pl.multiple_of(j * T, T), T), :]
        return _nt_dot(qs, kb) + slope * (ck + (j * T - q0)).astype(F32)

    def values(j):
        return v_ref[pl.ds(pl.multiple_of(j * T, T), T), :]

    def pair(jj, carry):
        s_a = logits(2 * jj)
        s_b = logits(2 * jj + 1)
        _online_softmax_update(s_a, values(2 * jj), m_s, l_s, acc_s)
        _online_softmax_update(s_b, values(2 * jj + 1), m_s, l_s, acc_s)
        return carry

    lax.fori_loop(0, i // 2, pair, 0)

    @pl.when(i % 2 == 1)
    def _():
        _online_softmax_update(logits(i - 1), values(i - 1), m_s, l_s, acc_s)

    r = lax.broadcasted_iota(jnp.int32, (T, T), 0)
    c = lax.broadcasted_iota(jnp.int32, (T, T), 1)
    keep = c <= r
    s = jnp.where(jnp.concatenate([keep, keep], axis=0), logits(i), NEG)
    _online_softmax_update(s, values(i), m_s, l_s, acc_s)
    l = jnp.sum(l_s[...], -1, keepdims=True)
    o_ref[...] = _diff_finish(acc_s[...], l, lam_ref, ng_ref[...], lam_init, T)


def _attn(aq, ak, av, ak_meta, av_meta, lam_rows, slopes, norm_g, *, batch, lam_init):
    m = aq.shape[0]
    seq = m // batch
    t = min(ATTN_BLOCK, seq)
    nq = seq // t
    qspec = pl.BlockSpec((t, 128), lambda b, h, i: (b * nq + i, h))
    kvspec = pl.BlockSpec((seq, 128), lambda b, h, i: (b, h))
    mspec = pl.BlockSpec((META_ROWS, 128), lambda b, h, i: (0, h))
    hrow = pl.BlockSpec((1, 128), lambda b, h, i: (0, h))
    return pl.pallas_call(
        functools.partial(_attn_kernel, T=t, lam_init=lam_init),
        out_shape=jax.ShapeDtypeStruct((m, A_WIDTH), BF16),
        grid=(batch, A_HEADS, nq),
        in_specs=[qspec, kvspec, kvspec, mspec, mspec,
                  pl.BlockSpec((8, 128), lambda b, h, i: (0, 0)), hrow, hrow],
        out_specs=qspec,
        scratch_shapes=[pltpu.VMEM((2 * t, 128), F32), pltpu.VMEM((2 * t, 128), F32),
                        pltpu.VMEM((2 * t, A_DV), F32)],
        compiler_params=pltpu.CompilerParams(
            dimension_semantics=("arbitrary", "arbitrary", "arbitrary"),
            vmem_limit_bytes=VMEM_LIMIT),
    )(aq, ak, av, ak_meta, av_meta, lam_rows, slopes, norm_g)


def _attn_meta_kernel(q_ref, k_ref, v_ref, lam_ref, sl_ref, ng_ref, o_ref, *, lam_init):
    t = META_ROWS
    slope = sl_ref[:, 0:1]
    qs = _stack_maps(q_ref[...])
    r = lax.broadcasted_iota(jnp.int32, (t, t), 0)
    c = lax.broadcasted_iota(jnp.int32, (t, t), 1)
    bias = jnp.where((c >= t - N_META) & (c <= r), slope * (c - r).astype(F32), NEG)
    s = _nt_dot(qs, k_ref[...]) + jnp.concatenate([bias, bias], axis=0)
    p = jnp.exp2(s - jnp.max(s, -1, keepdims=True))
    l = jnp.sum(p, -1, keepdims=True)
    acc = jnp.dot(p.astype(BF16), v_ref[...], preferred_element_type=F32)
    o_ref[...] = _diff_finish(acc, l, lam_ref, ng_ref[...], lam_init, t)


def _attn_meta(aq, ak, av, lam_rows, slopes, norm_g, *, lam_init):
    blk = pl.BlockSpec((META_ROWS, 128), lambda h: (0, h))
    hrow = pl.BlockSpec((1, 128), lambda h: (0, h))
    return pl.pallas_call(
        functools.partial(_attn_meta_kernel, lam_init=lam_init),
        out_shape=jax.ShapeDtypeStruct((META_ROWS, A_WIDTH), BF16),
        grid=(A_HEADS,),
        in_specs=[blk, blk, blk, pl.BlockSpec((8, 128), lambda h: (0, 0)), hrow, hrow],
        out_specs=blk,
        compiler_params=pltpu.CompilerParams(dimension_semantics=("arbitrary",)),
    )(aq, ak, av, lam_rows, slopes, norm_g)


def _merge_ffn_kernel(h_ref, hm_ref, ha_ref, gate_ref, wbm_ref, wba_ref, wo_ref, g2_ref, b2_ref,
                      wg_ref, wu_ref, wd_ref, g3_ref, b3_ref, o_ref):
    ym = jnp.dot(hm_ref[...], wbm_ref[...], preferred_element_type=F32)
    ya = jnp.dot(ha_ref[...], wba_ref[...], preferred_element_type=F32)
    gm = gate_ref[:, 0:D_MODEL].astype(F32)
    ga = gate_ref[:, D_MODEL:2 * D_MODEL].astype(F32)
    mixed = (gm * ym + ga * ya).astype(BF16)
    mix = jnp.dot(mixed, wo_ref[...], preferred_element_type=F32)
    h2 = _layer_norm(ALPHA * h_ref[...] + mix, g2_ref[...], b2_ref[...])
    o_ref[...] = _swiglu_ln(h2, wg_ref, wu_ref, wd_ref, g3_ref, b3_ref)


def _merge_ffn(h, hm, ha, gate, w_bm, w_ba, w_out, g2, b2, w_gate, w_up, w_down, g3, b3):
    m = h.shape[0]
    tm = min(ROW_TILE, m)
    row = lambda n: pl.BlockSpec((tm, n), lambda i: (i, 0))
    return pl.pallas_call(
        _merge_ffn_kernel,
        out_shape=jax.ShapeDtypeStruct((m, D_MODEL), F32),
        grid=(m // tm,),
        in_specs=[row(D_MODEL), row(512), row(512), row(2048),
                  _resident((512, D_MODEL)), _resident((512, D_MODEL)), _resident((D_MODEL, D_MODEL)),
                  _resident((1, D_MODEL)), _resident((1, D_MODEL)),
                  _resident((D_MODEL, D_FF)), _resident((D_MODEL, D_FF)), _resident((D_FF, D_MODEL)),
                  _resident((1, D_MODEL)), _resident((1, D_MODEL))],
        out_specs=row(D_MODEL),
        compiler_params=pltpu.CompilerParams(
            dimension_semantics=("arbitrary",), vmem_limit_bytes=VMEM_LIMIT),
    )(h, hm, ha, gate, w_bm, w_ba, w_out, g2, b2, w_gate, w_up, w_down, g3, b3)


def _ffn_weights(w_gate, w_up, w_down):
    return w_gate.astype(BF16), w_up.astype(BF16), w_down.astype(BF16)


def _proj_weights(w_in, b_if, b_gate):
    w = w_in.astype(BF16)
    w_all = jnp.concatenate(
        [w[:, :2048], jnp.pad(w[:, 2048:2056], ((0, 0), (0, 120))), w[:, 2056:]], axis=1)
    return w_all, jnp.pad(b_if, (0, 120)).reshape(1, 128), b_gate.reshape(1, 2 * D_MODEL)


def kernel(x, meta, ffn1_w_gate, ffn1_w_up, ffn1_w_down, ln1_g, ln1_b, w_in, conv_w, conv_b, b_if, m_norm_g, lam_q1, lam_k1, lam_q2, lam_k2, a_norm_g, w_bm, w_ba, b_gate, w_out, ln2_g, ln2_b, ffn2_w_gate, ffn2_w_up, ffn2_w_down, ln3_g, ln3_b):
    batch, seq, _ = x.shape
    rows = batch * seq
    hr = x.reshape(rows, D_MODEL)
    hm = jnp.pad(meta.astype(x.dtype), ((META_ROWS - N_META, 0), (0, 0)))

    slopes = jnp.repeat(
        jnp.array([LOG2E * 2.0 ** (-8.0 * (h + 1) / A_HEADS) for h in range(A_HEADS)], F32), 128).reshape(1, 512)
    is_meta = (jnp.arange(META_ROWS) >= META_ROWS - N_META)[:, None]
    null_gate = jnp.where(jnp.arange(128) < M_HEADS, NEG, 0.0).astype(F32)[None, :]
    state0 = (jnp.zeros((1, M_HEADS, M_DK, 2 * M_DK), F32), jnp.full((1, 8, 128), NEG, F32),
              jnp.zeros((1, 8, 1024), F32))
    vec = lambda a: a.reshape(1, -1)

    for i in range(DEPTH):
        lam_init = 0.8 - 0.6 * math.exp(-0.3 * i)
        f1 = _ffn_weights(ffn1_w_gate[i], ffn1_w_up[i], ffn1_w_down[i])
        f2 = _ffn_weights(ffn2_w_gate[i], ffn2_w_up[i], ffn2_w_down[i])
        w_all, bif, bg = _proj_weights(w_in[i], b_if[i], b_gate[i])
        cw = jnp.pad(conv_w[i], ((0, 8 - CONV_W), (0, 0)))
        cb = vec(conv_b[i])
        lam_rows = jnp.pad(jnp.stack([lam_q1[i], lam_k1[i], lam_q2[i], lam_k2[i]]).astype(F32),
                           ((0, 4), (0, 128 - A_DK)))
        wbm, wba, wo = w_bm[i].astype(BF16), w_ba[i].astype(BF16), w_out[i].astype(BF16)

        hr = _ffn_ln(hr, *f1, vec(ln1_g[i]), vec(ln1_b[i]))
        hm = _ffn_ln(hm, *f1, vec(ln1_g[i]), vec(ln1_b[i]))

        qk_r, v_r, og_r, gt_r, aq_r, ak_r, av_r, gate_r = _proj(hr, w_all, bif, bg)
        qk_m, v_m, og_m, gt_m, aq_m, ak_m, av_m, gate_m = _proj(hm, w_all, bif, bg)
        qk_m = jnp.where(is_meta, qk_m, 0.0)
        gt_m = jnp.where(is_meta, gt_m, null_gate)

        xm_m, *state = _mlstm(qk_m, v_m, og_m, gt_m, cw, cb, vec(m_norm_g[i]), state0,
                              batch=1, chunk=META_ROWS)
        xm_r = _mlstm(qk_r, v_r, og_r, gt_r, cw, cb, vec(m_norm_g[i]), tuple(state),
                      batch=batch, chunk=min(MLSTM_CHUNK, seq))[0]

        xa_m = _attn_meta(aq_m, ak_m, av_m, lam_rows, slopes, vec(a_norm_g[i]), lam_init=lam_init)
        xa_r = _attn(aq_r, ak_r, av_r, ak_m, av_m, lam_rows, slopes, vec(a_norm_g[i]),
                     batch=batch, lam_init=lam_init)

        tail_params = (wbm, wba, wo, vec(ln2_g[i]), vec(ln2_b[i]), *f2, vec(ln3_g[i]), vec(ln3_b[i]))
        hr = _merge_ffn(hr, xm_r, xa_r, gate_r, *tail_params)
        hm = _merge_ffn(hm, xm_m, xa_m, gate_m, *tail_params)

    return hr.reshape(batch, seq, D_MODEL)
```

```python
import functools
import math

import jax
import jax.numpy as jnp
from jax import lax
from jax.experimental import pallas as pl
from jax.experimental.pallas import tpu as pltpu

F32 = jnp.float32
BF16 = jnp.bfloat16

D_MODEL = 1024
N_META = 16
META_ROWS = 128
M_HEADS = 4
M_DK = 128
M_WIDTH = 512
CONV_W = 4
A_HEADS = 4
A_DK = 64
A_DV = 128
A_WIDTH = 512
D_FF = 2816
FF_CHUNK = 256
N_FF_CHUNKS = D_FF // FF_CHUNK
DEPTH = 2
ALPHA = (2 * DEPTH) ** 0.25
LN_EPS = 1e-5
NEG = -1e30

SEG_QK = (0, 1024)
SEG_V = (1024, 1536)
SEG_O = (1536, 2048)
SEG_IF = (2048, 2176)
SEG_AQ = (2176, 2688)
SEG_AK = (2688, 3200)
SEG_AV = (3200, 3712)
SEG_G = (3712, 5760)
PROJ_COLS = 5760

ROW_TILE = 512
FFN_ROW_TILE = 1024
MLSTM_CHUNK = 256
MLSTM_GROUP = 4
ATTN_BLOCK = 512
LOG2E = math.log2(math.e)
VMEM_LIMIT = 56 * 1024 * 1024


def _nt_dot(a, b):
    return lax.dot_general(a, b, (((1,), (1,)), ((), ())), preferred_element_type=F32)


def _layer_norm(y, g, b):
    mu = jnp.mean(y, -1, keepdims=True)
    d = y - mu
    var = jnp.mean(d * d, -1, keepdims=True)
    return d * lax.rsqrt(var + LN_EPS) * g + b


def _swiglu_ln(x, wg_ref, wu_ref, wd_ref, g_ref, b_ref):
    xb = x.astype(BF16)
    acc = jnp.zeros(x.shape, F32)
    for j in range(N_FF_CHUNKS):
        cols = slice(j * FF_CHUNK, (j + 1) * FF_CHUNK)
        g = jnp.dot(xb, wg_ref[:, cols], preferred_element_type=F32)
        u = jnp.dot(xb, wu_ref[:, cols], preferred_element_type=F32)
        a = (g * jax.nn.sigmoid(g) * u).astype(BF16)
        acc = acc + jnp.dot(a, wd_ref[cols, :], preferred_element_type=F32)
    return _layer_norm(ALPHA * x + 0.5 * acc, g_ref[...], b_ref[...])


def _ffn_ln_kernel(x_ref, wg_ref, wu_ref, wd_ref, g_ref, b_ref, o_ref):
    o_ref[...] = _swiglu_ln(x_ref[...], wg_ref, wu_ref, wd_ref, g_ref, b_ref)


def _resident(shape):
    return pl.BlockSpec(shape, lambda *_: (0,) * len(shape), pipeline_mode=pl.Buffered(1))


def _ffn_ln(x, w_gate, w_up, w_down, g, b):
    m = x.shape[0]
    tm = min(FFN_ROW_TILE, m)
    return pl.pallas_call(
        _ffn_ln_kernel,
        out_shape=jax.ShapeDtypeStruct((m, D_MODEL), F32),
        grid=(m // tm,),
        in_specs=[
            pl.BlockSpec((tm, D_MODEL), lambda i: (i, 0)),
            _resident((D_MODEL, D_FF)), _resident((D_MODEL, D_FF)), _resident((D_FF, D_MODEL)),
            _resident((1, D_MODEL)), _resident((1, D_MODEL)),
        ],
        out_specs=pl.BlockSpec((tm, D_MODEL), lambda i: (i, 0)),
        compiler_params=pltpu.CompilerParams(
            dimension_semantics=("arbitrary",), vmem_limit_bytes=VMEM_LIMIT),
    )(x, w_gate, w_up, w_down, g, b)


def _proj_kernel(h_ref, w_ref, bif_ref, bg_ref,
                 qk_ref, v_ref, og_ref, gt_ref, aq_ref, ak_ref, av_ref, gate_ref):
    hb = h_ref[...].astype(BF16)

    def seg(s):
        return jnp.dot(hb, w_ref[:, s[0]:s[1]], preferred_element_type=F32)

    qk_ref[...] = seg(SEG_QK)
    v_ref[...] = seg(SEG_V).astype(BF16)
    og_ref[...] = jax.nn.sigmoid(seg(SEG_O)).astype(BF16)
    z = seg(SEG_IF) + bif_ref[...]
    col = lax.broadcasted_iota(jnp.int32, z.shape, 1)
    log_f = jnp.minimum(z, 0.0) - jnp.log1p(jnp.exp(-jnp.abs(z)))
    gt_ref[...] = jnp.where(col < M_HEADS, z, jnp.where(col < 2 * M_HEADS, log_f, 0.0))
    aq_ref[...] = (seg(SEG_AQ) * (LOG2E * A_DK ** -0.5)).astype(BF16)
    ak_ref[...] = seg(SEG_AK).astype(BF16)
    av_ref[...] = seg(SEG_AV).astype(BF16)
    gate_ref[...] = jax.nn.sigmoid(seg(SEG_G) + bg_ref[...]).astype(BF16)


def _proj(h, w_all, b_if, b_gate):
    m = h.shape[0]
    tm = min(ROW_TILE, m)
    row = lambda n: pl.BlockSpec((tm, n), lambda i: (i, 0))
    const = lambda r, n: pl.BlockSpec((r, n), lambda i: (0, 0))
    out_shape = (
        jax.ShapeDtypeStruct((m, 1024), F32),
        jax.ShapeDtypeStruct((m, 512), BF16),
        jax.ShapeDtypeStruct((m, 512), BF16),
        jax.ShapeDtypeStruct((m, 128), F32),
        jax.ShapeDtypeStruct((m, 512), BF16),
        jax.ShapeDtypeStruct((m, 512), BF16),
        jax.ShapeDtypeStruct((m, 512), BF16),
        jax.ShapeDtypeStruct((m, 2048), BF16),
    )
    return pl.pallas_call(
        _proj_kernel,
        out_shape=out_shape,
        grid=(m // tm,),
        in_specs=[row(D_MODEL), const(D_MODEL, PROJ_COLS), const(1, 128), const(1, 2048)],
        out_specs=(row(1024), row(512), row(512), row(128), row(512), row(512), row(512), row(2048)),
        compiler_params=pltpu.CompilerParams(
            dimension_semantics=("arbitrary",), vmem_limit_bytes=VMEM_LIMIT),
    )(h, w_all, b_if, b_gate)


def _mlstm_kernel(qk_ref, v_ref, og_ref, gt_ref, cw_ref, cb_ref, ng_ref,
                  c0_ref, m0_ref, t0_ref,
                  h_ref, c_ref, m_ref, tail_ref, *, T, G):
    @pl.when(pl.program_id(1) == 0)
    def _():
        for g in range(G):
            c_ref[g] = c0_ref[0]
            m_ref[g] = m0_ref[0]
            tail_ref[g] = t0_ref[0]

    r = lax.broadcasted_iota(jnp.int32, (T, T), 0)
    c = lax.broadcasted_iota(jnp.int32, (T, T), 1)
    causal = c <= r
    eye = c == r
    tri = jnp.where(causal, 1.0, 0.0).astype(F32)
    ones = jnp.ones((T, M_DK), BF16)

    for g in range(G):
        u = jnp.concatenate([tail_ref[g], qk_ref[g]], axis=0)
        tail_ref[g] = u[T:T + 8, :]
        y = cb_ref[...] + cw_ref[CONV_W - 1:CONV_W, :] * u
        for j in range(1, CONV_W):
            y = y + cw_ref[CONV_W - 1 - j:CONV_W - j, :] * pltpu.roll(u, j, 0)
        y = y[8:8 + T, :]
        qk = y * jax.nn.sigmoid(y)

        gates = gt_ref[g]
        cums = jnp.dot(tri, gates, preferred_element_type=F32, precision=lax.Precision.HIGHEST)
        u_rows = (gates.T)[0:M_HEADS, :] - (cums.T)[M_HEADS:2 * M_HEADS, :]

        for h in range(M_HEADS):
            sl = slice(h * M_DK, (h + 1) * M_DK)
            qb = qk[:, sl].astype(BF16)
            k_h = qk[:, M_WIDTH + h * M_DK:M_WIDTH + (h + 1) * M_DK] * (M_DK ** -0.5)
            kb = k_h.astype(BF16)
            v_aug = jnp.concatenate([v_ref[g, :, sl], ones], axis=1)
            b_col = cums[:, M_HEADS + h:M_HEADS + h + 1]
            m_prev = m_ref[g, h:h + 1, 0:1]
            cn_prev = c_ref[g, h]

            um = jnp.where(causal, u_rows[h:h + 1, :], NEG)
            big_m = jnp.maximum(jnp.max(um, -1, keepdims=True), m_prev)
            p = jnp.exp(um - big_m)
            s = (_nt_dot(qb, kb) * p).astype(BF16)
            w_state = jnp.exp(m_prev - big_m)
            tot = (jnp.dot(s, v_aug, preferred_element_type=F32)
                   + w_state * jnp.dot(qb, cn_prev.astype(BF16), preferred_element_type=F32))
            num = tot[:, 0:M_DK]
            den = tot[:, M_DK:2 * M_DK]
            hh = num / jnp.maximum(jnp.abs(den), jnp.exp(-(b_col + big_m)))
            mu = jnp.mean(hh, -1, keepdims=True)
            d = hh - mu
            hn = d * lax.rsqrt(jnp.mean(d * d, -1, keepdims=True) + LN_EPS)
            h_ref[g, :, sl] = (hn * ng_ref[:, sl] * og_ref[g, :, sl].astype(F32)).astype(BF16)

            w_diag = jnp.where(eye, p[T - 1:T, :], 0.0).astype(BF16)
            vw = jnp.dot(w_diag, v_aug, preferred_element_type=F32)
            c_ref[g, h] = (w_state[T - 1:T, :] * cn_prev
                           + jnp.dot(k_h.T.astype(BF16), vw.astype(BF16), preferred_element_type=F32))
            m_ref[g, h:h + 1, :] = jnp.broadcast_to(b_col[T - 1:T, :] + big_m[T - 1:T, :], (1, 128))


def _mlstm(qk, v, og, gates, conv_w, conv_b, norm_g, state, *, batch, chunk):
    seq = qk.shape[0] // batch
    nc = seq // chunk
    group = min(MLSTM_GROUP, batch)
    c0, m0, t0 = state
    seq3 = lambda a: a.reshape(batch, seq, a.shape[-1])
    row = lambda n: pl.BlockSpec((group, chunk, n), lambda b, c: (b, c, 0))
    const2 = lambda r, n: pl.BlockSpec((r, n), lambda b, c: (0, 0))
    init3 = lambda r, n: pl.BlockSpec((1, r, n), lambda b, c: (0, 0, 0))
    out3 = lambda r, n: pl.BlockSpec((group, r, n), lambda b, c: (b, 0, 0))
    out_shape = (
        jax.ShapeDtypeStruct((batch, seq, M_WIDTH), BF16),
        jax.ShapeDtypeStruct((batch, M_HEADS, M_DK, 2 * M_DK), F32),
        jax.ShapeDtypeStruct((batch, 8, 128), F32),
        jax.ShapeDtypeStruct((batch, 8, 1024), F32),
    )
    out, c1, m1, t1 = pl.pallas_call(
        functools.partial(_mlstm_kernel, T=chunk, G=group),
        out_shape=out_shape,
        grid=(batch // group, nc),
        in_specs=[
            row(1024), row(512), row(512), row(128),
            const2(8, 1024), const2(1, 1024), const2(1, 512),
            pl.BlockSpec((1, M_HEADS, M_DK, 2 * M_DK), lambda b, c: (0, 0, 0, 0)),
            init3(8, 128), init3(8, 1024),
        ],
        out_specs=(
            row(512),
            pl.BlockSpec((group, M_HEADS, M_DK, 2 * M_DK), lambda b, c: (b, 0, 0, 0)),
            out3(8, 128), out3(8, 1024),
        ),
        compiler_params=pltpu.CompilerParams(
            dimension_semantics=("arbitrary", "arbitrary"), vmem_limit_bytes=VMEM_LIMIT),
    )(seq3(qk), seq3(v), seq3(og), seq3(gates), conv_w, conv_b, norm_g, c0, m0, t0)
    return out.reshape(batch * seq, M_WIDTH), c1, m1, t1


def _stack_maps(q):
    lane = lax.broadcasted_iota(jnp.int32, q.shape, 1)
    zero = jnp.zeros_like(q)
    return jnp.concatenate([jnp.where(lane < A_DK, q, zero), jnp.where(lane >= A_DK, q, zero)], axis=0)


def _diff_finish(acc, l, lam_ref, ng, lam_init, t):
    lam = (jnp.exp(jnp.sum(lam_ref[0:1, :] * lam_ref[1:2, :], -1, keepdims=True))
           - jnp.exp(jnp.sum(lam_ref[2:3, :] * lam_ref[3:4, :], -1, keepdims=True)) + lam_init)
    o = acc[:t] / l[:t] - lam * (acc[t:] / l[t:])
    o = o * lax.rsqrt(jnp.mean(o * o, -1, keepdims=True) + LN_EPS) * (1.0 - lam_init)
    return (o * ng).astype(BF16)


def _online_softmax_update(s, vb, m_s, l_s, acc_s):
    tiles = [s[:, t * 128:(t + 1) * 128] for t in range(s.shape[1] // 128)]
    part = functools.reduce(jnp.maximum, tiles)
    m_old = m_s[...]
    m_new = jnp.maximum(m_old, jnp.max(part, -1, keepdims=True))
    alpha = jnp.exp2(m_old - m_new)
    ps = [jnp.exp2(t - m_new) for t in tiles]
    l_s[...] = alpha * l_s[...] + functools.reduce(jnp.add, ps)
    p = jnp.concatenate([x.astype(BF16) for x in ps], axis=1)
    acc_s[...] = alpha * acc_s[...] + jnp.dot(p, vb, preferred_element_type=F32)
    m_s[...] = m_new


def _attn_kernel(q_ref, k_ref, v_ref, km_ref, vm_ref, lam_ref, sl_ref, ng_ref, o_ref,
                 m_s, l_s, acc_s, *, T, lam_init):
    i = pl.program_id(2)
    slope = sl_ref[:, 0:1]
    qs = _stack_maps(q_ref[...])
    q0 = i * T

    m_s[...] = jnp.full(m_s.shape, NEG, F32)
    l_s[...] = jnp.zeros(l_s.shape, F32)
    acc_s[...] = jnp.zeros(acc_s.shape, F32)

    cm = lax.broadcasted_iota(jnp.int32, (1, META_ROWS), 1)
    bias_m = jnp.where(cm >= META_ROWS - N_META, slope * (cm - (META_ROWS + q0)).astype(F32), NEG)
    _online_softmax_update(_nt_dot(qs, km_ref[...]) + bias_m, vm_ref[...], m_s, l_s, acc_s)

    ck = lax.broadcasted_iota(jnp.int32, (1, T), 1)

    def logits(j):
        kb = k_ref[pl.ds(pl.multiple_of(j * T, T), T), :]
        return _nt_dot(qs, kb) + slope * (ck + (j * T - q0)).astype(F32)

    def values(j):
        return v_ref[pl.ds(pl.multiple_of(j * T, T), T), :]

    def fold_blocks(first, n):
        scores = [logits(first + k) for k in range(n)]
        for k in range(n):
            _online_softmax_update(scores[k], values(first + k), m_s, l_s, acc_s)

    def quad(t, carry):
        fold_blocks(4 * t, 4)
        return carry

    lax.fori_loop(0, i // 4, quad, 0)
    done = i // 4 * 4

    @pl.when(i % 4 >= 2)
    def _():
        fold_blocks(done, 2)

    @pl.when(i % 2 == 1)
    def _():
        fold_blocks(i - 1, 1)

    r = lax.broadcasted_iota(jnp.int32, (T, T), 0)
    c = lax.broadcasted_iota(jnp.int32, (T, T), 1)
    keep = c <= r
    s = jnp.where(jnp.concatenate([keep, keep], axis=0), logits(i), NEG)
    _online_softmax_update(s, values(i), m_s, l_s, acc_s)
    l = jnp.sum(l_s[...], -1, keepdims=True)
    o_ref[...] = _diff_finish(acc_s[...], l, lam_ref, ng_ref[...], lam_init, T)


def _attn(aq, ak, av, ak_meta, av_meta, lam_rows, slopes, norm_g, *, batch, lam_init):
    m = aq.shape[0]
    seq = m // batch
    t = min(ATTN_BLOCK, seq)
    nq = seq // t
    qspec = pl.BlockSpec((t, 128), lambda b, h, i: (b * nq + i, h))
    kvspec = pl.BlockSpec((seq, 128), lambda b, h, i: (b, h))
    mspec = pl.BlockSpec((META_ROWS, 128), lambda b, h, i: (0, h))
    hrow = pl.BlockSpec((1, 128), lambda b, h, i: (0, h))
    return pl.pallas_call(
        functools.partial(_attn_kernel, T=t, lam_init=lam_init),
        out_shape=jax.ShapeDtypeStruct((m, A_WIDTH), BF16),
        grid=(batch, A_HEADS, nq),
        in_specs=[qspec, kvspec, kvspec, mspec, mspec,
                  pl.BlockSpec((8, 128), lambda b, h, i: (0, 0)), hrow, hrow],
        out_specs=qspec,
        scratch_shapes=[pltpu.VMEM((2 * t, 128), F32), pltpu.VMEM((2 * t, 128), F32),
                        pltpu.VMEM((2 * t, A_DV), F32)],
        compiler_params=pltpu.CompilerParams(
            dimension_semantics=("arbitrary", "arbitrary", "arbitrary"),
            vmem_limit_bytes=VMEM_LIMIT),
    )(aq, ak, av, ak_meta, av_meta, lam_rows, slopes, norm_g)


def _attn_meta_kernel(q_ref, k_ref, v_ref, lam_ref, sl_ref, ng_ref, o_ref, *, lam_init):
    t = META_ROWS
    slope = sl_ref[:, 0:1]
    qs = _stack_maps(q_ref[...])
    r = lax.broadcasted_iota(jnp.int32, (t, t), 0)
    c = lax.broadcasted_iota(jnp.int32, (t, t), 1)
    bias = jnp.where((c >= t - N_META) & (c <= r), slope * (c - r).astype(F32), NEG)
    s = _nt_dot(qs, k_ref[...]) + jnp.concatenate([bias, bias], axis=0)
    p = jnp.exp2(s - jnp.max(s, -1, keepdims=True))
    l = jnp.sum(p, -1, keepdims=True)
    acc = jnp.dot(p.astype(BF16), v_ref[...], preferred_element_type=F32)
    o_ref[...] = _diff_finish(acc, l, lam_ref, ng_ref[...], lam_init, t)


def _attn_meta(aq, ak, av, lam_rows, slopes, norm_g, *, lam_init):
    blk = pl.BlockSpec((META_ROWS, 128), lambda h: (0, h))
    hrow = pl.BlockSpec((1, 128), lambda h: (0, h))
    return pl.pallas_call(
        functools.partial(_attn_meta_kernel, lam_init=lam_init),
        out_shape=jax.ShapeDtypeStruct((META_ROWS, A_WIDTH), BF16),
        grid=(A_HEADS,),
        in_specs=[blk, blk, blk, pl.BlockSpec((8, 128), lambda h: (0, 0)), hrow, hrow],
        out_specs=blk,
        compiler_params=pltpu.CompilerParams(dimension_semantics=("arbitrary",)),
    )(aq, ak, av, lam_rows, slopes, norm_g)


def _merge_ffn_kernel(h_ref, hm_ref, ha_ref, gate_ref, wbm_ref, wba_ref, wo_ref, g2_ref, b2_ref,
                      wg_ref, wu_ref, wd_ref, g3_ref, b3_ref, o_ref):
    ym = jnp.dot(hm_ref[...], wbm_ref[...], preferred_element_type=F32)
    ya = jnp.dot(ha_ref[...], wba_ref[...], preferred_element_type=F32)
    gm = gate_ref[:, 0:D_MODEL].astype(F32)
    ga = gate_ref[:, D_MODEL:2 * D_MODEL].astype(F32)
    mixed = (gm * ym + ga * ya).astype(BF16)
    mix = jnp.dot(mixed, wo_ref[...], preferred_element_type=F32)
    h2 = _layer_norm(ALPHA * h_ref[...] + mix, g2_ref[...], b2_ref[...])
    o_ref[...] = _swiglu_ln(h2, wg_ref, wu_ref, wd_ref, g3_ref, b3_ref)


def _merge_ffn(h, hm, ha, gate, w_bm, w_ba, w_out, g2, b2, w_gate, w_up, w_down, g3, b3):
    m = h.shape[0]
    tm = min(ROW_TILE, m)
    row = lambda n: pl.BlockSpec((tm, n), lambda i: (i, 0))
    return pl.pallas_call(
        _merge_ffn_kernel,
        out_shape=jax.ShapeDtypeStruct((m, D_MODEL), F32),
        grid=(m // tm,),
        in_specs=[row(D_MODEL), row(512), row(512), row(2048),
                  _resident((512, D_MODEL)), _resident((512, D_MODEL)), _resident((D_MODEL, D_MODEL)),
                  _resident((1, D_MODEL)), _resident((1, D_MODEL)),
                  _resident((D_MODEL, D_FF)), _resident((D_MODEL, D_FF)), _resident((D_FF, D_MODEL)),
                  _resident((1, D_MODEL)), _resident((1, D_MODEL))],
        out_specs=row(D_MODEL),
        compiler_params=pltpu.CompilerParams(
            dimension_semantics=("arbitrary",), vmem_limit_bytes=VMEM_LIMIT),
    )(h, hm, ha, gate, w_bm, w_ba, w_out, g2, b2, w_gate, w_up, w_down, g3, b3)


def _ffn_weights(w_gate, w_up, w_down):
    return w_gate.astype(BF16), w_up.astype(BF16), w_down.astype(BF16)


def _proj_weights(w_in, b_if, b_gate):
    w = w_in.astype(BF16)
    w_all = jnp.concatenate(
        [w[:, :2048], jnp.pad(w[:, 2048:2056], ((0, 0), (0, 120))), w[:, 2056:]], axis=1)
    return w_all, jnp.pad(b_if, (0, 120)).reshape(1, 128), b_gate.reshape(1, 2 * D_MODEL)


def kernel(x, meta, ffn1_w_gate, ffn1_w_up, ffn1_w_down, ln1_g, ln1_b, w_in, conv_w, conv_b, b_if, m_norm_g, lam_q1, lam_k1, lam_q2, lam_k2, a_norm_g, w_bm, w_ba, b_gate, w_out, ln2_g, ln2_b, ffn2_w_gate, ffn2_w_up, ffn2_w_down, ln3_g, ln3_b):
    batch, seq, _ = x.shape
    rows = batch * seq
    hr = x.reshape(rows, D_MODEL)
    hm = jnp.pad(meta.astype(x.dtype), ((META_ROWS - N_META, 0), (0, 0)))

    slopes = jnp.repeat(
        jnp.array([LOG2E * 2.0 ** (-8.0 * (h + 1) / A_HEADS) for h in range(A_HEADS)], F32), 128).reshape(1, 512)
    is_meta = (jnp.arange(META_ROWS) >= META_ROWS - N_META)[:, None]
    null_gate = jnp.where(jnp.arange(128) < M_HEADS, NEG, 0.0).astype(F32)[None, :]
    state0 = (jnp.zeros((1, M_HEADS, M_DK, 2 * M_DK), F32), jnp.full((1, 8, 128), NEG, F32),
              jnp.zeros((1, 8, 1024), F32))
    vec = lambda a: a.reshape(1, -1)

    for i in range(DEPTH):
        lam_init = 0.8 - 0.6 * math.exp(-0.3 * i)
        f1 = _ffn_weights(ffn1_w_gate[i], ffn1_w_up[i], ffn1_w_down[i])
        f2 = _ffn_weights(ffn2_w_gate[i], ffn2_w_up[i], ffn2_w_down[i])
        w_all, bif, bg = _proj_weights(w_in[i], b_if[i], b_gate[i])
        cw = jnp.pad(conv_w[i], ((0, 8 - CONV_W), (0, 0)))
        cb = vec(conv_b[i])
        lam_rows = jnp.pad(jnp.stack([lam_q1[i], lam_k1[i], lam_q2[i], lam_k2[i]]).astype(F32),
                           ((0, 4), (0, 128 - A_DK)))
        wbm, wba, wo = w_bm[i].astype(BF16), w_ba[i].astype(BF16), w_out[i].astype(BF16)

        hr = _ffn_ln(hr, *f1, vec(ln1_g[i]), vec(ln1_b[i]))
        hm = _ffn_ln(hm, *f1, vec(ln1_g[i]), vec(ln1_b[i]))

        qk_r, v_r, og_r, gt_r, aq_r, ak_r, av_r, gate_r = _proj(hr, w_all, bif, bg)
        qk_m, v_m, og_m, gt_m, aq_m, ak_m, av_m, gate_m = _proj(hm, w_all, bif, bg)
        qk_m = jnp.where(is_meta, qk_m, 0.0)
        gt_m = jnp.where(is_meta, gt_m, null_gate)

        xm_m, *state = _mlstm(qk_m, v_m, og_m, gt_m, cw, cb, vec(m_norm_g[i]), state0,
                              batch=1, chunk=META_ROWS)
        xm_r = _mlstm(qk_r, v_r, og_r, gt_r, cw, cb, vec(m_norm_g[i]), tuple(state),
                      batch=batch, chunk=min(MLSTM_CHUNK, seq))[0]

        xa_m = _attn_meta(aq_m, ak_m, av_m, lam_rows, slopes, vec(a_norm_g[i]), lam_init=lam_init)
        xa_r = _attn(aq_r, ak_r, av_r, ak_m, av_m, lam_rows, slopes, vec(a_norm_g[i]),
                     batch=batch, lam_init=lam_init)

        tail_params = (wbm, wba, wo, vec(ln2_g[i]), vec(ln2_b[i]), *f2, vec(ln3_g[i]), vec(ln3_b[i]))
        hr = _merge_ffn(hr, xm_r, xa_r, gate_r, *tail_params)
        hm = _merge_ffn(hm, xm_m, xa_m, gate_m, *tail_params)

    return hr.reshape(batch, seq, D_MODEL)
```

```python
import functools
import math

import jax
import jax.numpy as jnp
from jax import lax
from jax.experimental import pallas as pl
from jax.experimental.pallas import tpu as pltpu

F32 = jnp.float32
BF16 = jnp.bfloat16

D_MODEL = 1024
N_META = 16
META_ROWS = 128
M_HEADS = 4
M_DK = 128
M_WIDTH = 512
CONV_W = 4
A_HEADS = 4
A_DK = 64
A_DV = 128
A_WIDTH = 512
D_FF = 2816
FF_CHUNK = 256
N_FF_CHUNKS = D_FF // FF_CHUNK
DEPTH = 2
ALPHA = (2 * DEPTH) ** 0.25
LN_EPS = 1e-5
NEG = -1e30

SEG_QK = (0, 1024)
SEG_V = (1024, 1536)
SEG_O = (1536, 2048)
SEG_IF = (2048, 2176)
SEG_AQ = (2176, 2688)
SEG_AK = (2688, 3200)
SEG_AV = (3200, 3712)
SEG_G = (3712, 5760)
PROJ_COLS = 5760

ROW_TILE = 512
FFN_ROW_TILE = 1024
MLSTM_CHUNK = 256
MLSTM_GROUP = 8
ATTN_BLOCK = 512
ATTN_HEAD_GROUP = 2
LOG2E = math.log2(math.e)
VMEM_LIMIT = 56 * 1024 * 1024


def _nt_dot(a, b):
    return lax.dot_general(a, b, (((1,), (1,)), ((), ())), preferred_element_type=F32)


def _layer_norm(y, g, b):
    mu = jnp.mean(y, -1, keepdims=True)
    d = y - mu
    var = jnp.mean(d * d, -1, keepdims=True)
    return d * lax.rsqrt(var + LN_EPS) * g + b


def _swiglu_ln(x, wg_ref, wu_ref, wd_ref, g_ref, b_ref):
    xb = x.astype(BF16)
    acc = jnp.zeros(x.shape, F32)
    for j in range(N_FF_CHUNKS):
        cols = slice(j * FF_CHUNK, (j + 1) * FF_CHUNK)
        g = jnp.dot(xb, wg_ref[:, cols], preferred_element_type=F32)
        u = jnp.dot(xb, wu_ref[:, cols], preferred_element_type=F32)
        a = (g * jax.nn.sigmoid(g) * u).astype(BF16)
        acc = acc + jnp.dot(a, wd_ref[cols, :], preferred_element_type=F32)
    return _layer_norm(ALPHA * x + 0.5 * acc, g_ref[...], b_ref[...])


def _ffn_ln_kernel(x_ref, wg_ref, wu_ref, wd_ref, g_ref, b_ref, o_ref):
    half = x_ref.shape[0] // 2
    for rows in (slice(0, half), slice(half, 2 * half)):
        o_ref[rows, :] = _swiglu_ln(x_ref[rows, :], wg_ref, wu_ref, wd_ref, g_ref, b_ref)


def _resident(shape):
    return pl.BlockSpec(shape, lambda *_: (0,) * len(shape), pipeline_mode=pl.Buffered(1))


def _ffn_ln(x, w_gate, w_up, w_down, g, b):
    m = x.shape[0]
    tm = min(FFN_ROW_TILE, m)
    return pl.pallas_call(
        _ffn_ln_kernel,
        out_shape=jax.ShapeDtypeStruct((m, D_MODEL), F32),
        grid=(m // tm,),
        in_specs=[
            pl.BlockSpec((tm, D_MODEL), lambda i: (i, 0)),
            _resident((D_MODEL, D_FF)), _resident((D_MODEL, D_FF)), _resident((D_FF, D_MODEL)),
            _resident((1, D_MODEL)), _resident((1, D_MODEL)),
        ],
        out_specs=pl.BlockSpec((tm, D_MODEL), lambda i: (i, 0)),
        compiler_params=pltpu.CompilerParams(
            dimension_semantics=("arbitrary",), vmem_limit_bytes=VMEM_LIMIT),
    )(x, w_gate, w_up, w_down, g, b)


def _proj_kernel(h_ref, w_ref, bif_ref, bg_ref,
                 qk_ref, v_ref, og_ref, gt_ref, aq_ref, ak_ref, av_ref, gate_ref):
    hb = h_ref[...].astype(BF16)

    def seg(s):
        return jnp.dot(hb, w_ref[:, s[0]:s[1]], preferred_element_type=F32)

    gate_ref[...] = jax.nn.sigmoid(seg(SEG_G) + bg_ref[...]).astype(BF16)
    og_ref[...] = jax.nn.sigmoid(seg(SEG_O)).astype(BF16)
    z = seg(SEG_IF) + bif_ref[...]
    col = lax.broadcasted_iota(jnp.int32, z.shape, 1)
    log_f = jnp.minimum(z, 0.0) - jnp.log1p(jnp.exp(-jnp.abs(z)))
    gt_ref[...] = jnp.where(col < M_HEADS, z, jnp.where(col < 2 * M_HEADS, log_f, 0.0))
    aq_ref[...] = (seg(SEG_AQ) * (LOG2E * A_DK ** -0.5)).astype(BF16)
    qk_ref[...] = seg(SEG_QK)
    v_ref[...] = seg(SEG_V).astype(BF16)
    ak_ref[...] = seg(SEG_AK).astype(BF16)
    av_ref[...] = seg(SEG_AV).astype(BF16)


def _proj(h, w_all, b_if, b_gate):
    m = h.shape[0]
    tm = min(ROW_TILE, m)
    row = lambda n: pl.BlockSpec((tm, n), lambda i: (i, 0))
    const = lambda r, n: pl.BlockSpec((r, n), lambda i: (0, 0))
    out_shape = (
        jax.ShapeDtypeStruct((m, 1024), F32),
        jax.ShapeDtypeStruct((m, 512), BF16),
        jax.ShapeDtypeStruct((m, 512), BF16),
        jax.ShapeDtypeStruct((m, 128), F32),
        jax.ShapeDtypeStruct((m, 512), BF16),
        jax.ShapeDtypeStruct((m, 512), BF16),
        jax.ShapeDtypeStruct((m, 512), BF16),
        jax.ShapeDtypeStruct((m, 2048), BF16),
    )
    return pl.pallas_call(
        _proj_kernel,
        out_shape=out_shape,
        grid=(m // tm,),
        in_specs=[row(D_MODEL), const(D_MODEL, PROJ_COLS), const(1, 128), const(1, 2048)],
        out_specs=(row(1024), row(512), row(512), row(128), row(512), row(512), row(512), row(2048)),
        compiler_params=pltpu.CompilerParams(
            dimension_semantics=("arbitrary",), vmem_limit_bytes=VMEM_LIMIT),
    )(h, w_all, b_if, b_gate)


def _mlstm_kernel(qk_ref, v_ref, og_ref, gt_ref, cw_ref, cb_ref, ng_ref,
                  c0_ref, m0_ref, t0_ref,
                  h_ref, c_ref, m_ref, tail_ref, *, T, G):
    @pl.when(pl.program_id(1) == 0)
    def _():
        for g in range(G):
            c_ref[g] = c0_ref[0]
            m_ref[g] = m0_ref[0]
            tail_ref[g] = t0_ref[0]

    r = lax.broadcasted_iota(jnp.int32, (T, T), 0)
    c = lax.broadcasted_iota(jnp.int32, (T, T), 1)
    causal = c <= r
    eye = c == r
    tri = jnp.where(causal, 1.0, 0.0).astype(F32)
    ones = jnp.ones((T, M_DK), BF16)

    for g in range(G):
        u = jnp.concatenate([tail_ref[g], qk_ref[g]], axis=0)
        tail_ref[g] = u[T:T + 8, :]
        y = cb_ref[...] + cw_ref[CONV_W - 1:CONV_W, :] * u
        for j in range(1, CONV_W):
            y = y + cw_ref[CONV_W - 1 - j:CONV_W - j, :] * pltpu.roll(u, j, 0)
        y = y[8:8 + T, :]
        qk = y * jax.nn.sigmoid(y)

        gates = gt_ref[g]
        cums = jnp.dot(tri, gates, preferred_element_type=F32, precision=lax.Precision.HIGHEST)
        u_rows = (gates.T)[0:M_HEADS, :] - (cums.T)[M_HEADS:2 * M_HEADS, :]

        for h in range(M_HEADS):
            sl = slice(h * M_DK, (h + 1) * M_DK)
            qb = qk[:, sl].astype(BF16)
            k_h = qk[:, M_WIDTH + h * M_DK:M_WIDTH + (h + 1) * M_DK] * (M_DK ** -0.5)
            kb = k_h.astype(BF16)
            v_aug = jnp.concatenate([v_ref[g, :, sl], ones], axis=1)
            b_col = cums[:, M_HEADS + h:M_HEADS + h + 1]
            m_prev = m_ref[g, h:h + 1, 0:1]
            cn_prev = c_ref[g, h]

            um = jnp.where(causal, u_rows[h:h + 1, :], NEG)
            big_m = jnp.maximum(jnp.max(um, -1, keepdims=True), m_prev)
            p = jnp.exp(um - big_m)
            s = (_nt_dot(qb, kb) * p).astype(BF16)
            w_state = jnp.exp(m_prev - big_m)
            tot = (jnp.dot(s, v_aug, preferred_element_type=F32)
                   + w_state * jnp.dot(qb, cn_prev.astype(BF16), preferred_element_type=F32))
            num = tot[:, 0:M_DK]
            den = tot[:, M_DK:2 * M_DK]
            hh = num / jnp.maximum(jnp.abs(den), jnp.exp(-(b_col + big_m)))
            mu = jnp.mean(hh, -1, keepdims=True)
            d = hh - mu
            hn = d * lax.rsqrt(jnp.mean(d * d, -1, keepdims=True) + LN_EPS)
            h_ref[g, :, sl] = (hn * ng_ref[:, sl] * og_ref[g, :, sl].astype(F32)).astype(BF16)

            w_diag = jnp.where(eye, p[T - 1:T, :], 0.0).astype(BF16)
            vw = jnp.dot(w_diag, v_aug, preferred_element_type=F32)
            c_ref[g, h] = (w_state[T - 1:T, :] * cn_prev
                           + jnp.dot(k_h.T.astype(BF16), vw.astype(BF16), preferred_element_type=F32))
            m_ref[g, h:h + 1, :] = jnp.broadcast_to(b_col[T - 1:T, :] + big_m[T - 1:T, :], (1, 128))


def _mlstm(qk, v, og, gates, conv_w, conv_b, norm_g, state, *, batch, chunk):
    seq = qk.shape[0] // batch
    nc = seq // chunk
    group = min(MLSTM_GROUP, batch)
    c0, m0, t0 = state
    seq3 = lambda a: a.reshape(batch, seq, a.shape[-1])
    row = lambda n: pl.BlockSpec((group, chunk, n), lambda b, c: (b, c, 0))
    const2 = lambda r, n: pl.BlockSpec((r, n), lambda b, c: (0, 0))
    init3 = lambda r, n: pl.BlockSpec((1, r, n), lambda b, c: (0, 0, 0))
    out3 = lambda r, n: pl.BlockSpec((group, r, n), lambda b, c: (b, 0, 0))
    out_shape = (
        jax.ShapeDtypeStruct((batch, seq, M_WIDTH), BF16),
        jax.ShapeDtypeStruct((batch, M_HEADS, M_DK, 2 * M_DK), F32),
        jax.ShapeDtypeStruct((batch, 8, 128), F32),
        jax.ShapeDtypeStruct((batch, 8, 1024), F32),
    )
    out, c1, m1, t1 = pl.pallas_call(
        functools.partial(_mlstm_kernel, T=chunk, G=group),
        out_shape=out_shape,
        grid=(batch // group, nc),
        in_specs=[
            row(1024), row(512), row(512), row(128),
            const2(8, 1024), const2(1, 1024), const2(1, 512),
            pl.BlockSpec((1, M_HEADS, M_DK, 2 * M_DK), lambda b, c: (0, 0, 0, 0)),
            init3(8, 128), init3(8, 1024),
        ],
        out_specs=(
            row(512),
            pl.BlockSpec((group, M_HEADS, M_DK, 2 * M_DK), lambda b, c: (b, 0, 0, 0)),
            out3(8, 128), out3(8, 1024),
        ),
        compiler_params=pltpu.CompilerParams(
            dimension_semantics=("arbitrary", "arbitrary"), vmem_limit_bytes=VMEM_LIMIT),
    )(seq3(qk), seq3(v), seq3(og), seq3(gates), conv_w, conv_b, norm_g, c0, m0, t0)
    return out.reshape(batch * seq, M_WIDTH), c1, m1, t1


def _stack_maps(q):
    lane = lax.broadcasted_iota(jnp.int32, q.shape, 1)
    zero = jnp.zeros_like(q)
    return jnp.concatenate([jnp.where(lane < A_DK, q, zero), jnp.where(lane >= A_DK, q, zero)], axis=0)


def _diff_finish(acc, l, lam_ref, ng, lam_init, t):
    lam = (jnp.exp(jnp.sum(lam_ref[0:1, :] * lam_ref[1:2, :], -1, keepdims=True))
           - jnp.exp(jnp.sum(lam_ref[2:3, :] * lam_ref[3:4, :], -1, keepdims=True)) + lam_init)
    o = acc[:t] / l[:t] - lam * (acc[t:] / l[t:])
    o = o * lax.rsqrt(jnp.mean(o * o, -1, keepdims=True) + LN_EPS) * (1.0 - lam_init)
    return (o * ng).astype(BF16)


def _online_softmax_update(s, vb, m_s, l_s, acc_s):
    tiles = [s[:, t * 128:(t + 1) * 128] for t in range(s.shape[1] // 128)]
    part = functools.reduce(jnp.maximum, tiles)
    m_old = m_s[...]
    m_new = jnp.maximum(m_old, jnp.max(part, -1, keepdims=True))
    alpha = jnp.exp2(m_old - m_new)
    ps = [jnp.exp2(t - m_new) for t in tiles]
    l_s[...] = alpha * l_s[...] + functools.reduce(jnp.add, ps)
    p = jnp.concatenate([x.astype(BF16) for x in ps], axis=1)
    acc_s[...] = alpha * acc_s[...] + jnp.dot(p, vb, preferred_element_type=F32)
    m_s[...] = m_new


def _attn_kernel(q_ref, k_ref, v_ref, km_ref, vm_ref, lam_ref, sl_ref, ng_ref, o_ref,
                 m_s, l_s, acc_s, *, T, HG, lam_init):
    i = pl.program_id(2)
    q0 = i * T
    cols = [slice(h * 128, (h + 1) * 128) for h in range(HG)]
    slope = [sl_ref[:, h * 128:h * 128 + 1] for h in range(HG)]
    qs = [_stack_maps(q_ref[:, cols[h]]) for h in range(HG)]
    stats = [(m_s.at[h], l_s.at[h], acc_s.at[h]) for h in range(HG)]

    m_s[...] = jnp.full(m_s.shape, NEG, F32)
    l_s[...] = jnp.zeros(l_s.shape, F32)
    acc_s[...] = jnp.zeros(acc_s.shape, F32)

    cm = lax.broadcasted_iota(jnp.int32, (1, META_ROWS), 1)
    s_meta = [_nt_dot(qs[h], km_ref[:, cols[h]])
              + jnp.where(cm >= META_ROWS - N_META, slope[h] * (cm - (META_ROWS + q0)).astype(F32), NEG)
              for h in range(HG)]
    for h in range(HG):
        _online_softmax_update(s_meta[h], vm_ref[:, cols[h]], *stats[h])

    ck = lax.broadcasted_iota(jnp.int32, (1, T), 1)

    def rows(j):
        return pl.ds(pl.multiple_of(j * T, T), T)

    def logits(h, j):
        return _nt_dot(qs[h], k_ref[rows(j), cols[h]]) + slope[h] * (ck + (j * T - q0)).astype(F32)

    def fold_blocks(h, first, n):
        scores = [logits(h, first + k) for k in range(n)]
        for k in range(n):
            _online_softmax_update(scores[k], v_ref[rows(first + k), cols[h]], *stats[h])

    for h in range(HG):
        lax.fori_loop(0, i // 4, lambda t, carry, h=h: fold_blocks(h, 4 * t, 4) or carry, 0)

    @pl.when(i % 4 >= 2)
    def _():
        for h in range(HG):
            fold_blocks(h, i // 4 * 4, 2)

    @pl.when(i % 2 == 1)
    def _():
        scores = [logits(h, i - 1) for h in range(HG)]
        for h in range(HG):
            _online_softmax_update(scores[h], v_ref[rows(i - 1), cols[h]], *stats[h])

    r = lax.broadcasted_iota(jnp.int32, (T, T), 0)
    c = lax.broadcasted_iota(jnp.int32, (T, T), 1)
    keep = c <= r
    keep2 = jnp.concatenate([keep, keep], axis=0)
    s_diag = [jnp.where(keep2, logits(h, i), NEG) for h in range(HG)]
    for h in range(HG):
        _online_softmax_update(s_diag[h], v_ref[rows(i), cols[h]], *stats[h])
    for h in range(HG):
        l = jnp.sum(l_s[h], -1, keepdims=True)
        o_ref[:, cols[h]] = _diff_finish(acc_s[h], l, lam_ref, ng_ref[:, cols[h]], lam_init, T)


def _attn(aq, ak, av, ak_meta, av_meta, lam_rows, slopes, norm_g, *, batch, lam_init):
    m = aq.shape[0]
    seq = m // batch
    t = min(ATTN_BLOCK, seq)
    nq = seq // t
    hg = ATTN_HEAD_GROUP
    w = hg * 128
    qspec = pl.BlockSpec((t, w), lambda b, h, i: (b * nq + i, h))
    kvspec = pl.BlockSpec((seq, w), lambda b, h, i: (b, h))
    mspec = pl.BlockSpec((META_ROWS, w), lambda b, h, i: (0, h))
    hrow = pl.BlockSpec((1, w), lambda b, h, i: (0, h))
    return pl.pallas_call(
        functools.partial(_attn_kernel, T=t, HG=hg, lam_init=lam_init),
        out_shape=jax.ShapeDtypeStruct((m, A_WIDTH), BF16),
        grid=(batch, A_HEADS // hg, nq),
        in_specs=[qspec, kvspec, kvspec, mspec, mspec,
                  pl.BlockSpec((8, 128), lambda b, h, i: (0, 0)), hrow, hrow],
        out_specs=qspec,
        scratch_shapes=[pltpu.VMEM((hg, 2 * t, 128), F32), pltpu.VMEM((hg, 2 * t, 128), F32),
                        pltpu.VMEM((hg, 2 * t, A_DV), F32)],
        compiler_params=pltpu.CompilerParams(
            dimension_semantics=("arbitrary", "arbitrary", "arbitrary"),
            vmem_limit_bytes=VMEM_LIMIT),
    )(aq, ak, av, ak_meta, av_meta, lam_rows, slopes, norm_g)


def _attn_meta_kernel(q_ref, k_ref, v_ref, lam_ref, sl_ref, ng_ref, o_ref, *, lam_init):
    t = META_ROWS
    slope = sl_ref[:, 0:1]
    qs = _stack_maps(q_ref[...])
    r = lax.broadcasted_iota(jnp.int32, (t, t), 0)
    c = lax.broadcasted_iota(jnp.int32, (t, t), 1)
    bias = jnp.where((c >= t - N_META) & (c <= r), slope * (c - r).astype(F32), NEG)
    s = _nt_dot(qs, k_ref[...]) + jnp.concatenate([bias, bias], axis=0)
    p = jnp.exp2(s - jnp.max(s, -1, keepdims=True))
    l = jnp.sum(p, -1, keepdims=True)
    acc = jnp.dot(p.astype(BF16), v_ref[...], preferred_element_type=F32)
    o_ref[...] = _diff_finish(acc, l, lam_ref, ng_ref[...], lam_init, t)


def _attn_meta(aq, ak, av, lam_rows, slopes, norm_g, *, lam_init):
    blk = pl.BlockSpec((META_ROWS, 128), lambda h: (0, h))
    hrow = pl.BlockSpec((1, 128), lambda h: (0, h))
    return pl.pallas_call(
        functools.partial(_attn_meta_kernel, lam_init=lam_init),
        out_shape=jax.ShapeDtypeStruct((META_ROWS, A_WIDTH), BF16),
        grid=(A_HEADS,),
        in_specs=[blk, blk, blk, pl.BlockSpec((8, 128), lambda h: (0, 0)), hrow, hrow],
        out_specs=blk,
        compiler_params=pltpu.CompilerParams(dimension_semantics=("arbitrary",)),
    )(aq, ak, av, lam_rows, slopes, norm_g)


def _merge_ffn_kernel(h_ref, hm_ref, ha_ref, gate_ref, wbm_ref, wba_ref, wo_ref, g2_ref, b2_ref,
                      wg_ref, wu_ref, wd_ref, g3_ref, b3_ref, o_ref):
    ym = jnp.dot(hm_ref[...], wbm_ref[...], preferred_element_type=F32)
    ya = jnp.dot(ha_ref[...], wba_ref[...], preferred_element_type=F32)
    gm = gate_ref[:, 0:D_MODEL].astype(F32)
    ga = gate_ref[:, D_MODEL:2 * D_MODEL].astype(F32)
    mixed = (gm * ym + ga * ya).astype(BF16)
    mix = jnp.dot(mixed, wo_ref[...], preferred_element_type=F32)
    h2 = _layer_norm(ALPHA * h_ref[...] + mix, g2_ref[...], b2_ref[...])
    o_ref[...] = _swiglu_ln(h2, wg_ref, wu_ref, wd_ref, g3_ref, b3_ref)


def _merge_ffn(h, hm, ha, gate, w_bm, w_ba, w_out, g2, b2, w_gate, w_up, w_down, g3, b3):
    m = h.shape[0]
    tm = min(ROW_TILE, m)
    row = lambda n: pl.BlockSpec((tm, n), lambda i: (i, 0))
    return pl.pallas_call(
        _merge_ffn_kernel,
        out_shape=jax.ShapeDtypeStruct((m, D_MODEL), F32),
        grid=(m // tm,),
        in_specs=[row(D_MODEL), row(512), row(512), row(2048),
                  _resident((512, D_MODEL)), _resident((512, D_MODEL)), _resident((D_MODEL, D_MODEL)),
                  _resident((1, D_MODEL)), _resident((1, D_MODEL)),
                  _resident((D_MODEL, D_FF)), _resident((D_MODEL, D_FF)), _resident((D_FF, D_MODEL)),
                  _resident((1, D_MODEL)), _resident((1, D_MODEL))],
        out_specs=row(D_MODEL),
        compiler_params=pltpu.CompilerParams(
            dimension_semantics=("arbitrary",), vmem_limit_bytes=VMEM_LIMIT),
    )(h, hm, ha, gate, w_bm, w_ba, w_out, g2, b2, w_gate, w_up, w_down, g3, b3)


def _ffn_weights(w_gate, w_up, w_down):
    return w_gate.astype(BF16), w_up.astype(BF16), w_down.astype(BF16)


def _proj_weights(w_in, b_if, b_gate):
    w = w_in.astype(BF16)
    w_all = jnp.concatenate(
        [w[:, :2048], jnp.pad(w[:, 2048:2056], ((0, 0), (0, 120))), w[:, 2056:]], axis=1)
    return w_all, jnp.pad(b_if, (0, 120)).reshape(1, 128), b_gate.reshape(1, 2 * D_MODEL)


def kernel(x, meta, ffn1_w_gate, ffn1_w_up, ffn1_w_down, ln1_g, ln1_b, w_in, conv_w, conv_b, b_if, m_norm_g, lam_q1, lam_k1, lam_q2, lam_k2, a_norm_g, w_bm, w_ba, b_gate, w_out, ln2_g, ln2_b, ffn2_w_gate, ffn2_w_up, ffn2_w_down, ln3_g, ln3_b):
    batch, seq, _ = x.shape
    rows = batch * seq
    hr = x.reshape(rows, D_MODEL)
    hm = jnp.pad(meta.astype(x.dtype), ((META_ROWS - N_META, 0), (0, 0)))

    slopes = jnp.repeat(
        jnp.array([LOG2E * 2.0 ** (-8.0 * (h + 1) / A_HEADS) for h in range(A_HEADS)], F32), 128).reshape(1, 512)
    is_meta = (jnp.arange(META_ROWS) >= META_ROWS - N_META)[:, None]
    null_gate = jnp.where(jnp.arange(128) < M_HEADS, NEG, 0.0).astype(F32)[None, :]
    state0 = (jnp.zeros((1, M_HEADS, M_DK, 2 * M_DK), F32), jnp.full((1, 8, 128), NEG, F32),
              jnp.zeros((1, 8, 1024), F32))
    vec = lambda a: a.reshape(1, -1)

    for i in range(DEPTH):
        lam_init = 0.8 - 0.6 * math.exp(-0.3 * i)
        f1 = _ffn_weights(ffn1_w_gate[i], ffn1_w_up[i], ffn1_w_down[i])
        f2 = _ffn_weights(ffn2_w_gate[i], ffn2_w_up[i], ffn2_w_down[i])
        w_all, bif, bg = _proj_weights(w_in[i], b_if[i], b_gate[i])
        cw = jnp.pad(conv_w[i], ((0, 8 - CONV_W), (0, 0)))
        cb = vec(conv_b[i])
        lam_rows = jnp.pad(jnp.stack([lam_q1[i], lam_k1[i], lam_q2[i], lam_k2[i]]).astype(F32),
                           ((0, 4), (0, 128 - A_DK)))
        wbm, wba, wo = w_bm[i].astype(BF16), w_ba[i].astype(BF16), w_out[i].astype(BF16)

        hr = _ffn_ln(hr, *f1, vec(ln1_g[i]), vec(ln1_b[i]))
        hm = _ffn_ln(hm, *f1, vec(ln1_g[i]), vec(ln1_b[i]))

        qk_r, v_r, og_r, gt_r, aq_r, ak_r, av_r, gate_r = _proj(hr, w_all, bif, bg)
        qk_m, v_m, og_m, gt_m, aq_m, ak_m, av_m, gate_m = _proj(hm, w_all, bif, bg)
        qk_m = jnp.where(is_meta, qk_m, 0.0)
        gt_m = jnp.where(is_meta, gt_m, null_gate)

        xm_m, *state = _mlstm(qk_m, v_m, og_m, gt_m, cw, cb, vec(m_norm_g[i]), state0,
                              batch=1, chunk=META_ROWS)
        xm_r = _mlstm(qk_r, v_r, og_r, gt_r, cw, cb, vec(m_norm_g[i]), tuple(state),
                      batch=batch, chunk=min(MLSTM_CHUNK, seq))[0]

        xa_m = _attn_meta(aq_m, ak_m, av_m, lam_rows, slopes, vec(a_norm_g[i]), lam_init=lam_init)
        xa_r = _attn(aq_r, ak_r, av_r, ak_m, av_m, lam_rows, slopes, vec(a_norm_g[i]),
                     batch=batch, lam_init=lam_init)

        tail_params = (wbm, wba, wo, vec(ln2_g[i]), vec(ln2_b[i]), *f2, vec(ln3_g[i]), vec(ln3_b[i]))
        hr = _merge_ffn(hr, xm_r, xa_r, gate_r, *tail_params)
        hm = _merge_ffn(hm, xm_m, xa_m, gate_m, *tail_params)

    return hr.reshape(batch, seq, D_MODEL)
```

```python
import functools
import math

import jax
import jax.numpy as jnp
from jax import lax
from jax.experimental import pallas as pl
from jax.experimental.pallas import tpu as pltpu

F32 = jnp.float32
BF16 = jnp.bfloat16

D_MODEL = 1024
N_META = 16
META_ROWS = 128
M_HEADS = 4
M_DK = 128
M_WIDTH = 512
CONV_W = 4
A_HEADS = 4
A_DK = 64
A_DV = 128
A_WIDTH = 512
D_FF = 2816
FF_CHUNK = 256
N_FF_CHUNKS = D_FF // FF_CHUNK
DEPTH = 2
ALPHA = (2 * DEPTH) ** 0.25
LN_EPS = 1e-5
NEG = -1e30

SEG_QK = (0, 1024)
SEG_V = (1024, 1536)
SEG_O = (1536, 2048)
SEG_IF = (2048, 2176)
SEG_AQ = (2176, 2688)
SEG_AK = (2688, 3200)
SEG_AV = (3200, 3712)
SEG_G = (3712, 5760)
PROJ_COLS = 5760

ROW_TILE = 512
FFN_ROW_TILE = 1024
MLSTM_CHUNK = 256
MLSTM_GROUP = 8
ATTN_BLOCK = 512
ATTN_HEAD_GROUP = 4
LOG2E = math.log2(math.e)
VMEM_LIMIT = 56 * 1024 * 1024


def _nt_dot(a, b):
    return lax.dot_general(a, b, (((1,), (1,)), ((), ())), preferred_element_type=F32)


def _layer_norm(y, g, b):
    mu = jnp.mean(y, -1, keepdims=True)
    d = y - mu
    var = jnp.mean(d * d, -1, keepdims=True)
    return d * lax.rsqrt(var + LN_EPS) * g + b


def _swiglu_ln(x, wg_ref, wu_ref, wd_ref, g_ref, b_ref):
    xb = x.astype(BF16)
    acc = jnp.zeros(x.shape, F32)
    for j in range(N_FF_CHUNKS):
        cols = slice(j * FF_CHUNK, (j + 1) * FF_CHUNK)
        g = jnp.dot(xb, wg_ref[:, cols], preferred_element_type=F32)
        u = jnp.dot(xb, wu_ref[:, cols], preferred_element_type=F32)
        a = (g * jax.nn.sigmoid(g) * u).astype(BF16)
        acc = acc + jnp.dot(a, wd_ref[cols, :], preferred_element_type=F32)
    return _layer_norm(ALPHA * x + 0.5 * acc, g_ref[...], b_ref[...])


def _ffn_ln_kernel(x_ref, wg_ref, wu_ref, wd_ref, g_ref, b_ref, o_ref):
    half = x_ref.shape[0] // 2
    for rows in (slice(0, half), slice(half, 2 * half)):
        o_ref[rows, :] = _swiglu_ln(x_ref[rows, :], wg_ref, wu_ref, wd_ref, g_ref, b_ref)


def _resident(shape):
    return pl.BlockSpec(shape, lambda *_: (0,) * len(shape), pipeline_mode=pl.Buffered(1))


def _ffn_ln(x, w_gate, w_up, w_down, g, b):
    m = x.shape[0]
    tm = min(FFN_ROW_TILE, m)
    return pl.pallas_call(
        _ffn_ln_kernel,
        out_shape=jax.ShapeDtypeStruct((m, D_MODEL), F32),
        grid=(m // tm,),
        in_specs=[
            pl.BlockSpec((tm, D_MODEL), lambda i: (i, 0)),
            _resident((D_MODEL, D_FF)), _resident((D_MODEL, D_FF)), _resident((D_FF, D_MODEL)),
            _resident((1, D_MODEL)), _resident((1, D_MODEL)),
        ],
        out_specs=pl.BlockSpec((tm, D_MODEL), lambda i: (i, 0)),
        compiler_params=pltpu.CompilerParams(
            dimension_semantics=("arbitrary",), vmem_limit_bytes=VMEM_LIMIT),
    )(x, w_gate, w_up, w_down, g, b)


def _proj_kernel(h_ref, w_ref, bif_ref, bg_ref,
                 qk_ref, v_ref, og_ref, gt_ref, aq_ref, ak_ref, av_ref, gate_ref):
    hb = h_ref[...].astype(BF16)

    def seg(s):
        return jnp.dot(hb, w_ref[:, s[0]:s[1]], preferred_element_type=F32)

    gate_ref[...] = jax.nn.sigmoid(seg(SEG_G) + bg_ref[...]).astype(BF16)
    og_ref[...] = jax.nn.sigmoid(seg(SEG_O)).astype(BF16)
    z = seg(SEG_IF) + bif_ref[...]
    col = lax.broadcasted_iota(jnp.int32, z.shape, 1)
    log_f = jnp.minimum(z, 0.0) - jnp.log1p(jnp.exp(-jnp.abs(z)))
    gt_ref[...] = jnp.where(col < M_HEADS, z, jnp.where(col < 2 * M_HEADS, log_f, 0.0))
    aq_ref[...] = (seg(SEG_AQ) * (LOG2E * A_DK ** -0.5)).astype(BF16)
    qk_ref[...] = seg(SEG_QK)
    v_ref[...] = seg(SEG_V).astype(BF16)
    ak_ref[...] = seg(SEG_AK).astype(BF16)
    av_ref[...] = seg(SEG_AV).astype(BF16)


def _proj(h, w_all, b_if, b_gate):
    m = h.shape[0]
    tm = min(ROW_TILE, m)
    row = lambda n: pl.BlockSpec((tm, n), lambda i: (i, 0))
    const = lambda r, n: pl.BlockSpec((r, n), lambda i: (0, 0))
    out_shape = (
        jax.ShapeDtypeStruct((m, 1024), F32),
        jax.ShapeDtypeStruct((m, 512), BF16),
        jax.ShapeDtypeStruct((m, 512), BF16),
        jax.ShapeDtypeStruct((m, 128), F32),
        jax.ShapeDtypeStruct((m, 512), BF16),
        jax.ShapeDtypeStruct((m, 512), BF16),
        jax.ShapeDtypeStruct((m, 512), BF16),
        jax.ShapeDtypeStruct((m, 2048), BF16),
    )
    return pl.pallas_call(
        _proj_kernel,
        out_shape=out_shape,
        grid=(m // tm,),
        in_specs=[row(D_MODEL), const(D_MODEL, PROJ_COLS), const(1, 128), const(1, 2048)],
        out_specs=(row(1024), row(512), row(512), row(128), row(512), row(512), row(512), row(2048)),
        compiler_params=pltpu.CompilerParams(
            dimension_semantics=("arbitrary",), vmem_limit_bytes=VMEM_LIMIT),
    )(h, w_all, b_if, b_gate)


def _mlstm_kernel(qk_ref, v_ref, og_ref, gt_ref, cw_ref, cb_ref, ng_ref,
                  c0_ref, m0_ref, t0_ref,
                  h_ref, c_ref, m_ref, tail_ref, *, T, G):
    @pl.when(pl.program_id(1) == 0)
    def _():
        for g in range(G):
            c_ref[g] = c0_ref[0]
            m_ref[g] = m0_ref[0]
            tail_ref[g] = t0_ref[0]

    r = lax.broadcasted_iota(jnp.int32, (T, T), 0)
    c = lax.broadcasted_iota(jnp.int32, (T, T), 1)
    causal = c <= r
    eye = c == r
    tri = jnp.where(causal, 1.0, 0.0).astype(F32)
    ones = jnp.ones((T, M_DK), BF16)

    for g in range(G):
        u = jnp.concatenate([tail_ref[g], qk_ref[g]], axis=0)
        tail_ref[g] = u[T:T + 8, :]
        y = cb_ref[...] + cw_ref[CONV_W - 1:CONV_W, :] * u
        for j in range(1, CONV_W):
            y = y + cw_ref[CONV_W - 1 - j:CONV_W - j, :] * pltpu.roll(u, j, 0)
        y = y[8:8 + T, :]
        qk = y * jax.nn.sigmoid(y)

        gates = gt_ref[g]
        cums = jnp.dot(tri, gates, preferred_element_type=F32, precision=lax.Precision.HIGHEST)
        u_rows = (gates.T)[0:M_HEADS, :] - (cums.T)[M_HEADS:2 * M_HEADS, :]

        for h in range(M_HEADS):
            sl = slice(h * M_DK, (h + 1) * M_DK)
            qb = qk[:, sl].astype(BF16)
            k_h = qk[:, M_WIDTH + h * M_DK:M_WIDTH + (h + 1) * M_DK] * (M_DK ** -0.5)
            kb = k_h.astype(BF16)
            v_aug = jnp.concatenate([v_ref[g, :, sl], ones], axis=1)
            b_col = cums[:, M_HEADS + h:M_HEADS + h + 1]
            m_prev = m_ref[g, h:h + 1, 0:1]
            cn_prev = c_ref[g, h]

            um = jnp.where(causal, u_rows[h:h + 1, :], NEG)
            big_m = jnp.maximum(jnp.max(um, -1, keepdims=True), m_prev)
            p = jnp.exp(um - big_m)
            s = (_nt_dot(qb, kb) * p).astype(BF16)
            w_state = jnp.exp(m_prev - big_m)
            tot = (jnp.dot(s, v_aug, preferred_element_type=F32)
                   + w_state * jnp.dot(qb, cn_prev.astype(BF16), preferred_element_type=F32))
            num = tot[:, 0:M_DK]
            den = tot[:, M_DK:2 * M_DK]
            hh = num / jnp.maximum(jnp.abs(den), jnp.exp(-(b_col + big_m)))
            mu = jnp.mean(hh, -1, keepdims=True)
            d = hh - mu
            hn = d * lax.rsqrt(jnp.mean(d * d, -1, keepdims=True) + LN_EPS)
            h_ref[g, :, sl] = (hn * ng_ref[:, sl] * og_ref[g, :, sl].astype(F32)).astype(BF16)

            w_diag = jnp.where(eye, p[T - 1:T, :], 0.0).astype(BF16)
            vw = jnp.dot(w_diag, v_aug, preferred_element_type=F32)
            c_ref[g, h] = (w_state[T - 1:T, :] * cn_prev
                           + jnp.dot(k_h.T.astype(BF16), vw.astype(BF16), preferred_element_type=F32))
            m_ref[g, h:h + 1, :] = jnp.broadcast_to(b_col[T - 1:T, :] + big_m[T - 1:T, :], (1, 128))


def _mlstm(qk, v, og, gates, conv_w, conv_b, norm_g, state, *, batch, chunk):
    seq = qk.shape[0] // batch
    nc = seq // chunk
    group = min(MLSTM_GROUP, batch)
    c0, m0, t0 = state
    seq3 = lambda a: a.reshape(batch, seq, a.shape[-1])
    row = lambda n: pl.BlockSpec((group, chunk, n), lambda b, c: (b, c, 0))
    const2 = lambda r, n: pl.BlockSpec((r, n), lambda b, c: (0, 0))
    init3 = lambda r, n: pl.BlockSpec((1, r, n), lambda b, c: (0, 0, 0))
    out3 = lambda r, n: pl.BlockSpec((group, r, n), lambda b, c: (b, 0, 0))
    out_shape = (
        jax.ShapeDtypeStruct((batch, seq, M_WIDTH), BF16),
        jax.ShapeDtypeStruct((batch, M_HEADS, M_DK, 2 * M_DK), F32),
        jax.ShapeDtypeStruct((batch, 8, 128), F32),
        jax.ShapeDtypeStruct((batch, 8, 1024), F32),
    )
    out, c1, m1, t1 = pl.pallas_call(
        functools.partial(_mlstm_kernel, T=chunk, G=group),
        out_shape=out_shape,
        grid=(batch // group, nc),
        in_specs=[
            row(1024), row(512), row(512), row(128),
            const2(8, 1024), const2(1, 1024), const2(1, 512),
            pl.BlockSpec((1, M_HEADS, M_DK, 2 * M_DK), lambda b, c: (0, 0, 0, 0)),
            init3(8, 128), init3(8, 1024),
        ],
        out_specs=(
            row(512),
            pl.BlockSpec((group, M_HEADS, M_DK, 2 * M_DK), lambda b, c: (b, 0, 0, 0)),
            out3(8, 128), out3(8, 1024),
        ),
        compiler_params=pltpu.CompilerParams(
            dimension_semantics=("arbitrary", "arbitrary"), vmem_limit_bytes=VMEM_LIMIT),
    )(seq3(qk), seq3(v), seq3(og), seq3(gates), conv_w, conv_b, norm_g, c0, m0, t0)
    return out.reshape(batch * seq, M_WIDTH), c1, m1, t1


def _stack_maps(q):
    lane = lax.broadcasted_iota(jnp.int32, q.shape, 1)
    zero = jnp.zeros_like(q)
    return jnp.concatenate([jnp.where(lane < A_DK, q, zero), jnp.where(lane >= A_DK, q, zero)], axis=0)


def _diff_finish(acc, l, lam_ref, ng, lam_init, t):
    lam = (jnp.exp(jnp.sum(lam_ref[0:1, :] * lam_ref[1:2, :], -1, keepdims=True))
           - jnp.exp(jnp.sum(lam_ref[2:3, :] * lam_ref[3:4, :], -1, keepdims=True)) + lam_init)
    o = acc[:t] / l[:t] - lam * (acc[t:] / l[t:])
    o = o * lax.rsqrt(jnp.mean(o * o, -1, keepdims=True) + LN_EPS) * (1.0 - lam_init)
    return (o * ng).astype(BF16)


def _online_softmax_update(s, vb, m_s, l_s, acc_s):
    tiles = [s[:, t * 128:(t + 1) * 128] for t in range(s.shape[1] // 128)]
    part = functools.reduce(jnp.maximum, tiles)
    m_old = m_s[...]
    m_new = jnp.maximum(m_old, jnp.max(part, -1, keepdims=True))
    alpha = jnp.exp2(m_old - m_new)
    ps = [jnp.exp2(t - m_new) for t in tiles]
    l_s[...] = alpha * l_s[...] + functools.reduce(jnp.add, ps)
    p = jnp.concatenate([x.astype(BF16) for x in ps], axis=1)
    acc_s[...] = alpha * acc_s[...] + jnp.dot(p, vb, preferred_element_type=F32)
    m_s[...] = m_new


def _attn_kernel(q_ref, k_ref, v_ref, km_ref, vm_ref, lam_ref, sl_ref, ng_ref, o_ref,
                 m_s, l_s, acc_s, *, T, HG, lam_init):
    i = pl.program_id(2)
    q0 = i * T
    cols = [slice(h * 128, (h + 1) * 128) for h in range(HG)]
    slope = [sl_ref[:, h * 128:h * 128 + 1] for h in range(HG)]
    qs = [_stack_maps(q_ref[:, cols[h]]) for h in range(HG)]
    stats = [(m_s.at[h], l_s.at[h], acc_s.at[h]) for h in range(HG)]

    m_s[...] = jnp.full(m_s.shape, NEG, F32)
    l_s[...] = jnp.zeros(l_s.shape, F32)
    acc_s[...] = jnp.zeros(acc_s.shape, F32)

    cm = lax.broadcasted_iota(jnp.int32, (1, META_ROWS), 1)
    s_meta = [_nt_dot(qs[h], km_ref[:, cols[h]])
              + jnp.where(cm >= META_ROWS - N_META, slope[h] * (cm - (META_ROWS + q0)).astype(F32), NEG)
              for h in range(HG)]
    for h in range(HG):
        _online_softmax_update(s_meta[h], vm_ref[:, cols[h]], *stats[h])

    ck = lax.broadcasted_iota(jnp.int32, (1, T), 1)

    def rows(j):
        return pl.ds(pl.multiple_of(j * T, T), T)

    def logits(h, j):
        return _nt_dot(qs[h], k_ref[rows(j), cols[h]]) + slope[h] * (ck + (j * T - q0)).astype(F32)

    def fold_blocks(h, first, n):
        scores = [logits(h, first + k) for k in range(n)]
        for k in range(n):
            _online_softmax_update(scores[k], v_ref[rows(first + k), cols[h]], *stats[h])

    for h in range(HG):
        lax.fori_loop(0, i // 4, lambda t, carry, h=h: fold_blocks(h, 4 * t, 4) or carry, 0)

    @pl.when(i % 4 >= 2)
    def _():
        for h in range(HG):
            fold_blocks(h, i // 4 * 4, 2)

    @pl.when(i % 2 == 1)
    def _():
        scores = [logits(h, i - 1) for h in range(HG)]
        for h in range(HG):
            _online_softmax_update(scores[h], v_ref[rows(i - 1), cols[h]], *stats[h])

    r = lax.broadcasted_iota(jnp.int32, (T, T), 0)
    c = lax.broadcasted_iota(jnp.int32, (T, T), 1)
    keep = c <= r
    keep2 = jnp.concatenate([keep, keep], axis=0)
    s_diag = [jnp.where(keep2, logits(h, i), NEG) for h in range(HG)]
    for h in range(HG):
        _online_softmax_update(s_diag[h], v_ref[rows(i), cols[h]], *stats[h])
    for h in range(HG):
        l = jnp.sum(l_s[h], -1, keepdims=True)
        o_ref[:, cols[h]] = _diff_finish(acc_s[h], l, lam_ref, ng_ref[:, cols[h]], lam_init, T)


def _attn(aq, ak, av, ak_meta, av_meta, lam_rows, slopes, norm_g, *, batch, lam_init):
    m = aq.shape[0]
    seq = m // batch
    t = min(ATTN_BLOCK, seq)
    nq = seq // t
    hg = ATTN_HEAD_GROUP
    w = hg * 128
    qspec = pl.BlockSpec((t, w), lambda b, h, i: (b * nq + i, h))
    kvspec = pl.BlockSpec((seq, w), lambda b, h, i: (b, h), pipeline_mode=pl.Buffered(1))
    mspec = pl.BlockSpec((META_ROWS, w), lambda b, h, i: (0, h))
    hrow = pl.BlockSpec((1, w), lambda b, h, i: (0, h))
    return pl.pallas_call(
        functools.partial(_attn_kernel, T=t, HG=hg, lam_init=lam_init),
        out_shape=jax.ShapeDtypeStruct((m, A_WIDTH), BF16),
        grid=(batch, A_HEADS // hg, nq),
        in_specs=[qspec, kvspec, kvspec, mspec, mspec,
                  pl.BlockSpec((8, 128), lambda b, h, i: (0, 0)), hrow, hrow],
        out_specs=qspec,
        scratch_shapes=[pltpu.VMEM((hg, 2 * t, 128), F32), pltpu.VMEM((hg, 2 * t, 128), F32),
                        pltpu.VMEM((hg, 2 * t, A_DV), F32)],
        compiler_params=pltpu.CompilerParams(
            dimension_semantics=("arbitrary", "arbitrary", "arbitrary"),
            vmem_limit_bytes=VMEM_LIMIT),
    )(aq, ak, av, ak_meta, av_meta, lam_rows, slopes, norm_g)


def _attn_meta_kernel(q_ref, k_ref, v_ref, lam_ref, sl_ref, ng_ref, o_ref, *, lam_init):
    t = META_ROWS
    slope = sl_ref[:, 0:1]
    qs = _stack_maps(q_ref[...])
    r = lax.broadcasted_iota(jnp.int32, (t, t), 0)
    c = lax.broadcasted_iota(jnp.int32, (t, t), 1)
    bias = jnp.where((c >= t - N_META) & (c <= r), slope * (c - r).astype(F32), NEG)
    s = _nt_dot(qs, k_ref[...]) + jnp.concatenate([bias, bias], axis=0)
    p = jnp.exp2(s - jnp.max(s, -1, keepdims=True))
    l = jnp.sum(p, -1, keepdims=True)
    acc = jnp.dot(p.astype(BF16), v_ref[...], preferred_element_type=F32)
    o_ref[...] = _diff_finish(acc, l, lam_ref, ng_ref[...], lam_init, t)


def _attn_meta(aq, ak, av, lam_rows, slopes, norm_g, *, lam_init):
    blk = pl.BlockSpec((META_ROWS, 128), lambda h: (0, h))
    hrow = pl.BlockSpec((1, 128), lambda h: (0, h))
    return pl.pallas_call(
        functools.partial(_attn_meta_kernel, lam_init=lam_init),
        out_shape=jax.ShapeDtypeStruct((META_ROWS, A_WIDTH), BF16),
        grid=(A_HEADS,),
        in_specs=[blk, blk, blk, pl.BlockSpec((8, 128), lambda h: (0, 0)), hrow, hrow],
        out_specs=blk,
        compiler_params=pltpu.CompilerParams(dimension_semantics=("arbitrary",)),
    )(aq, ak, av, lam_rows, slopes, norm_g)


def _merge_ffn_kernel(h_ref, hm_ref, ha_ref, gate_ref, wbm_ref, wba_ref, wo_ref, g2_ref, b2_ref,
                      wg_ref, wu_ref, wd_ref, g3_ref, b3_ref, o_ref):
    ym = jnp.dot(hm_ref[...], wbm_ref[...], preferred_element_type=F32)
    ya = jnp.dot(ha_ref[...], wba_ref[...], preferred_element_type=F32)
    gm = gate_ref[:, 0:D_MODEL].astype(F32)
    ga = gate_ref[:, D_MODEL:2 * D_MODEL].astype(F32)
    mixed = (gm * ym + ga * ya).astype(BF16)
    mix = jnp.dot(mixed, wo_ref[...], preferred_element_type=F32)
    h2 = _layer_norm(ALPHA * h_ref[...] + mix, g2_ref[...], b2_ref[...])
    o_ref[...] = _swiglu_ln(h2, wg_ref, wu_ref, wd_ref, g3_ref, b3_ref)


def _merge_ffn(h, hm, ha, gate, w_bm, w_ba, w_out, g2, b2, w_gate, w_up, w_down, g3, b3):
    m = h.shape[0]
    tm = min(ROW_TILE, m)
    row = lambda n: pl.BlockSpec((tm, n), lambda i: (i, 0))
    return pl.pallas_call(
        _merge_ffn_kernel,
        out_shape=jax.ShapeDtypeStruct((m, D_MODEL), F32),
        grid=(m // tm,),
        in_specs=[row(D_MODEL), row(512), row(512), row(2048),
                  _resident((512, D_MODEL)), _resident((512, D_MODEL)), _resident((D_MODEL, D_MODEL)),
                  _resident((1, D_MODEL)), _resident((1, D_MODEL)),
                  _resident((D_MODEL, D_FF)), _resident((D_MODEL, D_FF)), _resident((D_FF, D_MODEL)),
                  _resident((1, D_MODEL)), _resident((1, D_MODEL))],
        out_specs=row(D_MODEL),
        compiler_params=pltpu.CompilerParams(
            dimension_semantics=("arbitrary",), vmem_limit_bytes=VMEM_LIMIT),
    )(h, hm, ha, gate, w_bm, w_ba, w_out, g2, b2, w_gate, w_up, w_down, g3, b3)


def _ffn_weights(w_gate, w_up, w_down):
    return w_gate.astype(BF16), w_up.astype(BF16), w_down.astype(BF16)


def _proj_weights(w_in, b_if, b_gate):
    w = w_in.astype(BF16)
    w_all = jnp.concatenate(
        [w[:, :2048], jnp.pad(w[:, 2048:2056], ((0, 0), (0, 120))), w[:, 2056:]], axis=1)
    return w_all, jnp.pad(b_if, (0, 120)).reshape(1, 128), b_gate.reshape(1, 2 * D_MODEL)


def kernel(x, meta, ffn1_w_gate, ffn1_w_up, ffn1_w_down, ln1_g, ln1_b, w_in, conv_w, conv_b, b_if, m_norm_g, lam_q1, lam_k1, lam_q2, lam_k2, a_norm_g, w_bm, w_ba, b_gate, w_out, ln2_g, ln2_b, ffn2_w_gate, ffn2_w_up, ffn2_w_down, ln3_g, ln3_b):
    batch, seq, _ = x.shape
    rows = batch * seq
    hr = x.reshape(rows, D_MODEL)
    hm = jnp.pad(meta.astype(x.dtype), ((META_ROWS - N_META, 0), (0, 0)))

    slopes = jnp.repeat(
        jnp.array([LOG2E * 2.0 ** (-8.0 * (h + 1) / A_HEADS) for h in range(A_HEADS)], F32), 128).reshape(1, 512)
    is_meta = (jnp.arange(META_ROWS) >= META_ROWS - N_META)[:, None]
    null_gate = jnp.where(jnp.arange(128) < M_HEADS, NEG, 0.0).astype(F32)[None, :]
    state0 = (jnp.zeros((1, M_HEADS, M_DK, 2 * M_DK), F32), jnp.full((1, 8, 128), NEG, F32),
              jnp.zeros((1, 8, 1024), F32))
    vec = lambda a: a.reshape(1, -1)

    for i in range(DEPTH):
        lam_init = 0.8 - 0.6 * math.exp(-0.3 * i)
        f1 = _ffn_weights(ffn1_w_gate[i], ffn1_w_up[i], ffn1_w_down[i])
        f2 = _ffn_weights(ffn2_w_gate[i], ffn2_w_up[i], ffn2_w_down[i])
        w_all, bif, bg = _proj_weights(w_in[i], b_if[i], b_gate[i])
        cw = jnp.pad(conv_w[i], ((0, 8 - CONV_W), (0, 0)))
        cb = vec(conv_b[i])
        lam_rows = jnp.pad(jnp.stack([lam_q1[i], lam_k1[i], lam_q2[i], lam_k2[i]]).astype(F32),
                           ((0, 4), (0, 128 - A_DK)))
        wbm, wba, wo = w_bm[i].astype(BF16), w_ba[i].astype(BF16), w_out[i].astype(BF16)

        hr = _ffn_ln(hr, *f1, vec(ln1_g[i]), vec(ln1_b[i]))
        hm = _ffn_ln(hm, *f1, vec(ln1_g[i]), vec(ln1_b[i]))

        qk_r, v_r, og_r, gt_r, aq_r, ak_r, av_r, gate_r = _proj(hr, w_all, bif, bg)
        qk_m, v_m, og_m, gt_m, aq_m, ak_m, av_m, gate_m = _proj(hm, w_all, bif, bg)
        qk_m = jnp.where(is_meta, qk_m, 0.0)
        gt_m = jnp.where(is_meta, gt_m, null_gate)

        xm_m, *state = _mlstm(qk_m, v_m, og_m, gt_m, cw, cb, vec(m_norm_g[i]), state0,
                              batch=1, chunk=META_ROWS)
        xm_r = _mlstm(qk_r, v_r, og_r, gt_r, cw, cb, vec(m_norm_g[i]), tuple(state),
                      batch=batch, chunk=min(MLSTM_CHUNK, seq))[0]

        xa_m = _attn_meta(aq_m, ak_m, av_m, lam_rows, slopes, vec(a_norm_g[i]), lam_init=lam_init)
        xa_r = _attn(aq_r, ak_r, av_r, ak_m, av_m, lam_rows, slopes, vec(a_norm_g[i]),
                     batch=batch, lam_init=lam_init)

        tail_params = (wbm, wba, wo, vec(ln2_g[i]), vec(ln2_b[i]), *f2, vec(ln3_g[i]), vec(ln3_b[i]))
        hr = _merge_ffn(hr, xm_r, xa_r, gate_r, *tail_params)
        hm = _merge_ffn(hm, xm_m, xa_m, gate_m, *tail_params)

    return hr.reshape(batch, seq, D_MODEL)
```

```python
import functools
import math

import jax
import jax.numpy as jnp
from jax import lax
from jax.experimental import pallas as pl
from jax.experimental.pallas import tpu as pltpu

F32 = jnp.float32
BF16 = jnp.bfloat16

D_MODEL = 1024
N_META = 16
META_ROWS = 128
M_HEADS = 4
M_DK = 128
M_WIDTH = 512
CONV_W = 4
A_HEADS = 4
A_DK = 64
A_DV = 128
A_WIDTH = 512
D_FF = 2816
FF_CHUNK = 256
N_FF_CHUNKS = D_FF // FF_CHUNK
DEPTH = 2
ALPHA = (2 * DEPTH) ** 0.25
LN_EPS = 1e-5
NEG = -1e30

SEG_QK = (0, 1024)
SEG_V = (1024, 1536)
SEG_O = (1536, 2048)
SEG_IF = (2048, 2176)
SEG_AQ = (2176, 2688)
SEG_AK = (2688, 3200)
SEG_AV = (3200, 3712)
SEG_G = (3712, 5760)
PROJ_COLS = 5760

ROW_TILE = 512
FFN_ROW_TILE = 1024
MLSTM_CHUNK = 256
MLSTM_GROUP = 8
ATTN_BLOCK = 512
ATTN_HEAD_GROUP = 4
LOG2E = math.log2(math.e)
VMEM_LIMIT = 56 * 1024 * 1024


def _nt_dot(a, b):
    return lax.dot_general(a, b, (((1,), (1,)), ((), ())), preferred_element_type=F32)


def _layer_norm(y, g, b):
    mu = jnp.mean(y, -1, keepdims=True)
    d = y - mu
    var = jnp.mean(d * d, -1, keepdims=True)
    return d * lax.rsqrt(var + LN_EPS) * g + b


def _swiglu_ln(x, wg_ref, wu_ref, wd_ref, g_ref, b_ref):
    xb = x.astype(BF16)
    acc = jnp.zeros(x.shape, F32)
    for j in range(N_FF_CHUNKS):
        cols = slice(j * FF_CHUNK, (j + 1) * FF_CHUNK)
        g = jnp.dot(xb, wg_ref[:, cols], preferred_element_type=F32)
        u = jnp.dot(xb, wu_ref[:, cols], preferred_element_type=F32)
        a = (g * jax.nn.sigmoid(g) * u).astype(BF16)
        acc = acc + jnp.dot(a, wd_ref[cols, :], preferred_element_type=F32)
    return _layer_norm(ALPHA * x + 0.5 * acc, g_ref[...], b_ref[...])


def _ffn_ln_kernel(x_ref, wg_ref, wu_ref, wd_ref, g_ref, b_ref, o_ref):
    half = x_ref.shape[0] // 2
    for rows in (slice(0, half), slice(half, 2 * half)):
        o_ref[rows, :] = _swiglu_ln(x_ref[rows, :], wg_ref, wu_ref, wd_ref, g_ref, b_ref)


def _resident(shape):
    return pl.BlockSpec(shape, lambda *_: (0,) * len(shape), pipeline_mode=pl.Buffered(1))


def _ffn_ln(x, w_gate, w_up, w_down, g, b):
    m = x.shape[0]
    tm = min(FFN_ROW_TILE, m)
    return pl.pallas_call(
        _ffn_ln_kernel,
        out_shape=jax.ShapeDtypeStruct((m, D_MODEL), F32),
        grid=(m // tm,),
        in_specs=[
            pl.BlockSpec((tm, D_MODEL), lambda i: (i, 0)),
            _resident((D_MODEL, D_FF)), _resident((D_MODEL, D_FF)), _resident((D_FF, D_MODEL)),
            _resident((1, D_MODEL)), _resident((1, D_MODEL)),
        ],
        out_specs=pl.BlockSpec((tm, D_MODEL), lambda i: (i, 0)),
        compiler_params=pltpu.CompilerParams(
            dimension_semantics=("arbitrary",), vmem_limit_bytes=VMEM_LIMIT),
    )(x, w_gate, w_up, w_down, g, b)


def _proj_kernel(h_ref, w_ref, bif_ref, bg_ref,
                 qk_ref, v_ref, og_ref, gt_ref, aq_ref, ak_ref, av_ref, gate_ref):
    hb = h_ref[...].astype(BF16)

    def seg(s):
        return jnp.dot(hb, w_ref[:, s[0]:s[1]], preferred_element_type=F32)

    gate_ref[...] = jax.nn.sigmoid(seg(SEG_G) + bg_ref[...]).astype(BF16)
    og_ref[...] = jax.nn.sigmoid(seg(SEG_O)).astype(BF16)
    z = seg(SEG_IF) + bif_ref[...]
    col = lax.broadcasted_iota(jnp.int32, z.shape, 1)
    log_f = jnp.minimum(z, 0.0) - jnp.log1p(jnp.exp(-jnp.abs(z)))
    gt_ref[...] = jnp.where(col < M_HEADS, z, jnp.where(col < 2 * M_HEADS, log_f, 0.0))
    aq_ref[...] = (seg(SEG_AQ) * (LOG2E * A_DK ** -0.5)).astype(BF16)
    qk_ref[...] = seg(SEG_QK)
    v_ref[...] = seg(SEG_V).astype(BF16)
    ak_ref[...] = seg(SEG_AK).astype(BF16)
    av_ref[...] = seg(SEG_AV).astype(BF16)


def _proj(h, w_all, b_if, b_gate):
    m = h.shape[0]
    tm = min(ROW_TILE, m)
    row = lambda n: pl.BlockSpec((tm, n), lambda i: (i, 0))
    const = lambda r, n: pl.BlockSpec((r, n), lambda i: (0, 0))
    out_shape = (
        jax.ShapeDtypeStruct((m, 1024), F32),
        jax.ShapeDtypeStruct((m, 512), BF16),
        jax.ShapeDtypeStruct((m, 512), BF16),
        jax.ShapeDtypeStruct((m, 128), F32),
        jax.ShapeDtypeStruct((m, 512), BF16),
        jax.ShapeDtypeStruct((m, 512), BF16),
        jax.ShapeDtypeStruct((m, 512), BF16),
        jax.ShapeDtypeStruct((m, 2048), BF16),
    )
    return pl.pallas_call(
        _proj_kernel,
        out_shape=out_shape,
        grid=(m // tm,),
        in_specs=[row(D_MODEL), const(D_MODEL, PROJ_COLS), const(1, 128), const(1, 2048)],
        out_specs=(row(1024), row(512), row(512), row(128), row(512), row(512), row(512), row(2048)),
        compiler_params=pltpu.CompilerParams(
            dimension_semantics=("arbitrary",), vmem_limit_bytes=VMEM_LIMIT),
    )(h, w_all, b_if, b_gate)


def _mlstm_kernel(qk_ref, v_ref, og_ref, gt_ref, cw_ref, cb_ref, ng_ref,
                  c0_ref, m0_ref, t0_ref,
                  h_ref, c_ref, m_ref, tail_ref, *, T, G):
    @pl.when(pl.program_id(1) == 0)
    def _():
        for g in range(G):
            c_ref[g] = c0_ref[0]
            m_ref[g] = m0_ref[0]
            tail_ref[g] = t0_ref[0]

    r = lax.broadcasted_iota(jnp.int32, (T, T), 0)
    c = lax.broadcasted_iota(jnp.int32, (T, T), 1)
    causal = c <= r
    eye = c == r
    tri = jnp.where(causal, 1.0, 0.0).astype(F32)
    ones = jnp.ones((T, M_DK), BF16)

    for g in range(G):
        u = jnp.concatenate([tail_ref[g], qk_ref[g]], axis=0)
        tail_ref[g] = u[T:T + 8, :]
        y = cb_ref[...] + cw_ref[CONV_W - 1:CONV_W, :] * u
        for j in range(1, CONV_W):
            y = y + cw_ref[CONV_W - 1 - j:CONV_W - j, :] * pltpu.roll(u, j, 0)
        y = y[8:8 + T, :]
        qk = y * jax.nn.sigmoid(y)

        gates = gt_ref[g]
        cums = jnp.dot(tri, gates, preferred_element_type=F32, precision=lax.Precision.HIGHEST)
        u_rows = (gates.T)[0:M_HEADS, :] - (cums.T)[M_HEADS:2 * M_HEADS, :]

        for h in range(M_HEADS):
            sl = slice(h * M_DK, (h + 1) * M_DK)
            qb = qk[:, sl].astype(BF16)
            k_h = qk[:, M_WIDTH + h * M_DK:M_WIDTH + (h + 1) * M_DK] * (M_DK ** -0.5)
            kb = k_h.astype(BF16)
            v_aug = jnp.concatenate([v_ref[g, :, sl], ones], axis=1)
            b_col = cums[:, M_HEADS + h:M_HEADS + h + 1]
            m_prev = m_ref[g, h:h + 1, 0:1]
            cn_prev = c_ref[g, h]

            um = jnp.where(causal, u_rows[h:h + 1, :], NEG)
            big_m = jnp.maximum(jnp.max(um, -1, keepdims=True), m_prev)
            p = jnp.exp(um - big_m)
            s = (_nt_dot(qb, kb) * p).astype(BF16)
            w_state = jnp.exp(m_prev - big_m)
            tot = (jnp.dot(s, v_aug, preferred_element_type=F32)
                   + w_state * jnp.dot(qb, cn_prev.astype(BF16), preferred_element_type=F32))
            num = tot[:, 0:M_DK]
            den = tot[:, M_DK:2 * M_DK]
            hh = num / jnp.maximum(jnp.abs(den), jnp.exp(-(b_col + big_m)))
            mu = jnp.mean(hh, -1, keepdims=True)
            d = hh - mu
            hn = d * lax.rsqrt(jnp.mean(d * d, -1, keepdims=True) + LN_EPS)
            h_ref[g, :, sl] = (hn * ng_ref[:, sl] * og_ref[g, :, sl].astype(F32)).astype(BF16)

            w_diag = jnp.where(eye, p[T - 1:T, :], 0.0).astype(BF16)
            vw = jnp.dot(w_diag, v_aug, preferred_element_type=F32)
            c_ref[g, h] = (w_state[T - 1:T, :] * cn_prev
                           + jnp.dot(k_h.T.astype(BF16), vw.astype(BF16), preferred_element_type=F32))
            m_ref[g, h:h + 1, :] = jnp.broadcast_to(b_col[T - 1:T, :] + big_m[T - 1:T, :], (1, 128))


def _mlstm(qk, v, og, gates, conv_w, conv_b, norm_g, state, *, batch, chunk):
    seq = qk.shape[0] // batch
    nc = seq // chunk
    group = min(MLSTM_GROUP, batch)
    c0, m0, t0 = state
    seq3 = lambda a: a.reshape(batch, seq, a.shape[-1])
    row = lambda n: pl.BlockSpec((group, chunk, n), lambda b, c: (b, c, 0))
    const2 = lambda r, n: pl.BlockSpec((r, n), lambda b, c: (0, 0))
    init3 = lambda r, n: pl.BlockSpec((1, r, n), lambda b, c: (0, 0, 0))
    out3 = lambda r, n: pl.BlockSpec((group, r, n), lambda b, c: (b, 0, 0))
    out_shape = (
        jax.ShapeDtypeStruct((batch, seq, M_WIDTH), BF16),
        jax.ShapeDtypeStruct((batch, M_HEADS, M_DK, 2 * M_DK), F32),
        jax.ShapeDtypeStruct((batch, 8, 128), F32),
        jax.ShapeDtypeStruct((batch, 8, 1024), F32),
    )
    out, c1, m1, t1 = pl.pallas_call(
        functools.partial(_mlstm_kernel, T=chunk, G=group),
        out_shape=out_shape,
        grid=(batch // group, nc),
        in_specs=[
            row(1024), row(512), row(512), row(128),
            const2(8, 1024), const2(1, 1024), const2(1, 512),
            pl.BlockSpec((1, M_HEADS, M_DK, 2 * M_DK), lambda b, c: (0, 0, 0, 0)),
            init3(8, 128), init3(8, 1024),
        ],
        out_specs=(
            row(512),
            pl.BlockSpec((group, M_HEADS, M_DK, 2 * M_DK), lambda b, c: (b, 0, 0, 0)),
            out3(8, 128), out3(8, 1024),
        ),
        compiler_params=pltpu.CompilerParams(
            dimension_semantics=("arbitrary", "arbitrary"), vmem_limit_bytes=VMEM_LIMIT),
    )(seq3(qk), seq3(v), seq3(og), seq3(gates), conv_w, conv_b, norm_g, c0, m0, t0)
    return out.reshape(batch * seq, M_WIDTH), c1, m1, t1


def _stack_maps(q):
    lane = lax.broadcasted_iota(jnp.int32, q.shape, 1)
    zero = jnp.zeros_like(q)
    return jnp.concatenate([jnp.where(lane < A_DK, q, zero), jnp.where(lane >= A_DK, q, zero)], axis=0)


def _diff_finish(acc, l, lam_ref, ng, lam_init, t):
    lam = (jnp.exp(jnp.sum(lam_ref[0:1, :] * lam_ref[1:2, :], -1, keepdims=True))
           - jnp.exp(jnp.sum(lam_ref[2:3, :] * lam_ref[3:4, :], -1, keepdims=True)) + lam_init)
    o = acc[:t] / l[:t] - lam * (acc[t:] / l[t:])
    o = o * lax.rsqrt(jnp.mean(o * o, -1, keepdims=True) + LN_EPS) * (1.0 - lam_init)
    return (o * ng).astype(BF16)


ACC_ROWS = A_DV + 16


def _fold_keys(s_t, offset, v_t, m_s, acc_s):
    keys, width = s_t.shape
    m_old = m_s[...]
    m_new = jnp.maximum(m_old, jnp.max(s_t, axis=0, keepdims=True) + offset)
    alpha = jnp.exp2(m_old - m_new)
    p3 = jnp.exp2(s_t.reshape(keys // 8, 8, width) - (m_new - offset)[None])
    p = p3.reshape(keys, width).astype(BF16)
    acc3 = acc_s[...].reshape(ACC_ROWS // 8, 8, width)
    acc_s[...] = ((acc3 * alpha[None]).reshape(acc_s.shape)
                  + jnp.dot(v_t, p, preferred_element_type=F32))
    m_s[...] = m_new


def _attn_kernel(q_ref, k_ref, v_ref, km_ref, vm_ref, ab_ref, lam_ref, sl_ref, ngt_ref, o_ref,
                 m_s, acc_s, *, T, HG, lam_init):
    i = pl.program_id(2)
    q0 = i * T
    R = 2 * T
    cols = [slice(h * 128, (h + 1) * 128) for h in range(HG)]
    slope = [sl_ref[:, h * 128:h * 128 + 1] for h in range(HG)]
    row = lax.broadcasted_iota(jnp.int32, (128, R), 0)
    ones_rows = jnp.where(row < 3, 1.0, 0.0).astype(BF16)
    qs_t = [jnp.concatenate([_stack_maps(q_ref[:, cols[h]]).astype(F32).T.astype(BF16), ones_rows], axis=0)
            for h in range(HG)]
    stats = [(m_s.at[h], acc_s.at[h]) for h in range(HG)]

    m_s[...] = jnp.full(m_s.shape, NEG, F32)
    acc_s[...] = jnp.zeros(acc_s.shape, F32)

    def rows(j):
        return pl.ds(pl.multiple_of(j * T, T), T)

    def values_t(v):
        v_t = v.astype(F32).T.astype(BF16)
        return jnp.concatenate([v_t, jnp.ones((ACC_ROWS - A_DV, v_t.shape[1]), BF16)], axis=0)

    zero_off = jnp.zeros((1, 1), F32)

    km_pos = lax.broadcasted_iota(jnp.int32, (META_ROWS, 128), 0)
    s_meta = []
    for h in range(HG):
        bias = jnp.where(km_pos >= META_ROWS - N_META,
                         slope[h] * (km_pos - (META_ROWS + q0)).astype(F32), NEG)
        s_meta.append(jnp.dot(km_ref[:, cols[h]], qs_t[h][0:128], preferred_element_type=F32)
                      + jnp.concatenate([bias] * (R // 128), axis=1))
    for h in range(HG):
        _fold_keys(s_meta[h], zero_off, values_t(vm_ref[:, cols[h]]), *stats[h])

    def logits(h, j):
        k_aug = jnp.concatenate([k_ref[rows(j), cols[h]], ab_ref[:, cols[h]]], axis=1)
        return jnp.dot(k_aug, qs_t[h], preferred_element_type=F32)

    def offset(h, j):
        return slope[h] * (j * T - q0).astype(F32)

    def fold_blocks(h, first, n):
        scores = [logits(h, first + k) for k in range(n)]
        for k in range(n):
            _fold_keys(scores[k], offset(h, first + k), values_t(v_ref[rows(first + k), cols[h]]), *stats[h])

    for h in range(HG):
        lax.fori_loop(0, i // 4, lambda t, carry, h=h: fold_blocks(h, 4 * t, 4) or carry, 0)

    @pl.when(i % 4 >= 2)
    def _():
        for h in range(HG):
            fold_blocks(h, i // 4 * 4, 2)

    @pl.when(i % 2 == 1)
    def _():
        scores = [logits(h, i - 1) for h in range(HG)]
        for h in range(HG):
            _fold_keys(scores[h], offset(h, i - 1), values_t(v_ref[rows(i - 1), cols[h]]), *stats[h])

    key = lax.broadcasted_iota(jnp.int32, (T, T), 0)
    qry = lax.broadcasted_iota(jnp.int32, (T, T), 1)
    keep = key <= qry
    keep2 = jnp.concatenate([keep, keep], axis=1)
    s_diag = [jnp.where(keep2, logits(h, i), NEG) for h in range(HG)]
    for h in range(HG):
        _fold_keys(s_diag[h], zero_off, values_t(v_ref[rows(i), cols[h]]), *stats[h])
    lam = (jnp.exp(jnp.sum(lam_ref[0:1, :] * lam_ref[1:2, :], -1, keepdims=True))
           - jnp.exp(jnp.sum(lam_ref[2:3, :] * lam_ref[3:4, :], -1, keepdims=True)) + lam_init)
    for h in range(HG):
        acc = acc_s[h, 0:A_DV, :]
        l = acc_s[h, A_DV:A_DV + 1, :]
        o_t = acc[:, 0:T] / l[:, 0:T] - lam * (acc[:, T:R] / l[:, T:R])
        o_t = o_t * lax.rsqrt(jnp.mean(o_t * o_t, axis=0, keepdims=True) + LN_EPS) * (1.0 - lam_init)
        o_t = jnp.concatenate([o_t[:, t * 128:(t + 1) * 128] * ngt_ref[cols[h], :] for t in range(T // 128)], axis=1)
        o_ref[:, cols[h]] = o_t.T.astype(BF16)


def _alibi_columns(slopes_log2, t):
    x = slopes_log2[None, :] * jnp.arange(t, dtype=F32)[:, None]
    hi = x.astype(BF16)
    mid = (x - hi.astype(F32)).astype(BF16)
    lo = (x - hi.astype(F32) - mid.astype(F32)).astype(BF16)
    cols = jnp.stack([hi, mid, lo], axis=-1)
    return jnp.pad(cols, ((0, 0), (0, 0), (0, 125))).reshape(t, A_HEADS * 128)


def _attn(aq, ak, av, ak_meta, av_meta, alibi_cols, lam_rows, slopes, norm_g, *, batch, lam_init):
    m = aq.shape[0]
    seq = m // batch
    t = alibi_cols.shape[0]
    nq = seq // t
    hg = ATTN_HEAD_GROUP
    w = hg * 128
    qspec = pl.BlockSpec((t, w), lambda b, h, i: (b * nq + i, h))
    kvspec = pl.BlockSpec((seq, w), lambda b, h, i: (b, h), pipeline_mode=pl.Buffered(1))
    mspec = pl.BlockSpec((META_ROWS, w), lambda b, h, i: (0, h))
    hrow = pl.BlockSpec((1, w), lambda b, h, i: (0, h))
    norm_g_t = jnp.broadcast_to(norm_g.reshape(A_WIDTH, 1), (A_WIDTH, 128))
    return pl.pallas_call(
        functools.partial(_attn_kernel, T=t, HG=hg, lam_init=lam_init),
        out_shape=jax.ShapeDtypeStruct((m, A_WIDTH), BF16),
        grid=(batch, A_HEADS // hg, nq),
        in_specs=[qspec, kvspec, kvspec, mspec, mspec,
                  pl.BlockSpec((t, w), lambda b, h, i: (0, h)),
                  pl.BlockSpec((8, 128), lambda b, h, i: (0, 0)), hrow,
                  pl.BlockSpec((w, 128), lambda b, h, i: (h, 0))],
        out_specs=qspec,
        scratch_shapes=[pltpu.VMEM((hg, 8, 2 * t), F32), pltpu.VMEM((hg, ACC_ROWS, 2 * t), F32)],
        compiler_params=pltpu.CompilerParams(
            dimension_semantics=("arbitrary", "arbitrary", "arbitrary"),
            vmem_limit_bytes=VMEM_LIMIT),
    )(aq, ak, av, ak_meta, av_meta, alibi_cols, lam_rows, slopes, norm_g_t)


def _attn_meta_kernel(q_ref, k_ref, v_ref, lam_ref, sl_ref, ng_ref, o_ref, *, lam_init):
    t = META_ROWS
    slope = sl_ref[:, 0:1]
    qs = _stack_maps(q_ref[...])
    r = lax.broadcasted_iota(jnp.int32, (t, t), 0)
    c = lax.broadcasted_iota(jnp.int32, (t, t), 1)
    bias = jnp.where((c >= t - N_META) & (c <= r), slope * (c - r).astype(F32), NEG)
    s = _nt_dot(qs, k_ref[...]) + jnp.concatenate([bias, bias], axis=0)
    p = jnp.exp2(s - jnp.max(s, -1, keepdims=True))
    l = jnp.sum(p, -1, keepdims=True)
    acc = jnp.dot(p.astype(BF16), v_ref[...], preferred_element_type=F32)
    o_ref[...] = _diff_finish(acc, l, lam_ref, ng_ref[...], lam_init, t)


def _attn_meta(aq, ak, av, lam_rows, slopes, norm_g, *, lam_init):
    blk = pl.BlockSpec((META_ROWS, 128), lambda h: (0, h))
    hrow = pl.BlockSpec((1, 128), lambda h: (0, h))
    return pl.pallas_call(
        functools.partial(_attn_meta_kernel, lam_init=lam_init),
        out_shape=jax.ShapeDtypeStruct((META_ROWS, A_WIDTH), BF16),
        grid=(A_HEADS,),
        in_specs=[blk, blk, blk, pl.BlockSpec((8, 128), lambda h: (0, 0)), hrow, hrow],
        out_specs=blk,
        compiler_params=pltpu.CompilerParams(dimension_semantics=("arbitrary",)),
    )(aq, ak, av, lam_rows, slopes, norm_g)


def _merge_ffn_kernel(h_ref, hm_ref, ha_ref, gate_ref, wbm_ref, wba_ref, wo_ref, g2_ref, b2_ref,
                      wg_ref, wu_ref, wd_ref, g3_ref, b3_ref, o_ref):
    ym = jnp.dot(hm_ref[...], wbm_ref[...], preferred_element_type=F32)
    ya = jnp.dot(ha_ref[...], wba_ref[...], preferred_element_type=F32)
    gm = gate_ref[:, 0:D_MODEL].astype(F32)
    ga = gate_ref[:, D_MODEL:2 * D_MODEL].astype(F32)
    mixed = (gm * ym + ga * ya).astype(BF16)
    mix = jnp.dot(mixed, wo_ref[...], preferred_element_type=F32)
    h2 = _layer_norm(ALPHA * h_ref[...] + mix, g2_ref[...], b2_ref[...])
    o_ref[...] = _swiglu_ln(h2, wg_ref, wu_ref, wd_ref, g3_ref, b3_ref)


def _merge_ffn(h, hm, ha, gate, w_bm, w_ba, w_out, g2, b2, w_gate, w_up, w_down, g3, b3):
    m = h.shape[0]
    tm = min(ROW_TILE, m)
    row = lambda n: pl.BlockSpec((tm, n), lambda i: (i, 0))
    return pl.pallas_call(
        _merge_ffn_kernel,
        out_shape=jax.ShapeDtypeStruct((m, D_MODEL), F32),
        grid=(m // tm,),
        in_specs=[row(D_MODEL), row(512), row(512), row(2048),
                  _resident((512, D_MODEL)), _resident((512, D_MODEL)), _resident((D_MODEL, D_MODEL)),
                  _resident((1, D_MODEL)), _resident((1, D_MODEL)),
                  _resident((D_MODEL, D_FF)), _resident((D_MODEL, D_FF)), _resident((D_FF, D_MODEL)),
                  _resident((1, D_MODEL)), _resident((1, D_MODEL))],
        out_specs=row(D_MODEL),
        compiler_params=pltpu.CompilerParams(
            dimension_semantics=("arbitrary",), vmem_limit_bytes=VMEM_LIMIT),
    )(h, hm, ha, gate, w_bm, w_ba, w_out, g2, b2, w_gate, w_up, w_down, g3, b3)


def _ffn_weights(w_gate, w_up, w_down):
    return w_gate.astype(BF16), w_up.astype(BF16), w_down.astype(BF16)


def _proj_weights(w_in, b_if, b_gate):
    w = w_in.astype(BF16)
    w_all = jnp.concatenate(
        [w[:, :2048], jnp.pad(w[:, 2048:2056], ((0, 0), (0, 120))), w[:, 2056:]], axis=1)
    return w_all, jnp.pad(b_if, (0, 120)).reshape(1, 128), b_gate.reshape(1, 2 * D_MODEL)


def kernel(x, meta, ffn1_w_gate, ffn1_w_up, ffn1_w_down, ln1_g, ln1_b, w_in, conv_w, conv_b, b_if, m_norm_g, lam_q1, lam_k1, lam_q2, lam_k2, a_norm_g, w_bm, w_ba, b_gate, w_out, ln2_g, ln2_b, ffn2_w_gate, ffn2_w_up, ffn2_w_down, ln3_g, ln3_b):
    batch, seq, _ = x.shape
    rows = batch * seq
    hr = x.reshape(rows, D_MODEL)
    hm = jnp.pad(meta.astype(x.dtype), ((META_ROWS - N_META, 0), (0, 0)))

    slopes_log2 = jnp.array([LOG2E * 2.0 ** (-8.0 * (h + 1) / A_HEADS) for h in range(A_HEADS)], F32)
    slopes = jnp.repeat(slopes_log2, 128).reshape(1, 512)
    alibi_cols = _alibi_columns(slopes_log2, min(ATTN_BLOCK, seq))
    is_meta = (jnp.arange(META_ROWS) >= META_ROWS - N_META)[:, None]
    null_gate = jnp.where(jnp.arange(128) < M_HEADS, NEG, 0.0).astype(F32)[None, :]
    state0 = (jnp.zeros((1, M_HEADS, M_DK, 2 * M_DK), F32), jnp.full((1, 8, 128), NEG, F32),
              jnp.zeros((1, 8, 1024), F32))
    vec = lambda a: a.reshape(1, -1)

    for i in range(DEPTH):
        lam_init = 0.8 - 0.6 * math.exp(-0.3 * i)
        f1 = _ffn_weights(ffn1_w_gate[i], ffn1_w_up[i], ffn1_w_down[i])
        f2 = _ffn_weights(ffn2_w_gate[i], ffn2_w_up[i], ffn2_w_down[i])
        w_all, bif, bg = _proj_weights(w_in[i], b_if[i], b_gate[i])
        cw = jnp.pad(conv_w[i], ((0, 8 - CONV_W), (0, 0)))
        cb = vec(conv_b[i])
        lam_rows = jnp.pad(jnp.stack([lam_q1[i], lam_k1[i], lam_q2[i], lam_k2[i]]).astype(F32),
                           ((0, 4), (0, 128 - A_DK)))
        wbm, wba, wo = w_bm[i].astype(BF16), w_ba[i].astype(BF16), w_out[i].astype(BF16)

        hr = _ffn_ln(hr, *f1, vec(ln1_g[i]), vec(ln1_b[i]))
        hm = _ffn_ln(hm, *f1, vec(ln1_g[i]), vec(ln1_b[i]))

        qk_r, v_r, og_r, gt_r, aq_r, ak_r, av_r, gate_r = _proj(hr, w_all, bif, bg)
        qk_m, v_m, og_m, gt_m, aq_m, ak_m, av_m, gate_m = _proj(hm, w_all, bif, bg)
        qk_m = jnp.where(is_meta, qk_m, 0.0)
        gt_m = jnp.where(is_meta, gt_m, null_gate)

        xm_m, *state = _mlstm(qk_m, v_m, og_m, gt_m, cw, cb, vec(m_norm_g[i]), state0,
                              batch=1, chunk=META_ROWS)
        xm_r = _mlstm(qk_r, v_r, og_r, gt_r, cw, cb, vec(m_norm_g[i]), tuple(state),
                      batch=batch, chunk=min(MLSTM_CHUNK, seq))[0]

        xa_m = _attn_meta(aq_m, ak_m, av_m, lam_rows, slopes, vec(a_norm_g[i]), lam_init=lam_init)
        xa_r = _attn(aq_r, ak_r, av_r, ak_m, av_m, alibi_cols, lam_rows, slopes, vec(a_norm_g[i]),
                     batch=batch, lam_init=lam_init)

        tail_params = (wbm, wba, wo, vec(ln2_g[i]), vec(ln2_b[i]), *f2, vec(ln3_g[i]), vec(ln3_b[i]))
        hr = _merge_ffn(hr, xm_r, xa_r, gate_r, *tail_params)
        hm = _merge_ffn(hm, xm_m, xa_m, gate_m, *tail_params)

    return hr.reshape(batch, seq, D_MODEL)
```

```python
import functools
import math

import jax
import jax.numpy as jnp
import numpy as np
from jax import lax
from jax.experimental import pallas as pl
from jax.experimental.pallas import tpu as pltpu

F32 = jnp.float32
BF16 = jnp.bfloat16

D_MODEL = 1024
N_META = 16
META_ROWS = 128
M_HEADS = 4
M_DK = 128
M_WIDTH = 512
CONV_W = 4
A_HEADS = 4
A_DK = 64
A_DV = 128
A_WIDTH = 512
D_FF = 2816
FF_CHUNK = 256
N_FF_CHUNKS = D_FF // FF_CHUNK
DEPTH = 2
ALPHA = (2 * DEPTH) ** 0.25
LN_EPS = 1e-5
NEG = -1e30

SEG_QK = (0, 1024)
SEG_V = (1024, 1536)
SEG_O = (1536, 2048)
SEG_IF = (2048, 2176)
SEG_AQ = (2176, 2688)
SEG_AK = (2688, 3200)
SEG_AV = (3200, 3712)
SEG_G = (3712, 5760)
PROJ_COLS = 5760

ROW_TILE = 512
FFN_ROW_TILE = 1024
MLSTM_CHUNK = 256
MLSTM_GROUP = 8
ATTN_BLOCK = 512
ATTN_HEAD_GROUP = 4
LOG2E = math.log2(math.e)
VMEM_LIMIT = 56 * 1024 * 1024


def _nt_dot(a, b):
    return lax.dot_general(a, b, (((1,), (1,)), ((), ())), preferred_element_type=F32)


def _layer_norm(y, g, b):
    mu = jnp.mean(y, -1, keepdims=True)
    d = y - mu
    var = jnp.mean(d * d, -1, keepdims=True)
    return d * lax.rsqrt(var + LN_EPS) * g + b


def _swiglu_ln(x, wg_ref, wu_ref, wd_ref, g_ref, b_ref):
    xb = x.astype(BF16)
    acc = jnp.zeros(x.shape, F32)
    for j in range(N_FF_CHUNKS):
        cols = slice(j * FF_CHUNK, (j + 1) * FF_CHUNK)
        g = jnp.dot(xb, wg_ref[:, cols], preferred_element_type=F32)
        u = jnp.dot(xb, wu_ref[:, cols], preferred_element_type=F32)
        a = (g * jax.nn.sigmoid(g) * u).astype(BF16)
        acc = acc + jnp.dot(a, wd_ref[cols, :], preferred_element_type=F32)
    return _layer_norm(ALPHA * x + 0.5 * acc, g_ref[...], b_ref[...])


def _ffn_ln_kernel(x_ref, wg_ref, wu_ref, wd_ref, g_ref, b_ref, o_ref):
    half = x_ref.shape[0] // 2
    for rows in (slice(0, half), slice(half, 2 * half)):
        o_ref[rows, :] = _swiglu_ln(x_ref[rows, :], wg_ref, wu_ref, wd_ref, g_ref, b_ref)


def _resident(shape):
    return pl.BlockSpec(shape, lambda *_: (0,) * len(shape), pipeline_mode=pl.Buffered(1))


def _ffn_ln(x, w_gate, w_up, w_down, g, b):
    m = x.shape[0]
    tm = min(FFN_ROW_TILE, m)
    return pl.pallas_call(
        _ffn_ln_kernel,
        out_shape=jax.ShapeDtypeStruct((m, D_MODEL), F32),
        grid=(m // tm,),
        in_specs=[
            pl.BlockSpec((tm, D_MODEL), lambda i: (i, 0)),
            _resident((D_MODEL, D_FF)), _resident((D_MODEL, D_FF)), _resident((D_FF, D_MODEL)),
            _resident((1, D_MODEL)), _resident((1, D_MODEL)),
        ],
        out_specs=pl.BlockSpec((tm, D_MODEL), lambda i: (i, 0)),
        compiler_params=pltpu.CompilerParams(
            dimension_semantics=("arbitrary",), vmem_limit_bytes=VMEM_LIMIT),
    )(x, w_gate, w_up, w_down, g, b)


def _proj_kernel(h_ref, w_ref, bif_ref, bg_ref,
                 qk_ref, v_ref, og_ref, gt_ref, aq_ref, ak_ref, av_ref, gate_ref):
    hb = h_ref[...].astype(BF16)

    def seg(s):
        return jnp.dot(hb, w_ref[:, s[0]:s[1]], preferred_element_type=F32)

    gate_ref[...] = jax.nn.sigmoid(seg(SEG_G) + bg_ref[...]).astype(BF16)
    og_ref[...] = jax.nn.sigmoid(seg(SEG_O)).astype(BF16)
    z = seg(SEG_IF) + bif_ref[...]
    col = lax.broadcasted_iota(jnp.int32, z.shape, 1)
    log_f = jnp.minimum(z, 0.0) - jnp.log1p(jnp.exp(-jnp.abs(z)))
    gt_ref[...] = jnp.where(col < M_HEADS, z, jnp.where(col < 2 * M_HEADS, log_f, 0.0))
    aq_ref[...] = (seg(SEG_AQ) * (LOG2E * A_DK ** -0.5)).astype(BF16)
    qk_ref[...] = seg(SEG_QK)
    v_ref[...] = seg(SEG_V).astype(BF16)
    ak_ref[...] = seg(SEG_AK).astype(BF16)
    av_ref[...] = seg(SEG_AV).astype(BF16)


def _proj(h, w_all, b_if, b_gate):
    m = h.shape[0]
    tm = min(ROW_TILE, m)
    row = lambda n: pl.BlockSpec((tm, n), lambda i: (i, 0))
    const = lambda r, n: pl.BlockSpec((r, n), lambda i: (0, 0))
    out_shape = (
        jax.ShapeDtypeStruct((m, 1024), F32),
        jax.ShapeDtypeStruct((m, 512), BF16),
        jax.ShapeDtypeStruct((m, 512), BF16),
        jax.ShapeDtypeStruct((m, 128), F32),
        jax.ShapeDtypeStruct((m, 512), BF16),
        jax.ShapeDtypeStruct((m, 512), BF16),
        jax.ShapeDtypeStruct((m, 512), BF16),
        jax.ShapeDtypeStruct((m, 2048), BF16),
    )
    return pl.pallas_call(
        _proj_kernel,
        out_shape=out_shape,
        grid=(m // tm,),
        in_specs=[row(D_MODEL), const(D_MODEL, PROJ_COLS), const(1, 128), const(1, 2048)],
        out_specs=(row(1024), row(512), row(512), row(128), row(512), row(512), row(512), row(2048)),
        compiler_params=pltpu.CompilerParams(
            dimension_semantics=("arbitrary",), vmem_limit_bytes=VMEM_LIMIT),
    )(h, w_all, b_if, b_gate)


def _mlstm_kernel(qk_ref, v_ref, og_ref, gt_ref, cw_ref, cb_ref, ng_ref,
                  c0_ref, m0_ref, t0_ref,
                  h_ref, c_ref, m_ref, tail_ref, *, T, G):
    @pl.when(pl.program_id(1) == 0)
    def _():
        for g in range(G):
            c_ref[g] = c0_ref[0]
            m_ref[g] = m0_ref[0]
            tail_ref[g] = t0_ref[0]

    r = lax.broadcasted_iota(jnp.int32, (T, T), 0)
    c = lax.broadcasted_iota(jnp.int32, (T, T), 1)
    causal = c <= r
    eye = c == r
    tri = jnp.where(causal, 1.0, 0.0).astype(F32)
    ones = jnp.ones((T, M_DK), BF16)

    for g in range(G):
        u = jnp.concatenate([tail_ref[g], qk_ref[g]], axis=0)
        tail_ref[g] = u[T:T + 8, :]
        y = cb_ref[...] + cw_ref[CONV_W - 1:CONV_W, :] * u
        for j in range(1, CONV_W):
            y = y + cw_ref[CONV_W - 1 - j:CONV_W - j, :] * pltpu.roll(u, j, 0)
        y = y[8:8 + T, :]
        qk = y * jax.nn.sigmoid(y)

        gates = gt_ref[g]
        cums = jnp.dot(tri, gates, preferred_element_type=F32, precision=lax.Precision.HIGHEST)
        u_rows = (gates.T)[0:M_HEADS, :] - (cums.T)[M_HEADS:2 * M_HEADS, :]

        for h in range(M_HEADS):
            sl = slice(h * M_DK, (h + 1) * M_DK)
            qb = qk[:, sl].astype(BF16)
            k_h = qk[:, M_WIDTH + h * M_DK:M_WIDTH + (h + 1) * M_DK] * (M_DK ** -0.5)
            kb = k_h.astype(BF16)
            v_aug = jnp.concatenate([v_ref[g, :, sl], ones], axis=1)
            b_col = cums[:, M_HEADS + h:M_HEADS + h + 1]
            m_prev = m_ref[g, h:h + 1, 0:1]
            cn_prev = c_ref[g, h]

            um = jnp.where(causal, u_rows[h:h + 1, :], NEG)
            big_m = jnp.maximum(jnp.max(um, -1, keepdims=True), m_prev)
            p = jnp.exp(um - big_m)
            s = (_nt_dot(qb, kb) * p).astype(BF16)
            w_state = jnp.exp(m_prev - big_m)
            tot = (jnp.dot(s, v_aug, preferred_element_type=F32)
                   + w_state * jnp.dot(qb, cn_prev.astype(BF16), preferred_element_type=F32))
            num = tot[:, 0:M_DK]
            den = tot[:, M_DK:2 * M_DK]
            hh = num / jnp.maximum(jnp.abs(den), jnp.exp(-(b_col + big_m)))
            mu = jnp.mean(hh, -1, keepdims=True)
            d = hh - mu
            hn = d * lax.rsqrt(jnp.mean(d * d, -1, keepdims=True) + LN_EPS)
            h_ref[g, :, sl] = (hn * ng_ref[:, sl] * og_ref[g, :, sl].astype(F32)).astype(BF16)

            w_diag = jnp.where(eye, p[T - 1:T, :], 0.0).astype(BF16)
            vw = jnp.dot(w_diag, v_aug, preferred_element_type=F32)
            c_ref[g, h] = (w_state[T - 1:T, :] * cn_prev
                           + jnp.dot(k_h.T.astype(BF16), vw.astype(BF16), preferred_element_type=F32))
            m_ref[g, h:h + 1, :] = jnp.broadcast_to(b_col[T - 1:T, :] + big_m[T - 1:T, :], (1, 128))


def _mlstm(qk, v, og, gates, conv_w, conv_b, norm_g, state, *, batch, chunk):
    seq = qk.shape[0] // batch
    nc = seq // chunk
    group = min(MLSTM_GROUP, batch)
    c0, m0, t0 = state
    seq3 = lambda a: a.reshape(batch, seq, a.shape[-1])
    row = lambda n: pl.BlockSpec((group, chunk, n), lambda b, c: (b, c, 0))
    const2 = lambda r, n: pl.BlockSpec((r, n), lambda b, c: (0, 0))
    init3 = lambda r, n: pl.BlockSpec((1, r, n), lambda b, c: (0, 0, 0))
    out3 = lambda r, n: pl.BlockSpec((group, r, n), lambda b, c: (b, 0, 0))
    out_shape = (
        jax.ShapeDtypeStruct((batch, seq, M_WIDTH), BF16),
        jax.ShapeDtypeStruct((batch, M_HEADS, M_DK, 2 * M_DK), F32),
        jax.ShapeDtypeStruct((batch, 8, 128), F32),
        jax.ShapeDtypeStruct((batch, 8, 1024), F32),
    )
    out, c1, m1, t1 = pl.pallas_call(
        functools.partial(_mlstm_kernel, T=chunk, G=group),
        out_shape=out_shape,
        grid=(batch // group, nc),
        in_specs=[
            row(1024), row(512), row(512), row(128),
            const2(8, 1024), const2(1, 1024), const2(1, 512),
            pl.BlockSpec((1, M_HEADS, M_DK, 2 * M_DK), lambda b, c: (0, 0, 0, 0)),
            init3(8, 128), init3(8, 1024),
        ],
        out_specs=(
            row(512),
            pl.BlockSpec((group, M_HEADS, M_DK, 2 * M_DK), lambda b, c: (b, 0, 0, 0)),
            out3(8, 128), out3(8, 1024),
        ),
        compiler_params=pltpu.CompilerParams(
            dimension_semantics=("arbitrary", "arbitrary"), vmem_limit_bytes=VMEM_LIMIT),
    )(seq3(qk), seq3(v), seq3(og), seq3(gates), conv_w, conv_b, norm_g, c0, m0, t0)
    return out.reshape(batch * seq, M_WIDTH), c1, m1, t1


def _stack_maps(q):
    lane = lax.broadcasted_iota(jnp.int32, q.shape, 1)
    zero = jnp.zeros_like(q)
    return jnp.concatenate([jnp.where(lane < A_DK, q, zero), jnp.where(lane >= A_DK, q, zero)], axis=0)


def _diff_finish(acc, l, lam_ref, ng, lam_init, t):
    lam = (jnp.exp(jnp.sum(lam_ref[0:1, :] * lam_ref[1:2, :], -1, keepdims=True))
           - jnp.exp(jnp.sum(lam_ref[2:3, :] * lam_ref[3:4, :], -1, keepdims=True)) + lam_init)
    o = acc[:t] / l[:t] - lam * (acc[t:] / l[t:])
    o = o * lax.rsqrt(jnp.mean(o * o, -1, keepdims=True) + LN_EPS) * (1.0 - lam_init)
    return (o * ng).astype(BF16)


ACC_ROWS = A_DV + 16


def _fold_keys(s_t, offset, v_t, m_s, acc_s):
    keys, width = s_t.shape
    m_old = m_s[...]
    m_new = jnp.maximum(m_old, jnp.max(s_t, axis=0, keepdims=True) + offset)
    alpha = jnp.exp2(m_old - m_new)
    p3 = jnp.exp2(s_t.reshape(keys // 8, 8, width) - (m_new - offset)[None])
    p = p3.reshape(keys, width).astype(BF16)
    acc3 = acc_s[...].reshape(ACC_ROWS // 8, 8, width)
    acc_s[...] = ((acc3 * alpha[None]).reshape(acc_s.shape)
                  + jnp.dot(v_t, p, preferred_element_type=F32))
    m_s[...] = m_new


def _attn_kernel(q_ref, k_ref, v_ref, km_ref, vm_ref, ab_ref, lam_ref, sl_ref, ngt_ref, o_ref,
                 m_s, acc_s, *, T, HG, lam_init):
    i = pl.program_id(2)
    q0 = i * T
    R = 2 * T
    cols = [slice(h * 128, (h + 1) * 128) for h in range(HG)]
    slope = [sl_ref[:, h * 128:h * 128 + 1] for h in range(HG)]
    row = lax.broadcasted_iota(jnp.int32, (128, R), 0)
    ones_rows = jnp.where(row < 3, 1.0, 0.0).astype(BF16)
    qs_t = [jnp.concatenate([_stack_maps(q_ref[:, cols[h]]).astype(F32).T.astype(BF16), ones_rows], axis=0)
            for h in range(HG)]
    stats = [(m_s.at[h], acc_s.at[h]) for h in range(HG)]

    m_s[...] = jnp.full(m_s.shape, NEG, F32)
    acc_s[...] = jnp.zeros(acc_s.shape, F32)

    def rows(j):
        return pl.ds(pl.multiple_of(j * T, T), T)

    def values_t(v):
        v_t = v.astype(F32).T.astype(BF16)
        return jnp.concatenate([v_t, jnp.ones((ACC_ROWS - A_DV, v_t.shape[1]), BF16)], axis=0)

    zero_off = jnp.zeros((1, 1), F32)

    km_pos = lax.broadcasted_iota(jnp.int32, (META_ROWS, 128), 0)
    s_meta = []
    for h in range(HG):
        bias = jnp.where(km_pos >= META_ROWS - N_META,
                         slope[h] * (km_pos - (META_ROWS + q0)).astype(F32), NEG)
        s_meta.append(jnp.dot(km_ref[:, cols[h]], qs_t[h][0:128], preferred_element_type=F32)
                      + jnp.concatenate([bias] * (R // 128), axis=1))
    for h in range(HG):
        _fold_keys(s_meta[h], zero_off, values_t(vm_ref[:, cols[h]]), *stats[h])

    def logits(h, j):
        k_aug = jnp.concatenate([k_ref[rows(j), cols[h]], ab_ref[:, cols[h]]], axis=1)
        return jnp.dot(k_aug, qs_t[h], preferred_element_type=F32)

    def offset(h, j):
        return slope[h] * (j * T - q0).astype(F32)

    def fold_blocks(h, first, n):
        scores = [logits(h, first + k) for k in range(n)]
        for k in range(n):
            _fold_keys(scores[k], offset(h, first + k), values_t(v_ref[rows(first + k), cols[h]]), *stats[h])

    for h in range(HG):
        lax.fori_loop(0, i // 4, lambda t, carry, h=h: fold_blocks(h, 4 * t, 4) or carry, 0)

    @pl.when(i % 4 >= 2)
    def _():
        for h in range(HG):
            fold_blocks(h, i // 4 * 4, 2)

    @pl.when(i % 2 == 1)
    def _():
        scores = [logits(h, i - 1) for h in range(HG)]
        for h in range(HG):
            _fold_keys(scores[h], offset(h, i - 1), values_t(v_ref[rows(i - 1), cols[h]]), *stats[h])

    key = lax.broadcasted_iota(jnp.int32, (T, T), 0)
    qry = lax.broadcasted_iota(jnp.int32, (T, T), 1)
    keep = key <= qry
    keep2 = jnp.concatenate([keep, keep], axis=1)
    s_diag = [jnp.where(keep2, logits(h, i), NEG) for h in range(HG)]
    for h in range(HG):
        _fold_keys(s_diag[h], zero_off, values_t(v_ref[rows(i), cols[h]]), *stats[h])
    lam = (jnp.exp(jnp.sum(lam_ref[0:1, :] * lam_ref[1:2, :], -1, keepdims=True))
           - jnp.exp(jnp.sum(lam_ref[2:3, :] * lam_ref[3:4, :], -1, keepdims=True)) + lam_init)
    for h in range(HG):
        acc = acc_s[h, 0:A_DV, :]
        l = acc_s[h, A_DV:A_DV + 1, :]
        o_t = acc[:, 0:T] / l[:, 0:T] - lam * (acc[:, T:R] / l[:, T:R])
        o_t = o_t * lax.rsqrt(jnp.mean(o_t * o_t, axis=0, keepdims=True) + LN_EPS) * (1.0 - lam_init)
        o_t = jnp.concatenate([o_t[:, t * 128:(t + 1) * 128] * ngt_ref[cols[h], :] for t in range(T // 128)], axis=1)
        o_ref[:, cols[h]] = o_t.T.astype(BF16)


def _alibi_columns(slopes_log2, t):
    x = np.asarray(slopes_log2, np.float32)[None, :] * np.arange(t, dtype=np.float32)[:, None]
    hi = x.astype(BF16)
    mid = (x - hi.astype(np.float32)).astype(BF16)
    lo = (x - hi.astype(np.float32) - mid.astype(np.float32)).astype(BF16)
    cols = np.zeros((t, A_HEADS, 128), BF16)
    cols[:, :, 0], cols[:, :, 1], cols[:, :, 2] = hi, mid, lo
    return jnp.asarray(cols.reshape(t, A_HEADS * 128))


def _attn(aq, ak, av, ak_meta, av_meta, alibi_cols, lam_rows, slopes, norm_g, *, batch, lam_init):
    m = aq.shape[0]
    seq = m // batch
    t = alibi_cols.shape[0]
    nq = seq // t
    hg = ATTN_HEAD_GROUP
    w = hg * 128
    qspec = pl.BlockSpec((t, w), lambda b, h, i: (b * nq + i, h))
    kvspec = pl.BlockSpec((seq, w), lambda b, h, i: (b, h), pipeline_mode=pl.Buffered(1))
    mspec = pl.BlockSpec((META_ROWS, w), lambda b, h, i: (0, h))
    hrow = pl.BlockSpec((1, w), lambda b, h, i: (0, h))
    norm_g_t = jnp.broadcast_to(norm_g.reshape(A_WIDTH, 1), (A_WIDTH, 128))
    return pl.pallas_call(
        functools.partial(_attn_kernel, T=t, HG=hg, lam_init=lam_init),
        out_shape=jax.ShapeDtypeStruct((m, A_WIDTH), BF16),
        grid=(batch, A_HEADS // hg, nq),
        in_specs=[qspec, kvspec, kvspec, mspec, mspec,
                  pl.BlockSpec((t, w), lambda b, h, i: (0, h)),
                  pl.BlockSpec((8, 128), lambda b, h, i: (0, 0)), hrow,
                  pl.BlockSpec((w, 128), lambda b, h, i: (h, 0))],
        out_specs=qspec,
        scratch_shapes=[pltpu.VMEM((hg, 8, 2 * t), F32), pltpu.VMEM((hg, ACC_ROWS, 2 * t), F32)],
        compiler_params=pltpu.CompilerParams(
            dimension_semantics=("arbitrary", "arbitrary", "arbitrary"),
            vmem_limit_bytes=VMEM_LIMIT),
    )(aq, ak, av, ak_meta, av_meta, alibi_cols, lam_rows, slopes, norm_g_t)


def _attn_meta_kernel(q_ref, k_ref, v_ref, lam_ref, sl_ref, ng_ref, o_ref, *, lam_init):
    t = META_ROWS
    slope = sl_ref[:, 0:1]
    qs = _stack_maps(q_ref[...])
    r = lax.broadcasted_iota(jnp.int32, (t, t), 0)
    c = lax.broadcasted_iota(jnp.int32, (t, t), 1)
    bias = jnp.where((c >= t - N_META) & (c <= r), slope * (c - r).astype(F32), NEG)
    s = _nt_dot(qs, k_ref[...]) + jnp.concatenate([bias, bias], axis=0)
    p = jnp.exp2(s - jnp.max(s, -1, keepdims=True))
    l = jnp.sum(p, -1, keepdims=True)
    acc = jnp.dot(p.astype(BF16), v_ref[...], preferred_element_type=F32)
    o_ref[...] = _diff_finish(acc, l, lam_ref, ng_ref[...], lam_init, t)


def _attn_meta(aq, ak, av, lam_rows, slopes, norm_g, *, lam_init):
    blk = pl.BlockSpec((META_ROWS, 128), lambda h: (0, h))
    hrow = pl.BlockSpec((1, 128), lambda h: (0, h))
    return pl.pallas_call(
        functools.partial(_attn_meta_kernel, lam_init=lam_init),
        out_shape=jax.ShapeDtypeStruct((META_ROWS, A_WIDTH), BF16),
        grid=(A_HEADS,),
        in_specs=[blk, blk, blk, pl.BlockSpec((8, 128), lambda h: (0, 0)), hrow, hrow],
        out_specs=blk,
        compiler_params=pltpu.CompilerParams(dimension_semantics=("arbitrary",)),
    )(aq, ak, av, lam_rows, slopes, norm_g)


def _merge_ffn_kernel(h_ref, hm_ref, ha_ref, gate_ref, wbm_ref, wba_ref, wo_ref, g2_ref, b2_ref,
                      wg_ref, wu_ref, wd_ref, g3_ref, b3_ref, o_ref):
    ym = jnp.dot(hm_ref[...], wbm_ref[...], preferred_element_type=F32)
    ya = jnp.dot(ha_ref[...], wba_ref[...], preferred_element_type=F32)
    gm = gate_ref[:, 0:D_MODEL].astype(F32)
    ga = gate_ref[:, D_MODEL:2 * D_MODEL].astype(F32)
    mixed = (gm * ym + ga * ya).astype(BF16)
    mix = jnp.dot(mixed, wo_ref[...], preferred_element_type=F32)
    h2 = _layer_norm(ALPHA * h_ref[...] + mix, g2_ref[...], b2_ref[...])
    o_ref[...] = _swiglu_ln(h2, wg_ref, wu_ref, wd_ref, g3_ref, b3_ref)


def _merge_ffn(h, hm, ha, gate, w_bm, w_ba, w_out, g2, b2, w_gate, w_up, w_down, g3, b3):
    m = h.shape[0]
    tm = min(ROW_TILE, m)
    row = lambda n: pl.BlockSpec((tm, n), lambda i: (i, 0))
    return pl.pallas_call(
        _merge_ffn_kernel,
        out_shape=jax.ShapeDtypeStruct((m, D_MODEL), F32),
        grid=(m // tm,),
        in_specs=[row(D_MODEL), row(512), row(512), row(2048),
                  _resident((512, D_MODEL)), _resident((512, D_MODEL)), _resident((D_MODEL, D_MODEL)),
                  _resident((1, D_MODEL)), _resident((1, D_MODEL)),
                  _resident((D_MODEL, D_FF)), _resident((D_MODEL, D_FF)), _resident((D_FF, D_MODEL)),
                  _resident((1, D_MODEL)), _resident((1, D_MODEL))],
        out_specs=row(D_MODEL),
        compiler_params=pltpu.CompilerParams(
            dimension_semantics=("arbitrary",), vmem_limit_bytes=VMEM_LIMIT),
    )(h, hm, ha, gate, w_bm, w_ba, w_out, g2, b2, w_gate, w_up, w_down, g3, b3)


def _ffn_weights(w_gate, w_up, w_down):
    return w_gate.astype(BF16), w_up.astype(BF16), w_down.astype(BF16)


def _proj_weights(w_in, b_if, b_gate):
    w = w_in.astype(BF16)
    w_all = jnp.concatenate(
        [w[:, :2048], jnp.pad(w[:, 2048:2056], ((0, 0), (0, 120))), w[:, 2056:]], axis=1)
    return w_all, jnp.pad(b_if, (0, 120)).reshape(1, 128), b_gate.reshape(1, 2 * D_MODEL)


def kernel(x, meta, ffn1_w_gate, ffn1_w_up, ffn1_w_down, ln1_g, ln1_b, w_in, conv_w, conv_b, b_if, m_norm_g, lam_q1, lam_k1, lam_q2, lam_k2, a_norm_g, w_bm, w_ba, b_gate, w_out, ln2_g, ln2_b, ffn2_w_gate, ffn2_w_up, ffn2_w_down, ln3_g, ln3_b):
    batch, seq, _ = x.shape
    rows = batch * seq
    hr = x.reshape(rows, D_MODEL)
    hm = jnp.pad(meta.astype(x.dtype), ((META_ROWS - N_META, 0), (0, 0)))

    slopes_log2 = np.array([LOG2E * 2.0 ** (-8.0 * (h + 1) / A_HEADS) for h in range(A_HEADS)], np.float32)
    slopes = jnp.asarray(np.repeat(slopes_log2, 128).reshape(1, 512))
    alibi_cols = _alibi_columns(slopes_log2, min(ATTN_BLOCK, seq))
    is_meta = (jnp.arange(META_ROWS) >= META_ROWS - N_META)[:, None]
    null_gate = jnp.where(jnp.arange(128) < M_HEADS, NEG, 0.0).astype(F32)[None, :]
    state0 = (jnp.zeros((1, M_HEADS, M_DK, 2 * M_DK), F32), jnp.full((1, 8, 128), NEG, F32),
              jnp.zeros((1, 8, 1024), F32))
    vec = lambda a: a.reshape(1, -1)

    for i in range(DEPTH):
        lam_init = 0.8 - 0.6 * math.exp(-0.3 * i)
        f1 = _ffn_weights(ffn1_w_gate[i], ffn1_w_up[i], ffn1_w_down[i])
        f2 = _ffn_weights(ffn2_w_gate[i], ffn2_w_up[i], ffn2_w_down[i])
        w_all, bif, bg = _proj_weights(w_in[i], b_if[i], b_gate[i])
        cw = jnp.pad(conv_w[i], ((0, 8 - CONV_W), (0, 0)))
        cb = vec(conv_b[i])
        lam_rows = jnp.pad(jnp.stack([lam_q1[i], lam_k1[i], lam_q2[i], lam_k2[i]]).astype(F32),
                           ((0, 4), (0, 128 - A_DK)))
        wbm, wba, wo = w_bm[i].astype(BF16), w_ba[i].astype(BF16), w_out[i].astype(BF16)

        hr = _ffn_ln(hr, *f1, vec(ln1_g[i]), vec(ln1_b[i]))
        hm = _ffn_ln(hm, *f1, vec(ln1_g[i]), vec(ln1_b[i]))

        qk_r, v_r, og_r, gt_r, aq_r, ak_r, av_r, gate_r = _proj(hr, w_all, bif, bg)
        qk_m, v_m, og_m, gt_m, aq_m, ak_m, av_m, gate_m = _proj(hm, w_all, bif, bg)
        qk_m = jnp.where(is_meta, qk_m, 0.0)
        gt_m = jnp.where(is_meta, gt_m, null_gate)

        xm_m, *state = _mlstm(qk_m, v_m, og_m, gt_m, cw, cb, vec(m_norm_g[i]), state0,
                              batch=1, chunk=META_ROWS)
        xm_r = _mlstm(qk_r, v_r, og_r, gt_r, cw, cb, vec(m_norm_g[i]), tuple(state),
                      batch=batch, chunk=min(MLSTM_CHUNK, seq))[0]

        xa_m = _attn_meta(aq_m, ak_m, av_m, lam_rows, slopes, vec(a_norm_g[i]), lam_init=lam_init)
        xa_r = _attn(aq_r, ak_r, av_r, ak_m, av_m, alibi_cols, lam_rows, slopes, vec(a_norm_g[i]),
                     batch=batch, lam_init=lam_init)

        tail_params = (wbm, wba, wo, vec(ln2_g[i]), vec(ln2_b[i]), *f2, vec(ln3_g[i]), vec(ln3_b[i]))
        hr = _merge_ffn(hr, xm_r, xa_r, gate_r, *tail_params)
        hm = _merge_ffn(hm, xm_m, xa_m, gate_m, *tail_params)

    return hr.reshape(batch, seq, D_MODEL)
```

```python
import functools
import math

import jax
import jax.numpy as jnp
import numpy as np
from jax import lax
from jax.experimental import pallas as pl
from jax.experimental.pallas import tpu as pltpu

F32 = jnp.float32
BF16 = jnp.bfloat16

D_MODEL = 1024
N_META = 16
META_ROWS = 128
M_HEADS = 4
M_DK = 128
M_WIDTH = 512
CONV_W = 4
A_HEADS = 4
A_DK = 64
A_DV = 128
A_WIDTH = 512
D_FF = 2816
FF_CHUNK = 256
N_FF_CHUNKS = D_FF // FF_CHUNK
DEPTH = 2
ALPHA = (2 * DEPTH) ** 0.25
LN_EPS = 1e-5
NEG = -1e30

SEG_QK = (0, 1024)
SEG_V = (1024, 1536)
SEG_O = (1536, 2048)
SEG_IF = (2048, 2176)
SEG_AQ = (2176, 2688)
SEG_AK = (2688, 3200)
SEG_AV = (3200, 3712)
SEG_G = (3712, 5760)
PROJ_COLS = 5760

ROW_TILE = 512
FFN_ROW_TILE = 1024
MLSTM_CHUNK = 256
MLSTM_GROUP = 8
ATTN_BLOCK = 512
ATTN_HEAD_GROUP = 4
LOG2E = math.log2(math.e)
VMEM_LIMIT = 56 * 1024 * 1024


def _nt_dot(a, b):
    return lax.dot_general(a, b, (((1,), (1,)), ((), ())), preferred_element_type=F32)


def _layer_norm(y, g, b):
    mu = jnp.mean(y, -1, keepdims=True)
    d = y - mu
    var = jnp.mean(d * d, -1, keepdims=True)
    return d * lax.rsqrt(var + LN_EPS) * g + b


def _swiglu_ln(x, wg_ref, wu_ref, wd_ref, g_ref, b_ref):
    xb = x.astype(BF16)
    acc = jnp.zeros(x.shape, F32)
    for j in range(N_FF_CHUNKS):
        cols = slice(j * FF_CHUNK, (j + 1) * FF_CHUNK)
        g = jnp.dot(xb, wg_ref[:, cols], preferred_element_type=F32)
        u = jnp.dot(xb, wu_ref[:, cols], preferred_element_type=F32)
        a = (g * jax.nn.sigmoid(g) * u).astype(BF16)
        acc = acc + jnp.dot(a, wd_ref[cols, :], preferred_element_type=F32)
    return _layer_norm(ALPHA * x + 0.5 * acc, g_ref[...], b_ref[...])


def _ffn_ln_kernel(x_ref, wg_ref, wu_ref, wd_ref, g_ref, b_ref, o_ref):
    half = x_ref.shape[0] // 2
    for rows in (slice(0, half), slice(half, 2 * half)):
        o_ref[rows, :] = _swiglu_ln(x_ref[rows, :], wg_ref, wu_ref, wd_ref, g_ref, b_ref)


def _resident(shape):
    return pl.BlockSpec(shape, lambda *_: (0,) * len(shape), pipeline_mode=pl.Buffered(1))


def _ffn_ln(x, w_gate, w_up, w_down, g, b):
    m = x.shape[0]
    tm = min(FFN_ROW_TILE, m)
    return pl.pallas_call(
        _ffn_ln_kernel,
        out_shape=jax.ShapeDtypeStruct((m, D_MODEL), F32),
        grid=(m // tm,),
        in_specs=[
            pl.BlockSpec((tm, D_MODEL), lambda i: (i, 0)),
            _resident((D_MODEL, D_FF)), _resident((D_MODEL, D_FF)), _resident((D_FF, D_MODEL)),
            _resident((1, D_MODEL)), _resident((1, D_MODEL)),
        ],
        out_specs=pl.BlockSpec((tm, D_MODEL), lambda i: (i, 0)),
        compiler_params=pltpu.CompilerParams(
            dimension_semantics=("arbitrary",), vmem_limit_bytes=VMEM_LIMIT),
    )(x, w_gate, w_up, w_down, g, b)


def _proj_kernel(h_ref, w_ref, bif_ref, bg_ref,
                 qk_ref, v_ref, og_ref, gt_ref, aq_ref, ak_ref, av_ref, gate_ref):
    hb = h_ref[...].astype(BF16)

    def seg(s):
        return jnp.dot(hb, w_ref[:, s[0]:s[1]], preferred_element_type=F32)

    gate_ref[...] = jax.nn.sigmoid(seg(SEG_G) + bg_ref[...]).astype(BF16)
    og_ref[...] = jax.nn.sigmoid(seg(SEG_O)).astype(BF16)
    z = seg(SEG_IF) + bif_ref[...]
    col = lax.broadcasted_iota(jnp.int32, z.shape, 1)
    log_f = jnp.minimum(z, 0.0) - jnp.log1p(jnp.exp(-jnp.abs(z)))
    gt_ref[...] = jnp.where(col < M_HEADS, z, jnp.where(col < 2 * M_HEADS, log_f, 0.0))
    aq_ref[...] = (seg(SEG_AQ) * (LOG2E * A_DK ** -0.5)).astype(BF16)
    qk_ref[...] = seg(SEG_QK)
    v_ref[...] = seg(SEG_V).astype(BF16)
    ak_ref[...] = seg(SEG_AK).astype(BF16)
    av_ref[...] = seg(SEG_AV).astype(BF16)


def _proj(h, w_all, b_if, b_gate):
    m = h.shape[0]
    tm = min(ROW_TILE, m)
    row = lambda n: pl.BlockSpec((tm, n), lambda i: (i, 0))
    const = lambda r, n: pl.BlockSpec((r, n), lambda i: (0, 0))
    out_shape = (
        jax.ShapeDtypeStruct((m, 1024), F32),
        jax.ShapeDtypeStruct((m, 512), BF16),
        jax.ShapeDtypeStruct((m, 512), BF16),
        jax.ShapeDtypeStruct((m, 128), F32),
        jax.ShapeDtypeStruct((m, 512), BF16),
        jax.ShapeDtypeStruct((m, 512), BF16),
        jax.ShapeDtypeStruct((m, 512), BF16),
        jax.ShapeDtypeStruct((m, 2048), BF16),
    )
    return pl.pallas_call(
        _proj_kernel,
        out_shape=out_shape,
        grid=(m // tm,),
        in_specs=[row(D_MODEL), const(D_MODEL, PROJ_COLS), const(1, 128), const(1, 2048)],
        out_specs=(row(1024), row(512), row(512), row(128), row(512), row(512), row(512), row(2048)),
        compiler_params=pltpu.CompilerParams(
            dimension_semantics=("arbitrary",), vmem_limit_bytes=VMEM_LIMIT),
    )(h, w_all, b_if, b_gate)


def _mlstm_kernel(qk_ref, v_ref, og_ref, gt_ref, cw_ref, cb_ref, ng_ref,
                  c0_ref, m0_ref, t0_ref,
                  h_ref, c_ref, m_ref, tail_ref, *, T, G):
    @pl.when(pl.program_id(1) == 0)
    def _():
        for g in range(G):
            c_ref[g] = c0_ref[0]
            m_ref[g] = m0_ref[0]
            tail_ref[g] = t0_ref[0]

    r = lax.broadcasted_iota(jnp.int32, (T, T), 0)
    c = lax.broadcasted_iota(jnp.int32, (T, T), 1)
    causal = c <= r
    eye = c == r
    tri = jnp.where(causal, 1.0, 0.0).astype(F32)
    ones = jnp.ones((T, M_DK), BF16)

    for g in range(G):
        u = jnp.concatenate([tail_ref[g], qk_ref[g]], axis=0)
        tail_ref[g] = u[T:T + 8, :]
        y = cb_ref[...] + cw_ref[CONV_W - 1:CONV_W, :] * u
        for j in range(1, CONV_W):
            y = y + cw_ref[CONV_W - 1 - j:CONV_W - j, :] * pltpu.roll(u, j, 0)
        y = y[8:8 + T, :]
        qk = y * jax.nn.sigmoid(y)

        gates = gt_ref[g]
        cums = jnp.dot(tri, gates, preferred_element_type=F32, precision=lax.Precision.HIGHEST)
        u_rows = (gates.T)[0:M_HEADS, :] - (cums.T)[M_HEADS:2 * M_HEADS, :]

        for h in range(M_HEADS):
            sl = slice(h * M_DK, (h + 1) * M_DK)
            qb = qk[:, sl].astype(BF16)
            k_h = qk[:, M_WIDTH + h * M_DK:M_WIDTH + (h + 1) * M_DK] * (M_DK ** -0.5)
            kb = k_h.astype(BF16)
            v_aug = jnp.concatenate([v_ref[g, :, sl], ones], axis=1)
            b_col = cums[:, M_HEADS + h:M_HEADS + h + 1]
            m_prev = m_ref[g, h:h + 1, 0:1]
            cn_prev = c_ref[g, h]

            um = jnp.where(causal, u_rows[h:h + 1, :], NEG)
            big_m = jnp.maximum(jnp.max(um, -1, keepdims=True), m_prev)
            p = jnp.exp(um - big_m)
            s = (_nt_dot(qb, kb) * p).astype(BF16)
            w_state = jnp.exp(m_prev - big_m)
            tot = (jnp.dot(s, v_aug, preferred_element_type=F32)
                   + w_state * jnp.dot(qb, cn_prev.astype(BF16), preferred_element_type=F32))
            num = tot[:, 0:M_DK]
            den = tot[:, M_DK:2 * M_DK]
            hh = num / jnp.maximum(jnp.abs(den), jnp.exp(-(b_col + big_m)))
            mu = jnp.mean(hh, -1, keepdims=True)
            d = hh - mu
            hn = d * lax.rsqrt(jnp.mean(d * d, -1, keepdims=True) + LN_EPS)
            h_ref[g, :, sl] = (hn * ng_ref[:, sl] * og_ref[g, :, sl].astype(F32)).astype(BF16)

            w_diag = jnp.where(eye, p[T - 1:T, :], 0.0).astype(BF16)
            vw = jnp.dot(w_diag, v_aug, preferred_element_type=F32)
            c_ref[g, h] = (w_state[T - 1:T, :] * cn_prev
                           + jnp.dot(k_h.T.astype(BF16), vw.astype(BF16), preferred_element_type=F32))
            m_ref[g, h:h + 1, :] = jnp.broadcast_to(b_col[T - 1:T, :] + big_m[T - 1:T, :], (1, 128))


def _mlstm(qk, v, og, gates, conv_w, conv_b, norm_g, state, *, batch, chunk):
    seq = qk.shape[0] // batch
    nc = seq // chunk
    group = min(MLSTM_GROUP, batch)
    c0, m0, t0 = state
    seq3 = lambda a: a.reshape(batch, seq, a.shape[-1])
    row = lambda n: pl.BlockSpec((group, chunk, n), lambda b, c: (b, c, 0))
    const2 = lambda r, n: pl.BlockSpec((r, n), lambda b, c: (0, 0))
    init3 = lambda r, n: pl.BlockSpec((1, r, n), lambda b, c: (0, 0, 0))
    out3 = lambda r, n: pl.BlockSpec((group, r, n), lambda b, c: (b, 0, 0))
    out_shape = (
        jax.ShapeDtypeStruct((batch, seq, M_WIDTH), BF16),
        jax.ShapeDtypeStruct((batch, M_HEADS, M_DK, 2 * M_DK), F32),
        jax.ShapeDtypeStruct((batch, 8, 128), F32),
        jax.ShapeDtypeStruct((batch, 8, 1024), F32),
    )
    out, c1, m1, t1 = pl.pallas_call(
        functools.partial(_mlstm_kernel, T=chunk, G=group),
        out_shape=out_shape,
        grid=(batch // group, nc),
        in_specs=[
            row(1024), row(512), row(512), row(128),
            const2(8, 1024), const2(1, 1024), const2(1, 512),
            pl.BlockSpec((1, M_HEADS, M_DK, 2 * M_DK), lambda b, c: (0, 0, 0, 0)),
            init3(8, 128), init3(8, 1024),
        ],
        out_specs=(
            row(512),
            pl.BlockSpec((group, M_HEADS, M_DK, 2 * M_DK), lambda b, c: (b, 0, 0, 0)),
            out3(8, 128), out3(8, 1024),
        ),
        compiler_params=pltpu.CompilerParams(
            dimension_semantics=("arbitrary", "arbitrary"), vmem_limit_bytes=VMEM_LIMIT),
    )(seq3(qk), seq3(v), seq3(og), seq3(gates), conv_w, conv_b, norm_g, c0, m0, t0)
    return out.reshape(batch * seq, M_WIDTH), c1, m1, t1


def _stack_maps(q):
    lane = lax.broadcasted_iota(jnp.int32, q.shape, 1)
    zero = jnp.zeros_like(q)
    return jnp.concatenate([jnp.where(lane < A_DK, q, zero), jnp.where(lane >= A_DK, q, zero)], axis=0)


def _diff_finish(acc, l, lam_ref, ng, lam_init, t):
    lam = (jnp.exp(jnp.sum(lam_ref[0:1, :] * lam_ref[1:2, :], -1, keepdims=True))
           - jnp.exp(jnp.sum(lam_ref[2:3, :] * lam_ref[3:4, :], -1, keepdims=True)) + lam_init)
    o = acc[:t] / l[:t] - lam * (acc[t:] / l[t:])
    o = o * lax.rsqrt(jnp.mean(o * o, -1, keepdims=True) + LN_EPS) * (1.0 - lam_init)
    return (o * ng).astype(BF16)


ACC_ROWS = A_DV + 16


def _fold_keys(s_t, offset, v_t, m_s, acc_s):
    keys, width = s_t.shape
    m_old = m_s[...]
    m_new = jnp.maximum(m_old, jnp.max(s_t, axis=0, keepdims=True) + offset)
    alpha = jnp.exp2(m_old - m_new)
    p3 = jnp.exp2(s_t.reshape(keys // 8, 8, width) - (m_new - offset)[None])
    p = p3.reshape(keys, width).astype(BF16)
    acc3 = acc_s[...].reshape(ACC_ROWS // 8, 8, width)
    acc_s[...] = ((acc3 * alpha[None]).reshape(acc_s.shape)
                  + jnp.dot(v_t, p, preferred_element_type=F32))
    m_s[...] = m_new


def _attn_kernel(q_ref, k_ref, v_ref, km_ref, vm_ref, ab_ref, lam_ref, sl_ref, ngt_ref, o_ref,
                 m_s, acc_s, *, T, HG, lam_init):
    i = pl.program_id(2)
    q0 = i * T
    R = 2 * T
    cols = [slice(h * 128, (h + 1) * 128) for h in range(HG)]
    slope = [sl_ref[:, h * 128:h * 128 + 1] for h in range(HG)]
    row = lax.broadcasted_iota(jnp.int32, (128, R), 0)
    ones_rows = jnp.where(row < 3, 1.0, 0.0).astype(BF16)
    qs_t = [jnp.concatenate([_stack_maps(q_ref[:, cols[h]]).astype(F32).T.astype(BF16), ones_rows], axis=0)
            for h in range(HG)]
    stats = [(m_s.at[h], acc_s.at[h]) for h in range(HG)]

    m_s[...] = jnp.full(m_s.shape, NEG, F32)
    acc_s[...] = jnp.zeros(acc_s.shape, F32)

    def rows(j):
        return pl.ds(pl.multiple_of(j * T, T), T)

    def with_ones(v_t):
        return jnp.concatenate([v_t, jnp.ones((ACC_ROWS - A_DV, v_t.shape[1]), BF16)], axis=0)

    zero_off = jnp.zeros((1, 1), F32)

    km_pos = lax.broadcasted_iota(jnp.int32, (META_ROWS, 128), 0)
    s_meta = []
    for h in range(HG):
        bias = jnp.where(km_pos >= META_ROWS - N_META,
                         slope[h] * (km_pos - (META_ROWS + q0)).astype(F32), NEG)
        s_meta.append(jnp.dot(km_ref[:, cols[h]], qs_t[h][0:128], preferred_element_type=F32)
                      + jnp.concatenate([bias] * (R // 128), axis=1))
    for h in range(HG):
        _fold_keys(s_meta[h], zero_off, with_ones(vm_ref[h]), *stats[h])

    def logits(h, j):
        k_aug = jnp.concatenate([k_ref[rows(j), cols[h]], ab_ref[:, cols[h]]], axis=1)
        return jnp.dot(k_aug, qs_t[h], preferred_element_type=F32)

    def offset(h, j):
        return slope[h] * (j * T - q0).astype(F32)

    def fold_blocks(h, first, n):
        scores = [logits(h, first + k) for k in range(n)]
        for k in range(n):
            _fold_keys(scores[k], offset(h, first + k), with_ones(v_ref[0, h, first + k]), *stats[h])

    def fold_pair_of_heads(h0, first, n):
        scores = [[logits(h, first + k) for k in range(n)] for h in (h0, h0 + 1)]
        for k in range(n):
            for d in (0, 1):
                _fold_keys(scores[d][k], offset(h0 + d, first + k),
                           with_ones(v_ref[0, h0 + d, first + k]), *stats[h0 + d])

    for h0 in range(0, HG, 2):
        lax.fori_loop(0, i // 4, lambda t, carry, h0=h0: fold_pair_of_heads(h0, 4 * t, 4) or carry, 0)

    @pl.when(i % 4 >= 2)
    def _():
        for h in range(HG):
            fold_blocks(h, i // 4 * 4, 2)

    @pl.when(i % 2 == 1)
    def _():
        scores = [logits(h, i - 1) for h in range(HG)]
        for h in range(HG):
            _fold_keys(scores[h], offset(h, i - 1), with_ones(v_ref[0, h, i - 1]), *stats[h])

    key = lax.broadcasted_iota(jnp.int32, (T, T), 0)
    qry = lax.broadcasted_iota(jnp.int32, (T, T), 1)
    keep = key <= qry
    keep2 = jnp.concatenate([keep, keep], axis=1)
    s_diag = [jnp.where(keep2, logits(h, i), NEG) for h in range(HG)]
    for h in range(HG):
        _fold_keys(s_diag[h], zero_off, with_ones(v_ref[0, h, i]), *stats[h])
    lam = (jnp.exp(jnp.sum(lam_ref[0:1, :] * lam_ref[1:2, :], -1, keepdims=True))
           - jnp.exp(jnp.sum(lam_ref[2:3, :] * lam_ref[3:4, :], -1, keepdims=True)) + lam_init)
    for h in range(HG):
        acc = acc_s[h, 0:A_DV, :]
        l = acc_s[h, A_DV:A_DV + 1, :]
        o_t = acc[:, 0:T] / l[:, 0:T] - lam * (acc[:, T:R] / l[:, T:R])
        o_t = o_t * lax.rsqrt(jnp.mean(o_t * o_t, axis=0, keepdims=True) + LN_EPS) * (1.0 - lam_init)
        o_t = jnp.concatenate([o_t[:, t * 128:(t + 1) * 128] * ngt_ref[cols[h], :] for t in range(T // 128)], axis=1)
        o_ref[:, cols[h]] = o_t.T.astype(BF16)


def _alibi_columns(slopes_log2, t):
    x = np.asarray(slopes_log2, np.float32)[None, :] * np.arange(t, dtype=np.float32)[:, None]
    hi = x.astype(BF16)
    mid = (x - hi.astype(np.float32)).astype(BF16)
    lo = (x - hi.astype(np.float32) - mid.astype(np.float32)).astype(BF16)
    cols = np.zeros((t, A_HEADS, 128), BF16)
    cols[:, :, 0], cols[:, :, 1], cols[:, :, 2] = hi, mid, lo
    return jnp.asarray(cols.reshape(t, A_HEADS * 128))


def _attn(aq, ak, av, ak_meta, av_meta, alibi_cols, lam_rows, slopes, norm_g, *, batch, lam_init):
    m = aq.shape[0]
    seq = m // batch
    t = alibi_cols.shape[0]
    nq = seq // t
    hg = ATTN_HEAD_GROUP
    w = hg * 128
    qspec = pl.BlockSpec((t, w), lambda b, h, i: (b * nq + i, h))
    kvspec = pl.BlockSpec((seq, w), lambda b, h, i: (b, h), pipeline_mode=pl.Buffered(1))
    mspec = pl.BlockSpec((META_ROWS, w), lambda b, h, i: (0, h))
    hrow = pl.BlockSpec((1, w), lambda b, h, i: (0, h))
    norm_g_t = jnp.broadcast_to(norm_g.reshape(A_WIDTH, 1), (A_WIDTH, 128))
    av_t = av.reshape(batch, nq, t, A_HEADS, A_DV).transpose(0, 3, 1, 4, 2)
    av_meta_t = av_meta.reshape(META_ROWS, A_HEADS, A_DV).transpose(1, 2, 0)
    vspec = pl.BlockSpec((1, hg, nq, A_DV, t), lambda b, h, i: (b, h, 0, 0, 0), pipeline_mode=pl.Buffered(1))
    vmspec = pl.BlockSpec((hg, A_DV, META_ROWS), lambda b, h, i: (h, 0, 0))
    return pl.pallas_call(
        functools.partial(_attn_kernel, T=t, HG=hg, lam_init=lam_init),
        out_shape=jax.ShapeDtypeStruct((m, A_WIDTH), BF16),
        grid=(batch, A_HEADS // hg, nq),
        in_specs=[qspec, kvspec, vspec, mspec, vmspec,
                  pl.BlockSpec((t, w), lambda b, h, i: (0, h)),
                  pl.BlockSpec((8, 128), lambda b, h, i: (0, 0)), hrow,
                  pl.BlockSpec((w, 128), lambda b, h, i: (h, 0))],
        out_specs=qspec,
        scratch_shapes=[pltpu.VMEM((hg, 8, 2 * t), F32), pltpu.VMEM((hg, ACC_ROWS, 2 * t), F32)],
        compiler_params=pltpu.CompilerParams(
            dimension_semantics=("arbitrary", "arbitrary", "arbitrary"),
            vmem_limit_bytes=VMEM_LIMIT),
    )(aq, ak, av_t, ak_meta, av_meta_t, alibi_cols, lam_rows, slopes, norm_g_t)


def _attn_meta_kernel(q_ref, k_ref, v_ref, lam_ref, sl_ref, ng_ref, o_ref, *, lam_init):
    t = META_ROWS
    slope = sl_ref[:, 0:1]
    qs = _stack_maps(q_ref[...])
    r = lax.broadcasted_iota(jnp.int32, (t, t), 0)
    c = lax.broadcasted_iota(jnp.int32, (t, t), 1)
    bias = jnp.where((c >= t - N_META) & (c <= r), slope * (c - r).astype(F32), NEG)
    s = _nt_dot(qs, k_ref[...]) + jnp.concatenate([bias, bias], axis=0)
    p = jnp.exp2(s - jnp.max(s, -1, keepdims=True))
    l = jnp.sum(p, -1, keepdims=True)
    acc = jnp.dot(p.astype(BF16), v_ref[...], preferred_element_type=F32)
    o_ref[...] = _diff_finish(acc, l, lam_ref, ng_ref[...], lam_init, t)


def _attn_meta(aq, ak, av, lam_rows, slopes, norm_g, *, lam_init):
    blk = pl.BlockSpec((META_ROWS, 128), lambda h: (0, h))
    hrow = pl.BlockSpec((1, 128), lambda h: (0, h))
    return pl.pallas_call(
        functools.partial(_attn_meta_kernel, lam_init=lam_init),
        out_shape=jax.ShapeDtypeStruct((META_ROWS, A_WIDTH), BF16),
        grid=(A_HEADS,),
        in_specs=[blk, blk, blk, pl.BlockSpec((8, 128), lambda h: (0, 0)), hrow, hrow],
        out_specs=blk,
        compiler_params=pltpu.CompilerParams(dimension_semantics=("arbitrary",)),
    )(aq, ak, av, lam_rows, slopes, norm_g)


def _merge_ffn_kernel(h_ref, hm_ref, ha_ref, gate_ref, wbm_ref, wba_ref, wo_ref, g2_ref, b2_ref,
                      wg_ref, wu_ref, wd_ref, g3_ref, b3_ref, o_ref):
    ym = jnp.dot(hm_ref[...], wbm_ref[...], preferred_element_type=F32)
    ya = jnp.dot(ha_ref[...], wba_ref[...], preferred_element_type=F32)
    gm = gate_ref[:, 0:D_MODEL].astype(F32)
    ga = gate_ref[:, D_MODEL:2 * D_MODEL].astype(F32)
    mixed = (gm * ym + ga * ya).astype(BF16)
    mix = jnp.dot(mixed, wo_ref[...], preferred_element_type=F32)
    h2 = _layer_norm(ALPHA * h_ref[...] + mix, g2_ref[...], b2_ref[...])
    o_ref[...] = _swiglu_ln(h2, wg_ref, wu_ref, wd_ref, g3_ref, b3_ref)


def _merge_ffn(h, hm, ha, gate, w_bm, w_ba, w_out, g2, b2, w_gate, w_up, w_down, g3, b3):
    m = h.shape[0]
    tm = min(ROW_TILE, m)
    row = lambda n: pl.BlockSpec((tm, n), lambda i: (i, 0))
    return pl.pallas_call(
        _merge_ffn_kernel,
        out_shape=jax.ShapeDtypeStruct((m, D_MODEL), F32),
        grid=(m // tm,),
        in_specs=[row(D_MODEL), row(512), row(512), row(2048),
                  _resident((512, D_MODEL)), _resident((512, D_MODEL)), _resident((D_MODEL, D_MODEL)),
                  _resident((1, D_MODEL)), _resident((1, D_MODEL)),
                  _resident((D_MODEL, D_FF)), _resident((D_MODEL, D_FF)), _resident((D_FF, D_MODEL)),
                  _resident((1, D_MODEL)), _resident((1, D_MODEL))],
        out_specs=row(D_MODEL),
        compiler_params=pltpu.CompilerParams(
            dimension_semantics=("arbitrary",), vmem_limit_bytes=VMEM_LIMIT),
    )(h, hm, ha, gate, w_bm, w_ba, w_out, g2, b2, w_gate, w_up, w_down, g3, b3)


def _ffn_weights(w_gate, w_up, w_down):
    return w_gate.astype(BF16), w_up.astype(BF16), w_down.astype(BF16)


def _proj_weights(w_in, b_if, b_gate):
    w = w_in.astype(BF16)
    w_all = jnp.concatenate(
        [w[:, :2048], jnp.pad(w[:, 2048:2056], ((0, 0), (0, 120))), w[:, 2056:]], axis=1)
    return w_all, jnp.pad(b_if, (0, 120)).reshape(1, 128), b_gate.reshape(1, 2 * D_MODEL)


def kernel(x, meta, ffn1_w_gate, ffn1_w_up, ffn1_w_down, ln1_g, ln1_b, w_in, conv_w, conv_b, b_if, m_norm_g, lam_q1, lam_k1, lam_q2, lam_k2, a_norm_g, w_bm, w_ba, b_gate, w_out, ln2_g, ln2_b, ffn2_w_gate, ffn2_w_up, ffn2_w_down, ln3_g, ln3_b):
    batch, seq, _ = x.shape
    rows = batch * seq
    hr = x.reshape(rows, D_MODEL)
    hm = jnp.pad(meta.astype(x.dtype), ((META_ROWS - N_META, 0), (0, 0)))

    slopes_log2 = np.array([LOG2E * 2.0 ** (-8.0 * (h + 1) / A_HEADS) for h in range(A_HEADS)], np.float32)
    slopes = jnp.asarray(np.repeat(slopes_log2, 128).reshape(1, 512))
    alibi_cols = _alibi_columns(slopes_log2, min(ATTN_BLOCK, seq))
    is_meta = (jnp.arange(META_ROWS) >= META_ROWS - N_META)[:, None]
    null_gate = jnp.where(jnp.arange(128) < M_HEADS, NEG, 0.0).astype(F32)[None, :]
    state0 = (jnp.zeros((1, M_HEADS, M_DK, 2 * M_DK), F32), jnp.full((1, 8, 128), NEG, F32),
              jnp.zeros((1, 8, 1024), F32))
    vec = lambda a: a.reshape(1, -1)

    for i in range(DEPTH):
        lam_init = 0.8 - 0.6 * math.exp(-0.3 * i)
        f1 = _ffn_weights(ffn1_w_gate[i], ffn1_w_up[i], ffn1_w_down[i])
        f2 = _ffn_weights(ffn2_w_gate[i], ffn2_w_up[i], ffn2_w_down[i])
        w_all, bif, bg = _proj_weights(w_in[i], b_if[i], b_gate[i])
        cw = jnp.pad(conv_w[i], ((0, 8 - CONV_W), (0, 0)))
        cb = vec(conv_b[i])
        lam_rows = jnp.pad(jnp.stack([lam_q1[i], lam_k1[i], lam_q2[i], lam_k2[i]]).astype(F32),
                           ((0, 4), (0, 128 - A_DK)))
        wbm, wba, wo = w_bm[i].astype(BF16), w_ba[i].astype(BF16), w_out[i].astype(BF16)

        hr = _ffn_ln(hr, *f1, vec(ln1_g[i]), vec(ln1_b[i]))
        hm = _ffn_ln(hm, *f1, vec(ln1_g[i]), vec(ln1_b[i]))

        qk_r, v_r, og_r, gt_r, aq_r, ak_r, av_r, gate_r = _proj(hr, w_all, bif, bg)
        qk_m, v_m, og_m, gt_m, aq_m, ak_m, av_m, gate_m = _proj(hm, w_all, bif, bg)
        qk_m = jnp.where(is_meta, qk_m, 0.0)
        gt_m = jnp.where(is_meta, gt_m, null_gate)

        xm_m, *state = _mlstm(qk_m, v_m, og_m, gt_m, cw, cb, vec(m_norm_g[i]), state0,
                              batch=1, chunk=META_ROWS)
        xm_r = _mlstm(qk_r, v_r, og_r, gt_r, cw, cb, vec(m_norm_g[i]), tuple(state),
                      batch=batch, chunk=min(MLSTM_CHUNK, seq))[0]

        xa_m = _attn_meta(aq_m, ak_m, av_m, lam_rows, slopes, vec(a_norm_g[i]), lam_init=lam_init)
        xa_r = _attn(aq_r, ak_r, av_r, ak_m, av_m, alibi_cols, lam_rows, slopes, vec(a_norm_g[i]),
                     batch=batch, lam_init=lam_init)

        tail_params = (wbm, wba, wo, vec(ln2_g[i]), vec(ln2_b[i]), *f2, vec(ln3_g[i]), vec(ln3_b[i]))
        hr = _merge_ffn(hr, xm_r, xa_r, gate_r, *tail_params)
        hm = _merge_ffn(hm, xm_m, xa_m, gate_m, *tail_params)

    return hr.reshape(batch, seq, D_MODEL)
```

```python
import functools
import math

import jax
import jax.numpy as jnp
import numpy as np
from jax import lax
from jax.experimental import pallas as pl
from jax.experimental.pallas import tpu as pltpu

F32 = jnp.float32
BF16 = jnp.bfloat16

D_MODEL = 1024
N_META = 16
META_ROWS = 128
M_HEADS = 4
M_DK = 128
M_WIDTH = 512
CONV_W = 4
A_HEADS = 4
A_DK = 64
A_DV = 128
A_WIDTH = 512
D_FF = 2816
FF_CHUNK = 256
N_FF_CHUNKS = D_FF // FF_CHUNK
DEPTH = 2
ALPHA = (2 * DEPTH) ** 0.25
LN_EPS = 1e-5
NEG = -1e30

SEG_QK = (0, 1024)
SEG_V = (1024, 1536)
SEG_O = (1536, 2048)
SEG_IF = (2048, 2176)
SEG_AQ = (2176, 2688)
SEG_AK = (2688, 3200)
SEG_AV = (3200, 3712)
SEG_G = (3712, 5760)
PROJ_COLS = 5760

ROW_TILE = 512
FFN_ROW_TILE = 1024
MLSTM_CHUNK = 256
MLSTM_GROUP = 8
ATTN_BLOCK = 512
ATTN_HEAD_GROUP = 4
LOG2E = math.log2(math.e)
VMEM_LIMIT = 56 * 1024 * 1024


def _nt_dot(a, b):
    return lax.dot_general(a, b, (((1,), (1,)), ((), ())), preferred_element_type=F32)


def _layer_norm(y, g, b):
    mu = jnp.mean(y, -1, keepdims=True)
    d = y - mu
    var = jnp.mean(d * d, -1, keepdims=True)
    return d * lax.rsqrt(var + LN_EPS) * g + b


def _swiglu_ln(x, wg_ref, wu_ref, wd_ref, g_ref, b_ref):
    xb = x.astype(BF16)
    acc = jnp.zeros(x.shape, F32)
    for j in range(N_FF_CHUNKS):
        cols = slice(j * FF_CHUNK, (j + 1) * FF_CHUNK)
        g = jnp.dot(xb, wg_ref[:, cols], preferred_element_type=F32)
        u = jnp.dot(xb, wu_ref[:, cols], preferred_element_type=F32)
        a = (g * jax.nn.sigmoid(g) * u).astype(BF16)
        acc = acc + jnp.dot(a, wd_ref[cols, :], preferred_element_type=F32)
    return _layer_norm(ALPHA * x + 0.5 * acc, g_ref[...], b_ref[...])


def _ffn_ln_kernel(x_ref, wg_ref, wu_ref, wd_ref, g_ref, b_ref, o_ref):
    half = x_ref.shape[0] // 2
    for rows in (slice(0, half), slice(half, 2 * half)):
        o_ref[rows, :] = _swiglu_ln(x_ref[rows, :], wg_ref, wu_ref, wd_ref, g_ref, b_ref)


def _resident(shape):
    return pl.BlockSpec(shape, lambda *_: (0,) * len(shape), pipeline_mode=pl.Buffered(1))


def _ffn_ln(x, w_gate, w_up, w_down, g, b):
    m = x.shape[0]
    tm = min(FFN_ROW_TILE, m)
    return pl.pallas_call(
        _ffn_ln_kernel,
        out_shape=jax.ShapeDtypeStruct((m, D_MODEL), F32),
        grid=(m // tm,),
        in_specs=[
            pl.BlockSpec((tm, D_MODEL), lambda i: (i, 0)),
            _resident((D_MODEL, D_FF)), _resident((D_MODEL, D_FF)), _resident((D_FF, D_MODEL)),
            _resident((1, D_MODEL)), _resident((1, D_MODEL)),
        ],
        out_specs=pl.BlockSpec((tm, D_MODEL), lambda i: (i, 0)),
        compiler_params=pltpu.CompilerParams(
            dimension_semantics=("arbitrary",), vmem_limit_bytes=VMEM_LIMIT),
    )(x, w_gate, w_up, w_down, g, b)


def _proj_kernel(h_ref, w_ref, bif_ref, bg_ref,
                 qk_ref, v_ref, og_ref, gt_ref, aq_ref, ak_ref, av_ref, gate_ref):
    hb = h_ref[...].astype(BF16)

    def seg(s):
        return jnp.dot(hb, w_ref[:, s[0]:s[1]], preferred_element_type=F32)

    gate_ref[...] = jax.nn.sigmoid(seg(SEG_G) + bg_ref[...]).astype(BF16)
    og_ref[...] = jax.nn.sigmoid(seg(SEG_O)).astype(BF16)
    z = seg(SEG_IF) + bif_ref[...]
    col = lax.broadcasted_iota(jnp.int32, z.shape, 1)
    log_f = jnp.minimum(z, 0.0) - jnp.log1p(jnp.exp(-jnp.abs(z)))
    gt_ref[...] = jnp.where(col < M_HEADS, z, jnp.where(col < 2 * M_HEADS, log_f, 0.0))
    aq_ref[...] = (seg(SEG_AQ) * (LOG2E * A_DK ** -0.5)).astype(BF16)
    qk_ref[...] = seg(SEG_QK)
    v_ref[...] = seg(SEG_V).astype(BF16)
    ak_ref[...] = seg(SEG_AK).astype(BF16)
    av_ref[...] = seg(SEG_AV).astype(BF16)


def _proj(h, w_all, b_if, b_gate):
    m = h.shape[0]
    tm = min(ROW_TILE, m)
    row = lambda n: pl.BlockSpec((tm, n), lambda i: (i, 0))
    const = lambda r, n: pl.BlockSpec((r, n), lambda i: (0, 0))
    out_shape = (
        jax.ShapeDtypeStruct((m, 1024), F32),
        jax.ShapeDtypeStruct((m, 512), BF16),
        jax.ShapeDtypeStruct((m, 512), BF16),
        jax.ShapeDtypeStruct((m, 128), F32),
        jax.ShapeDtypeStruct((m, 512), BF16),
        jax.ShapeDtypeStruct((m, 512), BF16),
        jax.ShapeDtypeStruct((m, 512), BF16),
        jax.ShapeDtypeStruct((m, 2048), BF16),
    )
    return pl.pallas_call(
        _proj_kernel,
        out_shape=out_shape,
        grid=(m // tm,),
        in_specs=[row(D_MODEL), const(D_MODEL, PROJ_COLS), const(1, 128), const(1, 2048)],
        out_specs=(row(1024), row(512), row(512), row(128), row(512), row(512), row(512), row(2048)),
        compiler_params=pltpu.CompilerParams(
            dimension_semantics=("arbitrary",), vmem_limit_bytes=VMEM_LIMIT),
    )(h, w_all, b_if, b_gate)


def _mlstm_kernel(qk_ref, v_ref, og_ref, gt_ref, cw_ref, cb_ref, ng_ref,
                  c0_ref, m0_ref, t0_ref,
                  h_ref, c_ref, m_ref, tail_ref, *, T, G):
    @pl.when(pl.program_id(1) == 0)
    def _():
        for g in range(G):
            c_ref[g] = c0_ref[0]
            m_ref[g] = m0_ref[0]
            tail_ref[g] = t0_ref[0]

    r = lax.broadcasted_iota(jnp.int32, (T, T), 0)
    c = lax.broadcasted_iota(jnp.int32, (T, T), 1)
    causal = c <= r
    eye = c == r
    tri = jnp.where(causal, 1.0, 0.0).astype(F32)
    ones = jnp.ones((T, M_DK), BF16)

    for g in range(G):
        u = jnp.concatenate([tail_ref[g], qk_ref[g]], axis=0)
        tail_ref[g] = u[T:T + 8, :]
        y = cb_ref[...] + cw_ref[CONV_W - 1:CONV_W, :] * u
        for j in range(1, CONV_W):
            y = y + cw_ref[CONV_W - 1 - j:CONV_W - j, :] * pltpu.roll(u, j, 0)
        y = y[8:8 + T, :]
        qk = y * jax.nn.sigmoid(y)

        gates = gt_ref[g]
        cums = jnp.dot(tri, gates, preferred_element_type=F32, precision=lax.Precision.HIGHEST)
        u_rows = (gates.T)[0:M_HEADS, :] - (cums.T)[M_HEADS:2 * M_HEADS, :]

        for h in range(M_HEADS):
            sl = slice(h * M_DK, (h + 1) * M_DK)
            qb = qk[:, sl].astype(BF16)
            k_h = qk[:, M_WIDTH + h * M_DK:M_WIDTH + (h + 1) * M_DK] * (M_DK ** -0.5)
            kb = k_h.astype(BF16)
            v_aug = jnp.concatenate([v_ref[g, :, sl], ones], axis=1)
            b_col = cums[:, M_HEADS + h:M_HEADS + h + 1]
            m_prev = m_ref[g, h:h + 1, 0:1]
            cn_prev = c_ref[g, h]

            um = jnp.where(causal, u_rows[h:h + 1, :], NEG)
            big_m = jnp.maximum(jnp.max(um, -1, keepdims=True), m_prev)
            p = jnp.exp(um - big_m)
            s = (_nt_dot(qb, kb) * p).astype(BF16)
            w_state = jnp.exp(m_prev - big_m)
            tot = (jnp.dot(s, v_aug, preferred_element_type=F32)
                   + w_state * jnp.dot(qb, cn_prev.astype(BF16), preferred_element_type=F32))
            num = tot[:, 0:M_DK]
            den = tot[:, M_DK:2 * M_DK]
            hh = num / jnp.maximum(jnp.abs(den), jnp.exp(-(b_col + big_m)))
            mu = jnp.mean(hh, -1, keepdims=True)
            d = hh - mu
            hn = d * lax.rsqrt(jnp.mean(d * d, -1, keepdims=True) + LN_EPS)
            h_ref[g, :, sl] = (hn * ng_ref[:, sl] * og_ref[g, :, sl].astype(F32)).astype(BF16)

            w_diag = jnp.where(eye, p[T - 1:T, :], 0.0).astype(BF16)
            vw = jnp.dot(w_diag, v_aug, preferred_element_type=F32)
            c_ref[g, h] = (w_state[T - 1:T, :] * cn_prev
                           + jnp.dot(k_h.T.astype(BF16), vw.astype(BF16), preferred_element_type=F32))
            m_ref[g, h:h + 1, :] = jnp.broadcast_to(b_col[T - 1:T, :] + big_m[T - 1:T, :], (1, 128))


def _mlstm(qk, v, og, gates, conv_w, conv_b, norm_g, state, *, batch, chunk):
    seq = qk.shape[0] // batch
    nc = seq // chunk
    group = min(MLSTM_GROUP, batch)
    c0, m0, t0 = state
    seq3 = lambda a: a.reshape(batch, seq, a.shape[-1])
    row = lambda n: pl.BlockSpec((group, chunk, n), lambda b, c: (b, c, 0))
    const2 = lambda r, n: pl.BlockSpec((r, n), lambda b, c: (0, 0))
    init3 = lambda r, n: pl.BlockSpec((1, r, n), lambda b, c: (0, 0, 0))
    out3 = lambda r, n: pl.BlockSpec((group, r, n), lambda b, c: (b, 0, 0))
    out_shape = (
        jax.ShapeDtypeStruct((batch, seq, M_WIDTH), BF16),
        jax.ShapeDtypeStruct((batch, M_HEADS, M_DK, 2 * M_DK), F32),
        jax.ShapeDtypeStruct((batch, 8, 128), F32),
        jax.ShapeDtypeStruct((batch, 8, 1024), F32),
    )
    out, c1, m1, t1 = pl.pallas_call(
        functools.partial(_mlstm_kernel, T=chunk, G=group),
        out_shape=out_shape,
        grid=(batch // group, nc),
        in_specs=[
            row(1024), row(512), row(512), row(128),
            const2(8, 1024), const2(1, 1024), const2(1, 512),
            pl.BlockSpec((1, M_HEADS, M_DK, 2 * M_DK), lambda b, c: (0, 0, 0, 0)),
            init3(8, 128), init3(8, 1024),
        ],
        out_specs=(
            row(512),
            pl.BlockSpec((group, M_HEADS, M_DK, 2 * M_DK), lambda b, c: (b, 0, 0, 0)),
            out3(8, 128), out3(8, 1024),
        ),
        compiler_params=pltpu.CompilerParams(
            dimension_semantics=("arbitrary", "arbitrary"), vmem_limit_bytes=VMEM_LIMIT),
    )(seq3(qk), seq3(v), seq3(og), seq3(gates), conv_w, conv_b, norm_g, c0, m0, t0)
    return out.reshape(batch * seq, M_WIDTH), c1, m1, t1


def _stack_maps(q):
    lane = lax.broadcasted_iota(jnp.int32, q.shape, 1)
    zero = jnp.zeros_like(q)
    return jnp.concatenate([jnp.where(lane < A_DK, q, zero), jnp.where(lane >= A_DK, q, zero)], axis=0)


def _diff_finish(acc, l, lam_ref, ng, lam_init, t):
    lam = (jnp.exp(jnp.sum(lam_ref[0:1, :] * lam_ref[1:2, :], -1, keepdims=True))
           - jnp.exp(jnp.sum(lam_ref[2:3, :] * lam_ref[3:4, :], -1, keepdims=True)) + lam_init)
    o = acc[:t] / l[:t] - lam * (acc[t:] / l[t:])
    o = o * lax.rsqrt(jnp.mean(o * o, -1, keepdims=True) + LN_EPS) * (1.0 - lam_init)
    return (o * ng).astype(BF16)


ACC_ROWS = A_DV + 16


def _fold_keys(s_t, offset, v_t, m_s, acc_s):
    keys, width = s_t.shape
    m_old = m_s[...]
    m_new = jnp.maximum(m_old, jnp.max(s_t, axis=0, keepdims=True) + offset)
    alpha = jnp.exp2(m_old - m_new)
    p3 = jnp.exp2(s_t.reshape(keys // 8, 8, width) - (m_new - offset)[None])
    p = p3.reshape(keys, width).astype(BF16)
    acc3 = acc_s[...].reshape(ACC_ROWS // 8, 8, width)
    acc_s[...] = ((acc3 * alpha[None]).reshape(acc_s.shape)
                  + jnp.dot(v_t, p, preferred_element_type=F32))
    m_s[...] = m_new


def _attn_kernel(q_ref, k_ref, v_ref, km_ref, vm_ref, ab_ref, lam_ref, sl_ref, ngt_ref, o_ref,
                 m_s, acc_s, *, T, HG, lam_init):
    i = pl.program_id(2)
    q0 = i * T
    R = 2 * T
    cols = [slice(h * 128, (h + 1) * 128) for h in range(HG)]
    slope = [sl_ref[:, h * 128:h * 128 + 1] for h in range(HG)]
    row = lax.broadcasted_iota(jnp.int32, (128, R), 0)
    ones_rows = jnp.where(row < 3, 1.0, 0.0).astype(BF16)
    qs_t = [jnp.concatenate([_stack_maps(q_ref[:, cols[h]]).astype(F32).T.astype(BF16), ones_rows], axis=0)
            for h in range(HG)]
    stats = [(m_s.at[h], acc_s.at[h]) for h in range(HG)]

    m_s[...] = jnp.full(m_s.shape, NEG, F32)
    acc_s[...] = jnp.zeros(acc_s.shape, F32)

    def rows(j):
        return pl.ds(pl.multiple_of(j * T, T), T)

    def values_t(v):
        v_t = v.astype(F32).T.astype(BF16)
        return jnp.concatenate([v_t, jnp.ones((ACC_ROWS - A_DV, v_t.shape[1]), BF16)], axis=0)

    zero_off = jnp.zeros((1, 1), F32)

    km_pos = lax.broadcasted_iota(jnp.int32, (META_ROWS, 128), 0)
    s_meta = []
    for h in range(HG):
        bias = jnp.where(km_pos >= META_ROWS - N_META,
                         slope[h] * (km_pos - (META_ROWS + q0)).astype(F32), NEG)
        s_meta.append(jnp.dot(km_ref[:, cols[h]], qs_t[h][0:128], preferred_element_type=F32)
                      + jnp.concatenate([bias] * (R // 128), axis=1))
    for h in range(HG):
        _fold_keys(s_meta[h], zero_off, values_t(vm_ref[:, cols[h]]), *stats[h])

    def logits(h, j):
        k_aug = jnp.concatenate([k_ref[rows(j), cols[h]], ab_ref[:, cols[h]]], axis=1)
        return jnp.dot(k_aug, qs_t[h], preferred_element_type=F32)

    def offset(h, j):
        return slope[h] * (j * T - q0).astype(F32)

    def fold_blocks(h, first, n):
        scores = [logits(h, first + k) for k in range(n)]
        for k in range(n):
            _fold_keys(scores[k], offset(h, first + k), values_t(v_ref[rows(first + k), cols[h]]), *stats[h])

    def fold_pair_of_heads(h0, first, n):
        scores = [[logits(h, first + k) for k in range(n)] for h in (h0, h0 + 1)]
        for k in range(n):
            for d in (0, 1):
                _fold_keys(scores[d][k], offset(h0 + d, first + k),
                           values_t(v_ref[rows(first + k), cols[h0 + d]]), *stats[h0 + d])

    for h0 in range(0, HG, 2):
        lax.fori_loop(0, i // 4, lambda t, carry, h0=h0: fold_pair_of_heads(h0, 4 * t, 4) or carry, 0)

    @pl.when(i % 4 >= 2)
    def _():
        for h in range(HG):
            fold_blocks(h, i // 4 * 4, 2)

    @pl.when(i % 2 == 1)
    def _():
        scores = [logits(h, i - 1) for h in range(HG)]
        for h in range(HG):
            _fold_keys(scores[h], offset(h, i - 1), values_t(v_ref[rows(i - 1), cols[h]]), *stats[h])

    key = lax.broadcasted_iota(jnp.int32, (T, T), 0)
    qry = lax.broadcasted_iota(jnp.int32, (T, T), 1)
    keep = key <= qry
    keep2 = jnp.concatenate([keep, keep], axis=1)
    s_diag = [jnp.where(keep2, logits(h, i), NEG) for h in range(HG)]
    for h in range(HG):
        _fold_keys(s_diag[h], zero_off, values_t(v_ref[rows(i), cols[h]]), *stats[h])
    lam = (jnp.exp(jnp.sum(lam_ref[0:1, :] * lam_ref[1:2, :], -1, keepdims=True))
           - jnp.exp(jnp.sum(lam_ref[2:3, :] * lam_ref[3:4, :], -1, keepdims=True)) + lam_init)
    for h in range(HG):
        acc = acc_s[h, 0:A_DV, :]
        l = acc_s[h, A_DV:A_DV + 1, :]
        o_t = acc[:, 0:T] / l[:, 0:T] - lam * (acc[:, T:R] / l[:, T:R])
        o_t = o_t * lax.rsqrt(jnp.mean(o_t * o_t, axis=0, keepdims=True) + LN_EPS) * (1.0 - lam_init)
        o_t = jnp.concatenate([o_t[:, t * 128:(t + 1) * 128] * ngt_ref[cols[h], :] for t in range(T // 128)], axis=1)
        o_ref[:, cols[h]] = o_t.T.astype(BF16)


def _alibi_columns(slopes_log2, t):
    x = np.asarray(slopes_log2, np.float32)[None, :] * np.arange(t, dtype=np.float32)[:, None]
    hi = x.astype(BF16)
    mid = (x - hi.astype(np.float32)).astype(BF16)
    lo = (x - hi.astype(np.float32) - mid.astype(np.float32)).astype(BF16)
    cols = np.zeros((t, A_HEADS, 128), BF16)
    cols[:, :, 0], cols[:, :, 1], cols[:, :, 2] = hi, mid, lo
    return jnp.asarray(cols.reshape(t, A_HEADS * 128))


def _attn(aq, ak, av, ak_meta, av_meta, alibi_cols, lam_rows, slopes, norm_g, *, batch, lam_init):
    m = aq.shape[0]
    seq = m // batch
    t = alibi_cols.shape[0]
    nq = seq // t
    hg = ATTN_HEAD_GROUP
    w = hg * 128
    qspec = pl.BlockSpec((t, w), lambda b, h, i: (b * nq + i, h))
    kvspec = pl.BlockSpec((seq, w), lambda b, h, i: (b, h), pipeline_mode=pl.Buffered(1))
    mspec = pl.BlockSpec((META_ROWS, w), lambda b, h, i: (0, h))
    hrow = pl.BlockSpec((1, w), lambda b, h, i: (0, h))
    norm_g_t = jnp.broadcast_to(norm_g.reshape(A_WIDTH, 1), (A_WIDTH, 128))
    return pl.pallas_call(
        functools.partial(_attn_kernel, T=t, HG=hg, lam_init=lam_init),
        out_shape=jax.ShapeDtypeStruct((m, A_WIDTH), BF16),
        grid=(batch, A_HEADS // hg, nq),
        in_specs=[qspec, kvspec, kvspec, mspec, mspec,
                  pl.BlockSpec((t, w), lambda b, h, i: (0, h)),
                  pl.BlockSpec((8, 128), lambda b, h, i: (0, 0)), hrow,
                  pl.BlockSpec((w, 128), lambda b, h, i: (h, 0))],
        out_specs=qspec,
        scratch_shapes=[pltpu.VMEM((hg, 8, 2 * t), F32), pltpu.VMEM((hg, ACC_ROWS, 2 * t), F32)],
        compiler_params=pltpu.CompilerParams(
            dimension_semantics=("arbitrary", "arbitrary", "arbitrary"),
            vmem_limit_bytes=VMEM_LIMIT),
    )(aq, ak, av, ak_meta, av_meta, alibi_cols, lam_rows, slopes, norm_g_t)


def _attn_meta_kernel(q_ref, k_ref, v_ref, lam_ref, sl_ref, ng_ref, o_ref, *, lam_init):
    t = META_ROWS
    slope = sl_ref[:, 0:1]
    qs = _stack_maps(q_ref[...])
    r = lax.broadcasted_iota(jnp.int32, (t, t), 0)
    c = lax.broadcasted_iota(jnp.int32, (t, t), 1)
    bias = jnp.where((c >= t - N_META) & (c <= r), slope * (c - r).astype(F32), NEG)
    s = _nt_dot(qs, k_ref[...]) + jnp.concatenate([bias, bias], axis=0)
    p = jnp.exp2(s - jnp.max(s, -1, keepdims=True))
    l = jnp.sum(p, -1, keepdims=True)
    acc = jnp.dot(p.astype(BF16), v_ref[...], preferred_element_type=F32)
    o_ref[...] = _diff_finish(acc, l, lam_ref, ng_ref[...], lam_init, t)


def _attn_meta(aq, ak, av, lam_rows, slopes, norm_g, *, lam_init):
    blk = pl.BlockSpec((META_ROWS, 128), lambda h: (0, h))
    hrow = pl.BlockSpec((1, 128), lambda h: (0, h))
    return pl.pallas_call(
        functools.partial(_attn_meta_kernel, lam_init=lam_init),
        out_shape=jax.ShapeDtypeStruct((META_ROWS, A_WIDTH), BF16),
        grid=(A_HEADS,),
        in_specs=[blk, blk, blk, pl.BlockSpec((8, 128), lambda h: (0, 0)), hrow, hrow],
        out_specs=blk,
        compiler_params=pltpu.CompilerParams(dimension_semantics=("arbitrary",)),
    )(aq, ak, av, lam_rows, slopes, norm_g)


def _merge_ffn_kernel(h_ref, hm_ref, ha_ref, gate_ref, wbm_ref, wba_ref, wo_ref, g2_ref, b2_ref,
                      wg_ref, wu_ref, wd_ref, g3_ref, b3_ref, o_ref):
    ym = jnp.dot(hm_ref[...], wbm_ref[...], preferred_element_type=F32)
    ya = jnp.dot(ha_ref[...], wba_ref[...], preferred_element_type=F32)
    gm = gate_ref[:, 0:D_MODEL].astype(F32)
    ga = gate_ref[:, D_MODEL:2 * D_MODEL].astype(F32)
    mixed = (gm * ym + ga * ya).astype(BF16)
    mix = jnp.dot(mixed, wo_ref[...], preferred_element_type=F32)
    h2 = _layer_norm(ALPHA * h_ref[...] + mix, g2_ref[...], b2_ref[...])
    o_ref[...] = _swiglu_ln(h2, wg_ref, wu_ref, wd_ref, g3_ref, b3_ref)


def _merge_ffn(h, hm, ha, gate, w_bm, w_ba, w_out, g2, b2, w_gate, w_up, w_down, g3, b3):
    m = h.shape[0]
    tm = min(ROW_TILE, m)
    row = lambda n: pl.BlockSpec((tm, n), lambda i: (i, 0))
    return pl.pallas_call(
        _merge_ffn_kernel,
        out_shape=jax.ShapeDtypeStruct((m, D_MODEL), F32),
        grid=(m // tm,),
        in_specs=[row(D_MODEL), row(512), row(512), row(2048),
                  _resident((512, D_MODEL)), _resident((512, D_MODEL)), _resident((D_MODEL, D_MODEL)),
                  _resident((1, D_MODEL)), _resident((1, D_MODEL)),
                  _resident((D_MODEL, D_FF)), _resident((D_MODEL, D_FF)), _resident((D_FF, D_MODEL)),
                  _resident((1, D_MODEL)), _resident((1, D_MODEL))],
        out_specs=row(D_MODEL),
        compiler_params=pltpu.CompilerParams(
            dimension_semantics=("arbitrary",), vmem_limit_bytes=VMEM_LIMIT),
    )(h, hm, ha, gate, w_bm, w_ba, w_out, g2, b2, w_gate, w_up, w_down, g3, b3)


def _ffn_weights(w_gate, w_up, w_down):
    return w_gate.astype(BF16), w_up.astype(BF16), w_down.astype(BF16)


def _proj_weights(w_in, b_if, b_gate):
    w = w_in.astype(BF16)
    w_all = jnp.concatenate(
        [w[:, :2048], jnp.pad(w[:, 2048:2056], ((0, 0), (0, 120))), w[:, 2056:]], axis=1)
    return w_all, jnp.pad(b_if, (0, 120)).reshape(1, 128), b_gate.reshape(1, 2 * D_MODEL)


def kernel(x, meta, ffn1_w_gate, ffn1_w_up, ffn1_w_down, ln1_g, ln1_b, w_in, conv_w, conv_b, b_if, m_norm_g, lam_q1, lam_k1, lam_q2, lam_k2, a_norm_g, w_bm, w_ba, b_gate, w_out, ln2_g, ln2_b, ffn2_w_gate, ffn2_w_up, ffn2_w_down, ln3_g, ln3_b):
    batch, seq, _ = x.shape
    rows = batch * seq
    hr = x.reshape(rows, D_MODEL)
    hm = jnp.pad(meta.astype(x.dtype), ((META_ROWS - N_META, 0), (0, 0)))

    slopes_log2 = np.array([LOG2E * 2.0 ** (-8.0 * (h + 1) / A_HEADS) for h in range(A_HEADS)], np.float32)
    slopes = jnp.asarray(np.repeat(slopes_log2, 128).reshape(1, 512))
    alibi_cols = _alibi_columns(slopes_log2, min(ATTN_BLOCK, seq))
    is_meta = (jnp.arange(META_ROWS) >= META_ROWS - N_META)[:, None]
    null_gate = jnp.where(jnp.arange(128) < M_HEADS, NEG, 0.0).astype(F32)[None, :]
    state0 = (jnp.zeros((1, M_HEADS, M_DK, 2 * M_DK), F32), jnp.full((1, 8, 128), NEG, F32),
              jnp.zeros((1, 8, 1024), F32))
    vec = lambda a: a.reshape(1, -1)

    for i in range(DEPTH):
        lam_init = 0.8 - 0.6 * math.exp(-0.3 * i)
        f1 = _ffn_weights(ffn1_w_gate[i], ffn1_w_up[i], ffn1_w_down[i])
        f2 = _ffn_weights(ffn2_w_gate[i], ffn2_w_up[i], ffn2_w_down[i])
        w_all, bif, bg = _proj_weights(w_in[i], b_if[i], b_gate[i])
        cw = jnp.pad(conv_w[i], ((0, 8 - CONV_W), (0, 0)))
        cb = vec(conv_b[i])
        lam_rows = jnp.pad(jnp.stack([lam_q1[i], lam_k1[i], lam_q2[i], lam_k2[i]]).astype(F32),
                           ((0, 4), (0, 128 - A_DK)))
        wbm, wba, wo = w_bm[i].astype(BF16), w_ba[i].astype(BF16), w_out[i].astype(BF16)

        hr = _ffn_ln(hr, *f1, vec(ln1_g[i]), vec(ln1_b[i]))
        hm = _ffn_ln(hm, *f1, vec(ln1_g[i]), vec(ln1_b[i]))

        qk_r, v_r, og_r, gt_r, aq_r, ak_r, av_r, gate_r = _proj(hr, w_all, bif, bg)
        qk_m, v_m, og_m, gt_m, aq_m, ak_m, av_m, gate_m = _proj(hm, w_all, bif, bg)
        qk_m = jnp.where(is_meta, qk_m, 0.0)
        gt_m = jnp.where(is_meta, gt_m, null_gate)

        xm_m, *state = _mlstm(qk_m, v_m, og_m, gt_m, cw, cb, vec(m_norm_g[i]), state0,
                              batch=1, chunk=META_ROWS)
        xm_r = _mlstm(qk_r, v_r, og_r, gt_r, cw, cb, vec(m_norm_g[i]), tuple(state),
                      batch=batch, chunk=min(MLSTM_CHUNK, seq))[0]

        xa_m = _attn_meta(aq_m, ak_m, av_m, lam_rows, slopes, vec(a_norm_g[i]), lam_init=lam_init)
        xa_r = _attn(aq_r, ak_r, av_r, ak_m, av_m, alibi_cols, lam_rows, slopes, vec(a_norm_g[i]),
                     batch=batch, lam_init=lam_init)

        tail_params = (wbm, wba, wo, vec(ln2_g[i]), vec(ln2_b[i]), *f2, vec(ln3_g[i]), vec(ln3_b[i]))
        hr = _merge_ffn(hr, xm_r, xa_r, gate_r, *tail_params)
        hm = _merge_ffn(hm, xm_m, xa_m, gate_m, *tail_params)

    return hr.reshape(batch, seq, D_MODEL)
```

```python
import functools
import math

import jax
import jax.numpy as jnp
import numpy as np
from jax import lax
from jax.experimental import pallas as pl
from jax.experimental.pallas import tpu as pltpu

F32 = jnp.float32
BF16 = jnp.bfloat16

D_MODEL = 1024
N_META = 16
META_ROWS = 128
M_HEADS = 4
M_DK = 128
M_WIDTH = 512
CONV_W = 4
A_HEADS = 4
A_DK = 64
A_DV = 128
A_WIDTH = 512
D_FF = 2816
FF_CHUNK = 256
N_FF_CHUNKS = D_FF // FF_CHUNK
DEPTH = 2
ALPHA = (2 * DEPTH) ** 0.25
LN_EPS = 1e-5
NEG = -1e30

SEG_QK = (0, 1024)
SEG_V = (1024, 1536)
SEG_O = (1536, 2048)
SEG_IF = (2048, 2176)
SEG_AQ = (2176, 2688)
SEG_AK = (2688, 3200)
SEG_AV = (3200, 3712)
SEG_G = (3712, 5760)
PROJ_COLS = 5760

ROW_TILE = 512
FFN_ROW_TILE = 1024
MLSTM_CHUNK = 256
MLSTM_GROUP = 8
ATTN_BLOCK = 512
ATTN_HEAD_GROUP = 2
LOG2E = math.log2(math.e)
VMEM_LIMIT = 56 * 1024 * 1024


def _nt_dot(a, b):
    return lax.dot_general(a, b, (((1,), (1,)), ((), ())), preferred_element_type=F32)


def _layer_norm(y, g, b):
    mu = jnp.mean(y, -1, keepdims=True)
    d = y - mu
    var = jnp.mean(d * d, -1, keepdims=True)
    return d * lax.rsqrt(var + LN_EPS) * g + b


def _swiglu_ln(x, wg_ref, wu_ref, wd_ref, g_ref, b_ref):
    xb = x.astype(BF16)
    acc = jnp.zeros(x.shape, F32)
    for j in range(N_FF_CHUNKS):
        cols = slice(j * FF_CHUNK, (j + 1) * FF_CHUNK)
        g = jnp.dot(xb, wg_ref[:, cols], preferred_element_type=F32)
        u = jnp.dot(xb, wu_ref[:, cols], preferred_element_type=F32)
        a = (g * jax.nn.sigmoid(g) * u).astype(BF16)
        acc = acc + jnp.dot(a, wd_ref[cols, :], preferred_element_type=F32)
    return _layer_norm(ALPHA * x + 0.5 * acc, g_ref[...], b_ref[...])


def _ffn_ln_kernel(x_ref, wg_ref, wu_ref, wd_ref, g_ref, b_ref, o_ref):
    half = x_ref.shape[0] // 2
    for rows in (slice(0, half), slice(half, 2 * half)):
        o_ref[rows, :] = _swiglu_ln(x_ref[rows, :], wg_ref, wu_ref, wd_ref, g_ref, b_ref)


def _resident(shape):
    return pl.BlockSpec(shape, lambda *_: (0,) * len(shape), pipeline_mode=pl.Buffered(1))


def _ffn_ln(x, w_gate, w_up, w_down, g, b):
    m = x.shape[0]
    tm = min(FFN_ROW_TILE, m)
    return pl.pallas_call(
        _ffn_ln_kernel,
        out_shape=jax.ShapeDtypeStruct((m, D_MODEL), F32),
        grid=(m // tm,),
        in_specs=[
            pl.BlockSpec((tm, D_MODEL), lambda i: (i, 0)),
            _resident((D_MODEL, D_FF)), _resident((D_MODEL, D_FF)), _resident((D_FF, D_MODEL)),
            _resident((1, D_MODEL)), _resident((1, D_MODEL)),
        ],
        out_specs=pl.BlockSpec((tm, D_MODEL), lambda i: (i, 0)),
        compiler_params=pltpu.CompilerParams(
            dimension_semantics=("arbitrary",), vmem_limit_bytes=VMEM_LIMIT),
    )(x, w_gate, w_up, w_down, g, b)


def _proj_kernel(h_ref, w_ref, bif_ref, bg_ref,
                 qk_ref, v_ref, og_ref, gt_ref, aq_ref, ak_ref, av_ref, gate_ref):
    hb = h_ref[...].astype(BF16)

    def seg(s):
        return jnp.dot(hb, w_ref[:, s[0]:s[1]], preferred_element_type=F32)

    gate_ref[...] = jax.nn.sigmoid(seg(SEG_G) + bg_ref[...]).astype(BF16)
    og_ref[...] = jax.nn.sigmoid(seg(SEG_O)).astype(BF16)
    z = seg(SEG_IF) + bif_ref[...]
    col = lax.broadcasted_iota(jnp.int32, z.shape, 1)
    log_f = jnp.minimum(z, 0.0) - jnp.log1p(jnp.exp(-jnp.abs(z)))
    gt_ref[...] = jnp.where(col < M_HEADS, z, jnp.where(col < 2 * M_HEADS, log_f, 0.0))
    aq_ref[...] = (seg(SEG_AQ) * (LOG2E * A_DK ** -0.5)).astype(BF16)
    qk_ref[...] = seg(SEG_QK)
    v_ref[...] = seg(SEG_V).astype(BF16)
    ak_ref[...] = seg(SEG_AK).astype(BF16)
    av_ref[...] = seg(SEG_AV).astype(BF16)


def _proj(h, w_all, b_if, b_gate):
    m = h.shape[0]
    tm = min(ROW_TILE, m)
    row = lambda n: pl.BlockSpec((tm, n), lambda i: (i, 0))
    const = lambda r, n: pl.BlockSpec((r, n), lambda i: (0, 0))
    out_shape = (
        jax.ShapeDtypeStruct((m, 1024), F32),
        jax.ShapeDtypeStruct((m, 512), BF16),
        jax.ShapeDtypeStruct((m, 512), BF16),
        jax.ShapeDtypeStruct((m, 128), F32),
        jax.ShapeDtypeStruct((m, 512), BF16),
        jax.ShapeDtypeStruct((m, 512), BF16),
        jax.ShapeDtypeStruct((m, 512), BF16),
        jax.ShapeDtypeStruct((m, 2048), BF16),
    )
    return pl.pallas_call(
        _proj_kernel,
        out_shape=out_shape,
        grid=(m // tm,),
        in_specs=[row(D_MODEL), const(D_MODEL, PROJ_COLS), const(1, 128), const(1, 2048)],
        out_specs=(row(1024), row(512), row(512), row(128), row(512), row(512), row(512), row(2048)),
        compiler_params=pltpu.CompilerParams(
            dimension_semantics=("arbitrary",), vmem_limit_bytes=VMEM_LIMIT),
    )(h, w_all, b_if, b_gate)


def _mlstm_kernel(qk_ref, v_ref, og_ref, gt_ref, cw_ref, cb_ref, ng_ref,
                  c0_ref, m0_ref, t0_ref,
                  h_ref, c_ref, m_ref, tail_ref, *, T, G):
    @pl.when(pl.program_id(1) == 0)
    def _():
        for g in range(G):
            c_ref[g] = c0_ref[0]
            m_ref[g] = m0_ref[0]
            tail_ref[g] = t0_ref[0]

    r = lax.broadcasted_iota(jnp.int32, (T, T), 0)
    c = lax.broadcasted_iota(jnp.int32, (T, T), 1)
    causal = c <= r
    eye = c == r
    tri = jnp.where(causal, 1.0, 0.0).astype(F32)
    ones = jnp.ones((T, M_DK), BF16)

    for g in range(G):
        u = jnp.concatenate([tail_ref[g], qk_ref[g]], axis=0)
        tail_ref[g] = u[T:T + 8, :]
        y = cb_ref[...] + cw_ref[CONV_W - 1:CONV_W, :] * u
        for j in range(1, CONV_W):
            y = y + cw_ref[CONV_W - 1 - j:CONV_W - j, :] * pltpu.roll(u, j, 0)
        y = y[8:8 + T, :]
        qk = y * jax.nn.sigmoid(y)

        gates = gt_ref[g]
        cums = jnp.dot(tri, gates, preferred_element_type=F32, precision=lax.Precision.HIGHEST)
        u_rows = (gates.T)[0:M_HEADS, :] - (cums.T)[M_HEADS:2 * M_HEADS, :]

        for h in range(M_HEADS):
            sl = slice(h * M_DK, (h + 1) * M_DK)
            qb = qk[:, sl].astype(BF16)
            k_h = qk[:, M_WIDTH + h * M_DK:M_WIDTH + (h + 1) * M_DK] * (M_DK ** -0.5)
            kb = k_h.astype(BF16)
            v_aug = jnp.concatenate([v_ref[g, :, sl], ones], axis=1)
            b_col = cums[:, M_HEADS + h:M_HEADS + h + 1]
            m_prev = m_ref[g, h:h + 1, 0:1]
            cn_prev = c_ref[g, h]

            um = jnp.where(causal, u_rows[h:h + 1, :], NEG)
            big_m = jnp.maximum(jnp.max(um, -1, keepdims=True), m_prev)
            p = jnp.exp(um - big_m)
            s = (_nt_dot(qb, kb) * p).astype(BF16)
            w_state = jnp.exp(m_prev - big_m)
            tot = (jnp.dot(s, v_aug, preferred_element_type=F32)
                   + w_state * jnp.dot(qb, cn_prev.astype(BF16), preferred_element_type=F32))
            num = tot[:, 0:M_DK]
            den = tot[:, M_DK:2 * M_DK]
            hh = num / jnp.maximum(jnp.abs(den), jnp.exp(-(b_col + big_m)))
            mu = jnp.mean(hh, -1, keepdims=True)
            d = hh - mu
            hn = d * lax.rsqrt(jnp.mean(d * d, -1, keepdims=True) + LN_EPS)
            h_ref[g, :, sl] = (hn * ng_ref[:, sl] * og_ref[g, :, sl].astype(F32)).astype(BF16)

            w_diag = jnp.where(eye, p[T - 1:T, :], 0.0).astype(BF16)
            vw = jnp.dot(w_diag, v_aug, preferred_element_type=F32)
            c_ref[g, h] = (w_state[T - 1:T, :] * cn_prev
                           + jnp.dot(k_h.T.astype(BF16), vw.astype(BF16), preferred_element_type=F32))
            m_ref[g, h:h + 1, :] = jnp.broadcast_to(b_col[T - 1:T, :] + big_m[T - 1:T, :], (1, 128))


def _mlstm(qk, v, og, gates, conv_w, conv_b, norm_g, state, *, batch, chunk):
    seq = qk.shape[0] // batch
    nc = seq // chunk
    group = min(MLSTM_GROUP, batch)
    c0, m0, t0 = state
    seq3 = lambda a: a.reshape(batch, seq, a.shape[-1])
    row = lambda n: pl.BlockSpec((group, chunk, n), lambda b, c: (b, c, 0))
    const2 = lambda r, n: pl.BlockSpec((r, n), lambda b, c: (0, 0))
    init3 = lambda r, n: pl.BlockSpec((1, r, n), lambda b, c: (0, 0, 0))
    out3 = lambda r, n: pl.BlockSpec((group, r, n), lambda b, c: (b, 0, 0))
    out_shape = (
        jax.ShapeDtypeStruct((batch, seq, M_WIDTH), BF16),
        jax.ShapeDtypeStruct((batch, M_HEADS, M_DK, 2 * M_DK), F32),
        jax.ShapeDtypeStruct((batch, 8, 128), F32),
        jax.ShapeDtypeStruct((batch, 8, 1024), F32),
    )
    out, c1, m1, t1 = pl.pallas_call(
        functools.partial(_mlstm_kernel, T=chunk, G=group),
        out_shape=out_shape,
        grid=(batch // group, nc),
        in_specs=[
            row(1024), row(512), row(512), row(128),
            const2(8, 1024), const2(1, 1024), const2(1, 512),
            pl.BlockSpec((1, M_HEADS, M_DK, 2 * M_DK), lambda b, c: (0, 0, 0, 0)),
            init3(8, 128), init3(8, 1024),
        ],
        out_specs=(
            row(512),
            pl.BlockSpec((group, M_HEADS, M_DK, 2 * M_DK), lambda b, c: (b, 0, 0, 0)),
            out3(8, 128), out3(8, 1024),
        ),
        compiler_params=pltpu.CompilerParams(
            dimension_semantics=("arbitrary", "arbitrary"), vmem_limit_bytes=VMEM_LIMIT),
    )(seq3(qk), seq3(v), seq3(og), seq3(gates), conv_w, conv_b, norm_g, c0, m0, t0)
    return out.reshape(batch * seq, M_WIDTH), c1, m1, t1


def _stack_maps(q):
    lane = lax.broadcasted_iota(jnp.int32, q.shape, 1)
    zero = jnp.zeros_like(q)
    return jnp.concatenate([jnp.where(lane < A_DK, q, zero), jnp.where(lane >= A_DK, q, zero)], axis=0)


def _diff_finish(acc, l, lam_ref, ng, lam_init, t):
    lam = (jnp.exp(jnp.sum(lam_ref[0:1, :] * lam_ref[1:2, :], -1, keepdims=True))
           - jnp.exp(jnp.sum(lam_ref[2:3, :] * lam_ref[3:4, :], -1, keepdims=True)) + lam_init)
    o = acc[:t] / l[:t] - lam * (acc[t:] / l[t:])
    o = o * lax.rsqrt(jnp.mean(o * o, -1, keepdims=True) + LN_EPS) * (1.0 - lam_init)
    return (o * ng).astype(BF16)


ACC_ROWS = A_DV + 16


def _fold_keys(s_t, offset, v_t, m_s, acc_s):
    keys, width = s_t.shape
    m_old = m_s[...]
    m_new = jnp.maximum(m_old, jnp.max(s_t, axis=0, keepdims=True) + offset)
    alpha = jnp.exp2(m_old - m_new)
    p3 = jnp.exp2(s_t.reshape(keys // 8, 8, width) - (m_new - offset)[None])
    p = p3.reshape(keys, width).astype(BF16)
    acc3 = acc_s[...].reshape(ACC_ROWS // 8, 8, width)
    acc_s[...] = ((acc3 * alpha[None]).reshape(acc_s.shape)
                  + jnp.dot(v_t, p, preferred_element_type=F32))
    m_s[...] = m_new


def _attn_kernel(q_ref, k_ref, v_ref, km_ref, vm_ref, ab_ref, lam_ref, sl_ref, ngt_ref, o_ref,
                 m_s, acc_s, *, T, HG, lam_init):
    i = pl.program_id(2)
    q0 = i * T
    R = 2 * T
    cols = [slice(h * 128, (h + 1) * 128) for h in range(HG)]
    slope = [sl_ref[:, h * 128:h * 128 + 1] for h in range(HG)]
    row = lax.broadcasted_iota(jnp.int32, (128, R), 0)
    ones_rows = jnp.where(row < 3, 1.0, 0.0).astype(BF16)
    qs_t = [jnp.concatenate([_stack_maps(q_ref[:, cols[h]]).astype(F32).T.astype(BF16), ones_rows], axis=0)
            for h in range(HG)]
    stats = [(m_s.at[h], acc_s.at[h]) for h in range(HG)]

    m_s[...] = jnp.full(m_s.shape, NEG, F32)
    acc_s[...] = jnp.zeros(acc_s.shape, F32)

    def rows(j):
        return pl.ds(pl.multiple_of(j * T, T), T)

    def values_t(v):
        v_t = v.astype(F32).T.astype(BF16)
        return jnp.concatenate([v_t, jnp.ones((ACC_ROWS - A_DV, v_t.shape[1]), BF16)], axis=0)

    zero_off = jnp.zeros((1, 1), F32)

    km_pos = lax.broadcasted_iota(jnp.int32, (META_ROWS, 128), 0)
    s_meta = []
    for h in range(HG):
        bias = jnp.where(km_pos >= META_ROWS - N_META,
                         slope[h] * (km_pos - (META_ROWS + q0)).astype(F32), NEG)
        s_meta.append(jnp.dot(km_ref[:, cols[h]], qs_t[h][0:128], preferred_element_type=F32)
                      + jnp.concatenate([bias] * (R // 128), axis=1))
    for h in range(HG):
        _fold_keys(s_meta[h], zero_off, values_t(vm_ref[:, cols[h]]), *stats[h])

    def logits(h, j):
        k_aug = jnp.concatenate([k_ref[rows(j), cols[h]], ab_ref[:, cols[h]]], axis=1)
        return jnp.dot(k_aug, qs_t[h], preferred_element_type=F32)

    def offset(h, j):
        return slope[h] * (j * T - q0).astype(F32)

    def fold_blocks(h, first, n):
        scores = [logits(h, first + k) for k in range(n)]
        for k in range(n):
            _fold_keys(scores[k], offset(h, first + k), values_t(v_ref[rows(first + k), cols[h]]), *stats[h])

    def fold_pair_of_heads(h0, first, n):
        scores = [[logits(h, first + k) for k in range(n)] for h in (h0, h0 + 1)]
        for k in range(n):
            for d in (0, 1):
                _fold_keys(scores[d][k], offset(h0 + d, first + k),
                           values_t(v_ref[rows(first + k), cols[h0 + d]]), *stats[h0 + d])

    for h0 in range(0, HG, 2):
        lax.fori_loop(0, i // 4, lambda t, carry, h0=h0: fold_pair_of_heads(h0, 4 * t, 4) or carry, 0)

    @pl.when(i % 4 >= 2)
    def _():
        for h in range(HG):
            fold_blocks(h, i // 4 * 4, 2)

    @pl.when(i % 2 == 1)
    def _():
        scores = [logits(h, i - 1) for h in range(HG)]
        for h in range(HG):
            _fold_keys(scores[h], offset(h, i - 1), values_t(v_ref[rows(i - 1), cols[h]]), *stats[h])

    key = lax.broadcasted_iota(jnp.int32, (T, T), 0)
    qry = lax.broadcasted_iota(jnp.int32, (T, T), 1)
    keep = key <= qry
    keep2 = jnp.concatenate([keep, keep], axis=1)
    s_diag = [jnp.where(keep2, logits(h, i), NEG) for h in range(HG)]
    for h in range(HG):
        _fold_keys(s_diag[h], zero_off, values_t(v_ref[rows(i), cols[h]]), *stats[h])
    lam = (jnp.exp(jnp.sum(lam_ref[0:1, :] * lam_ref[1:2, :], -1, keepdims=True))
           - jnp.exp(jnp.sum(lam_ref[2:3, :] * lam_ref[3:4, :], -1, keepdims=True)) + lam_init)
    for h in range(HG):
        acc = acc_s[h, 0:A_DV, :]
        l = acc_s[h, A_DV:A_DV + 1, :]
        o_t = acc[:, 0:T] / l[:, 0:T] - lam * (acc[:, T:R] / l[:, T:R])
        o_t = o_t * lax.rsqrt(jnp.mean(o_t * o_t, axis=0, keepdims=True) + LN_EPS) * (1.0 - lam_init)
        o_t = jnp.concatenate([o_t[:, t * 128:(t + 1) * 128] * ngt_ref[cols[h], :] for t in range(T // 128)], axis=1)
        o_ref[:, cols[h]] = o_t.T.astype(BF16)


def _alibi_columns(slopes_log2, t):
    x = np.asarray(slopes_log2, np.float32)[None, :] * np.arange(t, dtype=np.float32)[:, None]
    hi = x.astype(BF16)
    mid = (x - hi.astype(np.float32)).astype(BF16)
    lo = (x - hi.astype(np.float32) - mid.astype(np.float32)).astype(BF16)
    cols = np.zeros((t, A_HEADS, 128), BF16)
    cols[:, :, 0], cols[:, :, 1], cols[:, :, 2] = hi, mid, lo
    return jnp.asarray(cols.reshape(t, A_HEADS * 128))


def _attn(aq, ak, av, ak_meta, av_meta, alibi_cols, lam_rows, slopes, norm_g, *, batch, lam_init):
    m = aq.shape[0]
    seq = m // batch
    t = alibi_cols.shape[0]
    nq = seq // t
    hg = ATTN_HEAD_GROUP
    w = hg * 128
    qspec = pl.BlockSpec((t, w), lambda b, h, i: (b * nq + i, h))
    kvspec = pl.BlockSpec((seq, w), lambda b, h, i: (b, h))
    mspec = pl.BlockSpec((META_ROWS, w), lambda b, h, i: (0, h))
    hrow = pl.BlockSpec((1, w), lambda b, h, i: (0, h))
    norm_g_t = jnp.broadcast_to(norm_g.reshape(A_WIDTH, 1), (A_WIDTH, 128))
    return pl.pallas_call(
        functools.partial(_attn_kernel, T=t, HG=hg, lam_init=lam_init),
        out_shape=jax.ShapeDtypeStruct((m, A_WIDTH), BF16),
        grid=(batch, A_HEADS // hg, nq),
        in_specs=[qspec, kvspec, kvspec, mspec, mspec,
                  pl.BlockSpec((t, w), lambda b, h, i: (0, h)),
                  pl.BlockSpec((8, 128), lambda b, h, i: (0, 0)), hrow,
                  pl.BlockSpec((w, 128), lambda b, h, i: (h, 0))],
        out_specs=qspec,
        scratch_shapes=[pltpu.VMEM((hg, 8, 2 * t), F32), pltpu.VMEM((hg, ACC_ROWS, 2 * t), F32)],
        compiler_params=pltpu.CompilerParams(
            dimension_semantics=("arbitrary", "arbitrary", "arbitrary"),
            vmem_limit_bytes=VMEM_LIMIT),
    )(aq, ak, av, ak_meta, av_meta, alibi_cols, lam_rows, slopes, norm_g_t)


def _attn_meta_kernel(q_ref, k_ref, v_ref, lam_ref, sl_ref, ng_ref, o_ref, *, lam_init):
    t = META_ROWS
    slope = sl_ref[:, 0:1]
    qs = _stack_maps(q_ref[...])
    r = lax.broadcasted_iota(jnp.int32, (t, t), 0)
    c = lax.broadcasted_iota(jnp.int32, (t, t), 1)
    bias = jnp.where((c >= t - N_META) & (c <= r), slope * (c - r).astype(F32), NEG)
    s = _nt_dot(qs, k_ref[...]) + jnp.concatenate([bias, bias], axis=0)
    p = jnp.exp2(s - jnp.max(s, -1, keepdims=True))
    l = jnp.sum(p, -1, keepdims=True)
    acc = jnp.dot(p.astype(BF16), v_ref[...], preferred_element_type=F32)
    o_ref[...] = _diff_finish(acc, l, lam_ref, ng_ref[...], lam_init, t)


def _attn_meta(aq, ak, av, lam_rows, slopes, norm_g, *, lam_init):
    blk = pl.BlockSpec((META_ROWS, 128), lambda h: (0, h))
    hrow = pl.BlockSpec((1, 128), lambda h: (0, h))
    return pl.pallas_call(
        functools.partial(_attn_meta_kernel, lam_init=lam_init),
        out_shape=jax.ShapeDtypeStruct((META_ROWS, A_WIDTH), BF16),
        grid=(A_HEADS,),
        in_specs=[blk, blk, blk, pl.BlockSpec((8, 128), lambda h: (0, 0)), hrow, hrow],
        out_specs=blk,
        compiler_params=pltpu.CompilerParams(dimension_semantics=("arbitrary",)),
    )(aq, ak, av, lam_rows, slopes, norm_g)


def _merge_ffn_kernel(h_ref, hm_ref, ha_ref, gate_ref, wbm_ref, wba_ref, wo_ref, g2_ref, b2_ref,
                      wg_ref, wu_ref, wd_ref, g3_ref, b3_ref, o_ref):
    ym = jnp.dot(hm_ref[...], wbm_ref[...], preferred_element_type=F32)
    ya = jnp.dot(ha_ref[...], wba_ref[...], preferred_element_type=F32)
    gm = gate_ref[:, 0:D_MODEL].astype(F32)
    ga = gate_ref[:, D_MODEL:2 * D_MODEL].astype(F32)
    mixed = (gm * ym + ga * ya).astype(BF16)
    mix = jnp.dot(mixed, wo_ref[...], preferred_element_type=F32)
    h2 = _layer_norm(ALPHA * h_ref[...] + mix, g2_ref[...], b2_ref[...])
    o_ref[...] = _swiglu_ln(h2, wg_ref, wu_ref, wd_ref, g3_ref, b3_ref)


def _merge_ffn(h, hm, ha, gate, w_bm, w_ba, w_out, g2, b2, w_gate, w_up, w_down, g3, b3):
    m = h.shape[0]
    tm = min(ROW_TILE, m)
    row = lambda n: pl.BlockSpec((tm, n), lambda i: (i, 0))
    return pl.pallas_call(
        _merge_ffn_kernel,
        out_shape=jax.ShapeDtypeStruct((m, D_MODEL), F32),
        grid=(m // tm,),
        in_specs=[row(D_MODEL), row(512), row(512), row(2048),
                  _resident((512, D_MODEL)), _resident((512, D_MODEL)), _resident((D_MODEL, D_MODEL)),
                  _resident((1, D_MODEL)), _resident((1, D_MODEL)),
                  _resident((D_MODEL, D_FF)), _resident((D_MODEL, D_FF)), _resident((D_FF, D_MODEL)),
                  _resident((1, D_MODEL)), _resident((1, D_MODEL))],
        out_specs=row(D_MODEL),
        compiler_params=pltpu.CompilerParams(
            dimension_semantics=("arbitrary",), vmem_limit_bytes=VMEM_LIMIT),
    )(h, hm, ha, gate, w_bm, w_ba, w_out, g2, b2, w_gate, w_up, w_down, g3, b3)


def _ffn_weights(w_gate, w_up, w_down):
    return w_gate.astype(BF16), w_up.astype(BF16), w_down.astype(BF16)


def _proj_weights(w_in, b_if, b_gate):
    w = w_in.astype(BF16)
    w_all = jnp.concatenate(
        [w[:, :2048], jnp.pad(w[:, 2048:2056], ((0, 0), (0, 120))), w[:, 2056:]], axis=1)
    return w_all, jnp.pad(b_if, (0, 120)).reshape(1, 128), b_gate.reshape(1, 2 * D_MODEL)


def kernel(x, meta, ffn1_w_gate, ffn1_w_up, ffn1_w_down, ln1_g, ln1_b, w_in, conv_w, conv_b, b_if, m_norm_g, lam_q1, lam_k1, lam_q2, lam_k2, a_norm_g, w_bm, w_ba, b_gate, w_out, ln2_g, ln2_b, ffn2_w_gate, ffn2_w_up, ffn2_w_down, ln3_g, ln3_b):
    batch, seq, _ = x.shape
    rows = batch * seq
    hr = x.reshape(rows, D_MODEL)
    hm = jnp.pad(meta.astype(x.dtype), ((META_ROWS - N_META, 0), (0, 0)))

    slopes_log2 = np.array([LOG2E * 2.0 ** (-8.0 * (h + 1) / A_HEADS) for h in range(A_HEADS)], np.float32)
    slopes = jnp.asarray(np.repeat(slopes_log2, 128).reshape(1, 512))
    alibi_cols = _alibi_columns(slopes_log2, min(ATTN_BLOCK, seq))
    is_meta = (jnp.arange(META_ROWS) >= META_ROWS - N_META)[:, None]
    null_gate = jnp.where(jnp.arange(128) < M_HEADS, NEG, 0.0).astype(F32)[None, :]
    state0 = (jnp.zeros((1, M_HEADS, M_DK, 2 * M_DK), F32), jnp.full((1, 8, 128), NEG, F32),
              jnp.zeros((1, 8, 1024), F32))
    vec = lambda a: a.reshape(1, -1)

    for i in range(DEPTH):
        lam_init = 0.8 - 0.6 * math.exp(-0.3 * i)
        f1 = _ffn_weights(ffn1_w_gate[i], ffn1_w_up[i], ffn1_w_down[i])
        f2 = _ffn_weights(ffn2_w_gate[i], ffn2_w_up[i], ffn2_w_down[i])
        w_all, bif, bg = _proj_weights(w_in[i], b_if[i], b_gate[i])
        cw = jnp.pad(conv_w[i], ((0, 8 - CONV_W), (0, 0)))
        cb = vec(conv_b[i])
        lam_rows = jnp.pad(jnp.stack([lam_q1[i], lam_k1[i], lam_q2[i], lam_k2[i]]).astype(F32),
                           ((0, 4), (0, 128 - A_DK)))
        wbm, wba, wo = w_bm[i].astype(BF16), w_ba[i].astype(BF16), w_out[i].astype(BF16)

        hr = _ffn_ln(hr, *f1, vec(ln1_g[i]), vec(ln1_b[i]))
        hm = _ffn_ln(hm, *f1, vec(ln1_g[i]), vec(ln1_b[i]))

        qk_r, v_r, og_r, gt_r, aq_r, ak_r, av_r, gate_r = _proj(hr, w_all, bif, bg)
        qk_m, v_m, og_m, gt_m, aq_m, ak_m, av_m, gate_m = _proj(hm, w_all, bif, bg)
        qk_m = jnp.where(is_meta, qk_m, 0.0)
        gt_m = jnp.where(is_meta, gt_m, null_gate)

        xm_m, *state = _mlstm(qk_m, v_m, og_m, gt_m, cw, cb, vec(m_norm_g[i]), state0,
                              batch=1, chunk=META_ROWS)
        xm_r = _mlstm(qk_r, v_r, og_r, gt_r, cw, cb, vec(m_norm_g[i]), tuple(state),
                      batch=batch, chunk=min(MLSTM_CHUNK, seq))[0]

        xa_m = _attn_meta(aq_m, ak_m, av_m, lam_rows, slopes, vec(a_norm_g[i]), lam_init=lam_init)
        xa_r = _attn(aq_r, ak_r, av_r, ak_m, av_m, alibi_cols, lam_rows, slopes, vec(a_norm_g[i]),
                     batch=batch, lam_init=lam_init)

        tail_params = (wbm, wba, wo, vec(ln2_g[i]), vec(ln2_b[i]), *f2, vec(ln3_g[i]), vec(ln3_b[i]))
        hr = _merge_ffn(hr, xm_r, xa_r, gate_r, *tail_params)
        hm = _merge_ffn(hm, xm_m, xa_m, gate_m, *tail_params)

    return hr.reshape(batch, seq, D_MODEL)
```

```python
import functools
import math

import jax
import jax.numpy as jnp
import numpy as np
from jax import lax
from jax.experimental import pallas as pl
from jax.experimental.pallas import tpu as pltpu

F32 = jnp.float32
BF16 = jnp.bfloat16

D_MODEL = 1024
N_META = 16
META_ROWS = 128
M_HEADS = 4
M_DK = 128
M_WIDTH = 512
CONV_W = 4
A_HEADS = 4
A_DK = 64
A_DV = 128
A_WIDTH = 512
D_FF = 2816
FF_CHUNK = 256
N_FF_CHUNKS = D_FF // FF_CHUNK
DEPTH = 2
ALPHA = (2 * DEPTH) ** 0.25
LN_EPS = 1e-5
NEG = -1e30

SEG_QK = (0, 1024)
SEG_V = (1024, 1536)
SEG_O = (1536, 2048)
SEG_IF = (2048, 2176)
SEG_AQ = (2176, 2688)
SEG_AK = (2688, 3200)
SEG_AV = (3200, 3712)
SEG_G = (3712, 5760)
PROJ_COLS = 5760

ROW_TILE = 512
FFN_ROW_TILE = 1024
MLSTM_CHUNK = 256
MLSTM_GROUP = 8
ATTN_BLOCK = 512
ATTN_HEAD_GROUP = 4
LOG2E = math.log2(math.e)
VMEM_LIMIT = 56 * 1024 * 1024


def _nt_dot(a, b):
    return lax.dot_general(a, b, (((1,), (1,)), ((), ())), preferred_element_type=F32)


def _layer_norm(y, g, b):
    mu = jnp.mean(y, -1, keepdims=True)
    d = y - mu
    var = jnp.mean(d * d, -1, keepdims=True)
    return d * lax.rsqrt(var + LN_EPS) * g + b


def _swiglu_ln(x, wg_ref, wu_ref, wd_ref, g_ref, b_ref):
    xb = x.astype(BF16)
    acc = jnp.zeros(x.shape, F32)
    for j in range(N_FF_CHUNKS):
        cols = slice(j * FF_CHUNK, (j + 1) * FF_CHUNK)
        g = jnp.dot(xb, wg_ref[:, cols], preferred_element_type=F32)
        u = jnp.dot(xb, wu_ref[:, cols], preferred_element_type=F32)
        a = (g * jax.nn.sigmoid(g) * u).astype(BF16)
        acc = acc + jnp.dot(a, wd_ref[cols, :], preferred_element_type=F32)
    return _layer_norm(ALPHA * x + 0.5 * acc, g_ref[...], b_ref[...])


def _ffn_ln_kernel(x_ref, wg_ref, wu_ref, wd_ref, g_ref, b_ref, o_ref):
    half = x_ref.shape[0] // 2
    for rows in (slice(0, half), slice(half, 2 * half)):
        o_ref[rows, :] = _swiglu_ln(x_ref[rows, :], wg_ref, wu_ref, wd_ref, g_ref, b_ref)


def _resident(shape):
    return pl.BlockSpec(shape, lambda *_: (0,) * len(shape), pipeline_mode=pl.Buffered(1))


def _ffn_ln(x, w_gate, w_up, w_down, g, b):
    m = x.shape[0]
    tm = min(FFN_ROW_TILE, m)
    return pl.pallas_call(
        _ffn_ln_kernel,
        out_shape=jax.ShapeDtypeStruct((m, D_MODEL), F32),
        grid=(m // tm,),
        in_specs=[
            pl.BlockSpec((tm, D_MODEL), lambda i: (i, 0)),
            _resident((D_MODEL, D_FF)), _resident((D_MODEL, D_FF)), _resident((D_FF, D_MODEL)),
            _resident((1, D_MODEL)), _resident((1, D_MODEL)),
        ],
        out_specs=pl.BlockSpec((tm, D_MODEL), lambda i: (i, 0)),
        compiler_params=pltpu.CompilerParams(
            dimension_semantics=("arbitrary",), vmem_limit_bytes=VMEM_LIMIT),
    )(x, w_gate, w_up, w_down, g, b)


def _proj_kernel(h_ref, w_ref, bif_ref, bg_ref,
                 qk_ref, v_ref, og_ref, gt_ref, aq_ref, ak_ref, av_ref, gate_ref):
    hb = h_ref[...].astype(BF16)

    def seg(s):
        return jnp.dot(hb, w_ref[:, s[0]:s[1]], preferred_element_type=F32)

    gate_ref[...] = jax.nn.sigmoid(seg(SEG_G) + bg_ref[...]).astype(BF16)
    og_ref[...] = jax.nn.sigmoid(seg(SEG_O)).astype(BF16)
    z = seg(SEG_IF) + bif_ref[...]
    col = lax.broadcasted_iota(jnp.int32, z.shape, 1)
    log_f = jnp.minimum(z, 0.0) - jnp.log1p(jnp.exp(-jnp.abs(z)))
    gt_ref[...] = jnp.where(col < M_HEADS, z, jnp.where(col < 2 * M_HEADS, log_f, 0.0))
    aq_ref[...] = (seg(SEG_AQ) * (LOG2E * A_DK ** -0.5)).astype(BF16)
    qk_ref[...] = seg(SEG_QK)
    v_ref[...] = seg(SEG_V).astype(BF16)
    ak_ref[...] = seg(SEG_AK).astype(BF16)
    av_ref[...] = seg(SEG_AV).astype(BF16)


def _proj(h, w_all, b_if, b_gate):
    m = h.shape[0]
    tm = min(ROW_TILE, m)
    row = lambda n: pl.BlockSpec((tm, n), lambda i: (i, 0))
    const = lambda r, n: pl.BlockSpec((r, n), lambda i: (0, 0))
    out_shape = (
        jax.ShapeDtypeStruct((m, 1024), F32),
        jax.ShapeDtypeStruct((m, 512), BF16),
        jax.ShapeDtypeStruct((m, 512), BF16),
        jax.ShapeDtypeStruct((m, 128), F32),
        jax.ShapeDtypeStruct((m, 512), BF16),
        jax.ShapeDtypeStruct((m, 512), BF16),
        jax.ShapeDtypeStruct((m, 512), BF16),
        jax.ShapeDtypeStruct((m, 2048), BF16),
    )
    return pl.pallas_call(
        _proj_kernel,
        out_shape=out_shape,
        grid=(m // tm,),
        in_specs=[row(D_MODEL), const(D_MODEL, PROJ_COLS), const(1, 128), const(1, 2048)],
        out_specs=(row(1024), row(512), row(512), row(128), row(512), row(512), row(512), row(2048)),
        compiler_params=pltpu.CompilerParams(
            dimension_semantics=("arbitrary",), vmem_limit_bytes=VMEM_LIMIT),
    )(h, w_all, b_if, b_gate)


def _mlstm_kernel(qk_ref, v_ref, og_ref, gt_ref, cw_ref, cb_ref, ng_ref,
                  c0_ref, m0_ref, t0_ref,
                  h_ref, c_ref, m_ref, tail_ref, *, T, G):
    @pl.when(pl.program_id(1) == 0)
    def _():
        for g in range(G):
            c_ref[g] = c0_ref[0]
            m_ref[g] = m0_ref[0]
            tail_ref[g] = t0_ref[0]

    r = lax.broadcasted_iota(jnp.int32, (T, T), 0)
    c = lax.broadcasted_iota(jnp.int32, (T, T), 1)
    causal = c <= r
    eye = c == r
    tri = jnp.where(causal, 1.0, 0.0).astype(F32)
    ones = jnp.ones((T, M_DK), BF16)

    for g in range(G):
        u = jnp.concatenate([tail_ref[g], qk_ref[g]], axis=0)
        tail_ref[g] = u[T:T + 8, :]
        y = cb_ref[...] + cw_ref[CONV_W - 1:CONV_W, :] * u
        for j in range(1, CONV_W):
            y = y + cw_ref[CONV_W - 1 - j:CONV_W - j, :] * pltpu.roll(u, j, 0)
        y = y[8:8 + T, :]
        qk = y * jax.nn.sigmoid(y)

        gates = gt_ref[g]
        cums = jnp.dot(tri, gates, preferred_element_type=F32, precision=lax.Precision.HIGHEST)
        u_rows = (gates.T)[0:M_HEADS, :] - (cums.T)[M_HEADS:2 * M_HEADS, :]

        for h in range(M_HEADS):
            sl = slice(h * M_DK, (h + 1) * M_DK)
            qb = qk[:, sl].astype(BF16)
            k_h = qk[:, M_WIDTH + h * M_DK:M_WIDTH + (h + 1) * M_DK] * (M_DK ** -0.5)
            kb = k_h.astype(BF16)
            v_aug = jnp.concatenate([v_ref[g, :, sl], ones], axis=1)
            b_col = cums[:, M_HEADS + h:M_HEADS + h + 1]
            m_prev = m_ref[g, h:h + 1, 0:1]
            cn_prev = c_ref[g, h]

            um = jnp.where(causal, u_rows[h:h + 1, :], NEG)
            big_m = jnp.maximum(jnp.max(um, -1, keepdims=True), m_prev)
            p = jnp.exp(um - big_m)
            s = (_nt_dot(qb, kb) * p).astype(BF16)
            w_state = jnp.exp(m_prev - big_m)
            tot = (jnp.dot(s, v_aug, preferred_element_type=F32)
                   + w_state * jnp.dot(qb, cn_prev.astype(BF16), preferred_element_type=F32))
            num = tot[:, 0:M_DK]
            den = tot[:, M_DK:2 * M_DK]
            hh = num / jnp.maximum(jnp.abs(den), jnp.exp(-(b_col + big_m)))
            mu = jnp.mean(hh, -1, keepdims=True)
            d = hh - mu
            hn = d * lax.rsqrt(jnp.mean(d * d, -1, keepdims=True) + LN_EPS)
            h_ref[g, :, sl] = (hn * ng_ref[:, sl] * og_ref[g, :, sl].astype(F32)).astype(BF16)

            w_diag = jnp.where(eye, p[T - 1:T, :], 0.0).astype(BF16)
            vw = jnp.dot(w_diag, v_aug, preferred_element_type=F32)
            c_ref[g, h] = (w_state[T - 1:T, :] * cn_prev
                           + jnp.dot(k_h.T.astype(BF16), vw.astype(BF16), preferred_element_type=F32))
            m_ref[g, h:h + 1, :] = jnp.broadcast_to(b_col[T - 1:T, :] + big_m[T - 1:T, :], (1, 128))


def _mlstm(qk, v, og, gates, conv_w, conv_b, norm_g, state, *, batch, chunk):
    seq = qk.shape[0] // batch
    nc = seq // chunk
    group = min(MLSTM_GROUP, batch)
    c0, m0, t0 = state
    seq3 = lambda a: a.reshape(batch, seq, a.shape[-1])
    row = lambda n: pl.BlockSpec((group, chunk, n), lambda b, c: (b, c, 0))
    const2 = lambda r, n: pl.BlockSpec((r, n), lambda b, c: (0, 0))
    init3 = lambda r, n: pl.BlockSpec((1, r, n), lambda b, c: (0, 0, 0))
    out3 = lambda r, n: pl.BlockSpec((group, r, n), lambda b, c: (b, 0, 0))
    out_shape = (
        jax.ShapeDtypeStruct((batch, seq, M_WIDTH), BF16),
        jax.ShapeDtypeStruct((batch, M_HEADS, M_DK, 2 * M_DK), F32),
        jax.ShapeDtypeStruct((batch, 8, 128), F32),
        jax.ShapeDtypeStruct((batch, 8, 1024), F32),
    )
    out, c1, m1, t1 = pl.pallas_call(
        functools.partial(_mlstm_kernel, T=chunk, G=group),
        out_shape=out_shape,
        grid=(batch // group, nc),
        in_specs=[
            row(1024), row(512), row(512), row(128),
            const2(8, 1024), const2(1, 1024), const2(1, 512),
            pl.BlockSpec((1, M_HEADS, M_DK, 2 * M_DK), lambda b, c: (0, 0, 0, 0)),
            init3(8, 128), init3(8, 1024),
        ],
        out_specs=(
            row(512),
            pl.BlockSpec((group, M_HEADS, M_DK, 2 * M_DK), lambda b, c: (b, 0, 0, 0)),
            out3(8, 128), out3(8, 1024),
        ),
        compiler_params=pltpu.CompilerParams(
            dimension_semantics=("arbitrary", "arbitrary"), vmem_limit_bytes=VMEM_LIMIT),
    )(seq3(qk), seq3(v), seq3(og), seq3(gates), conv_w, conv_b, norm_g, c0, m0, t0)
    return out.reshape(batch * seq, M_WIDTH), c1, m1, t1


def _stack_maps(q):
    lane = lax.broadcasted_iota(jnp.int32, q.shape, 1)
    zero = jnp.zeros_like(q)
    return jnp.concatenate([jnp.where(lane < A_DK, q, zero), jnp.where(lane >= A_DK, q, zero)], axis=0)


def _diff_finish(acc, l, lam_ref, ng, lam_init, t):
    lam = (jnp.exp(jnp.sum(lam_ref[0:1, :] * lam_ref[1:2, :], -1, keepdims=True))
           - jnp.exp(jnp.sum(lam_ref[2:3, :] * lam_ref[3:4, :], -1, keepdims=True)) + lam_init)
    o = acc[:t] / l[:t] - lam * (acc[t:] / l[t:])
    o = o * lax.rsqrt(jnp.mean(o * o, -1, keepdims=True) + LN_EPS) * (1.0 - lam_init)
    return (o * ng).astype(BF16)


ACC_ROWS = A_DV + 16


def _fold_keys(s_t, offset, v_t, m_s, acc_s):
    keys, width = s_t.shape
    m_old = m_s[...]
    m_new = jnp.maximum(m_old, jnp.max(s_t, axis=0, keepdims=True) + offset)
    alpha = jnp.exp2(m_old - m_new)
    p3 = jnp.exp2(s_t.reshape(keys // 8, 8, width) - (m_new - offset)[None])
    p = p3.reshape(keys, width).astype(BF16)
    acc3 = acc_s[...].reshape(ACC_ROWS // 8, 8, width)
    acc_s[...] = ((acc3 * alpha[None]).reshape(acc_s.shape)
                  + jnp.dot(v_t, p, preferred_element_type=F32))
    m_s[...] = m_new


def _attn_kernel(q_ref, k_ref, v_ref, km_ref, vm_ref, ab_ref, lam_ref, sl_ref, ngt_ref, o_ref,
                 m_s, acc_s, *, T, HG, lam_init):
    i = pl.program_id(2)
    q0 = i * T
    R = 2 * T
    cols = [slice(h * 128, (h + 1) * 128) for h in range(HG)]
    slope = [sl_ref[:, h * 128:h * 128 + 1] for h in range(HG)]
    row = lax.broadcasted_iota(jnp.int32, (128, R), 0)
    ones_rows = jnp.where(row < 3, 1.0, 0.0).astype(BF16)
    qs_t = [jnp.concatenate([_stack_maps(q_ref[:, cols[h]]).astype(F32).T.astype(BF16), ones_rows], axis=0)
            for h in range(HG)]
    stats = [(m_s.at[h], acc_s.at[h]) for h in range(HG)]

    m_s[...] = jnp.full(m_s.shape, NEG, F32)
    acc_s[...] = jnp.zeros(acc_s.shape, F32)

    def rows(j):
        return pl.ds(pl.multiple_of(j * T, T), T)

    def values_t(v):
        v_t = v.astype(F32).T.astype(BF16)
        return jnp.concatenate([v_t, jnp.ones((ACC_ROWS - A_DV, v_t.shape[1]), BF16)], axis=0)

    zero_off = jnp.zeros((1, 1), F32)

    km_pos = lax.broadcasted_iota(jnp.int32, (META_ROWS, 128), 0)
    s_meta = []
    for h in range(HG):
        bias = jnp.where(km_pos >= META_ROWS - N_META,
                         slope[h] * (km_pos - (META_ROWS + q0)).astype(F32), NEG)
        s_meta.append(jnp.dot(km_ref[:, cols[h]], qs_t[h][0:128], preferred_element_type=F32)
                      + jnp.concatenate([bias] * (R // 128), axis=1))
    for h in range(HG):
        _fold_keys(s_meta[h], zero_off, values_t(vm_ref[:, cols[h]]), *stats[h])

    def logits(h, j):
        k_aug = jnp.concatenate([k_ref[rows(j), cols[h]], ab_ref[:, cols[h]]], axis=1)
        return jnp.dot(k_aug, qs_t[h], preferred_element_type=F32)

    def offset(h, j):
        return slope[h] * (j * T - q0).astype(F32)

    def fold_blocks(h, first, n):
        scores = [logits(h, first + k) for k in range(n)]
        for k in range(n):
            _fold_keys(scores[k], offset(h, first + k), values_t(v_ref[rows(first + k), cols[h]]), *stats[h])

    def logits_of_blocks(h, first, n):
        k_aug = jnp.concatenate(
            [jnp.concatenate([k_ref[rows(first + k), cols[h]], ab_ref[:, cols[h]]], axis=1) for k in range(n)], axis=0)
        s_all = jnp.dot(k_aug, qs_t[h], preferred_element_type=F32)
        return [s_all[k * T:(k + 1) * T] for k in range(n)]

    def fold_pair_of_heads(h0, first, n):
        scores = [logits_of_blocks(h, first, n) for h in (h0, h0 + 1)]
        for k in range(n):
            for d in (0, 1):
                _fold_keys(scores[d][k], offset(h0 + d, first + k),
                           values_t(v_ref[rows(first + k), cols[h0 + d]]), *stats[h0 + d])

    for h0 in range(0, HG, 2):
        lax.fori_loop(0, i // 4, lambda t, carry, h0=h0: fold_pair_of_heads(h0, 4 * t, 4) or carry, 0)

    @pl.when(i % 4 >= 2)
    def _():
        for h in range(HG):
            fold_blocks(h, i // 4 * 4, 2)

    @pl.when(i % 2 == 1)
    def _():
        scores = [logits(h, i - 1) for h in range(HG)]
        for h in range(HG):
            _fold_keys(scores[h], offset(h, i - 1), values_t(v_ref[rows(i - 1), cols[h]]), *stats[h])

    key = lax.broadcasted_iota(jnp.int32, (T, T), 0)
    qry = lax.broadcasted_iota(jnp.int32, (T, T), 1)
    keep = key <= qry
    keep2 = jnp.concatenate([keep, keep], axis=1)
    s_diag = [jnp.where(keep2, logits(h, i), NEG) for h in range(HG)]
    for h in range(HG):
        _fold_keys(s_diag[h], zero_off, values_t(v_ref[rows(i), cols[h]]), *stats[h])
    lam = (jnp.exp(jnp.sum(lam_ref[0:1, :] * lam_ref[1:2, :], -1, keepdims=True))
           - jnp.exp(jnp.sum(lam_ref[2:3, :] * lam_ref[3:4, :], -1, keepdims=True)) + lam_init)
    for h in range(HG):
        acc = acc_s[h, 0:A_DV, :]
        l = acc_s[h, A_DV:A_DV + 1, :]
        o_t = acc[:, 0:T] / l[:, 0:T] - lam * (acc[:, T:R] / l[:, T:R])
        o_t = o_t * lax.rsqrt(jnp.mean(o_t * o_t, axis=0, keepdims=True) + LN_EPS) * (1.0 - lam_init)
        o_t = jnp.concatenate([o_t[:, t * 128:(t + 1) * 128] * ngt_ref[cols[h], :] for t in range(T // 128)], axis=1)
        o_ref[:, cols[h]] = o_t.T.astype(BF16)


def _alibi_columns(slopes_log2, t):
    x = np.asarray(slopes_log2, np.float32)[None, :] * np.arange(t, dtype=np.float32)[:, None]
    hi = x.astype(BF16)
    mid = (x - hi.astype(np.float32)).astype(BF16)
    lo = (x - hi.astype(np.float32) - mid.astype(np.float32)).astype(BF16)
    cols = np.zeros((t, A_HEADS, 128), BF16)
    cols[:, :, 0], cols[:, :, 1], cols[:, :, 2] = hi, mid, lo
    return jnp.asarray(cols.reshape(t, A_HEADS * 128))


def _attn(aq, ak, av, ak_meta, av_meta, alibi_cols, lam_rows, slopes, norm_g, *, batch, lam_init):
    m = aq.shape[0]
    seq = m // batch
    t = alibi_cols.shape[0]
    nq = seq // t
    hg = ATTN_HEAD_GROUP
    w = hg * 128
    qspec = pl.BlockSpec((t, w), lambda b, h, i: (b * nq + i, h))
    kvspec = pl.BlockSpec((seq, w), lambda b, h, i: (b, h), pipeline_mode=pl.Buffered(1))
    mspec = pl.BlockSpec((META_ROWS, w), lambda b, h, i: (0, h))
    hrow = pl.BlockSpec((1, w), lambda b, h, i: (0, h))
    norm_g_t = jnp.broadcast_to(norm_g.reshape(A_WIDTH, 1), (A_WIDTH, 128))
    return pl.pallas_call(
        functools.partial(_attn_kernel, T=t, HG=hg, lam_init=lam_init),
        out_shape=jax.ShapeDtypeStruct((m, A_WIDTH), BF16),
        grid=(batch, A_HEADS // hg, nq),
        in_specs=[qspec, kvspec, kvspec, mspec, mspec,
                  pl.BlockSpec((t, w), lambda b, h, i: (0, h)),
                  pl.BlockSpec((8, 128), lambda b, h, i: (0, 0)), hrow,
                  pl.BlockSpec((w, 128), lambda b, h, i: (h, 0))],
        out_specs=qspec,
        scratch_shapes=[pltpu.VMEM((hg, 8, 2 * t), F32), pltpu.VMEM((hg, ACC_ROWS, 2 * t), F32)],
        compiler_params=pltpu.CompilerParams(
            dimension_semantics=("arbitrary", "arbitrary", "arbitrary"),
            vmem_limit_bytes=VMEM_LIMIT),
    )(aq, ak, av, ak_meta, av_meta, alibi_cols, lam_rows, slopes, norm_g_t)


def _attn_meta_kernel(q_ref, k_ref, v_ref, lam_ref, sl_ref, ng_ref, o_ref, *, lam_init):
    t = META_ROWS
    slope = sl_ref[:, 0:1]
    qs = _stack_maps(q_ref[...])
    r = lax.broadcasted_iota(jnp.int32, (t, t), 0)
    c = lax.broadcasted_iota(jnp.int32, (t, t), 1)
    bias = jnp.where((c >= t - N_META) & (c <= r), slope * (c - r).astype(F32), NEG)
    s = _nt_dot(qs, k_ref[...]) + jnp.concatenate([bias, bias], axis=0)
    p = jnp.exp2(s - jnp.max(s, -1, keepdims=True))
    l = jnp.sum(p, -1, keepdims=True)
    acc = jnp.dot(p.astype(BF16), v_ref[...], preferred_element_type=F32)
    o_ref[...] = _diff_finish(acc, l, lam_ref, ng_ref[...], lam_init, t)


def _attn_meta(aq, ak, av, lam_rows, slopes, norm_g, *, lam_init):
    blk = pl.BlockSpec((META_ROWS, 128), lambda h: (0, h))
    hrow = pl.BlockSpec((1, 128), lambda h: (0, h))
    return pl.pallas_call(
        functools.partial(_attn_meta_kernel, lam_init=lam_init),
        out_shape=jax.ShapeDtypeStruct((META_ROWS, A_WIDTH), BF16),
        grid=(A_HEADS,),
        in_specs=[blk, blk, blk, pl.BlockSpec((8, 128), lambda h: (0, 0)), hrow, hrow],
        out_specs=blk,
        compiler_params=pltpu.CompilerParams(dimension_semantics=("arbitrary",)),
    )(aq, ak, av, lam_rows, slopes, norm_g)


def _merge_ffn_kernel(h_ref, hm_ref, ha_ref, gate_ref, wbm_ref, wba_ref, wo_ref, g2_ref, b2_ref,
                      wg_ref, wu_ref, wd_ref, g3_ref, b3_ref, o_ref):
    ym = jnp.dot(hm_ref[...], wbm_ref[...], preferred_element_type=F32)
    ya = jnp.dot(ha_ref[...], wba_ref[...], preferred_element_type=F32)
    gm = gate_ref[:, 0:D_MODEL].astype(F32)
    ga = gate_ref[:, D_MODEL:2 * D_MODEL].astype(F32)
    mixed = (gm * ym + ga * ya).astype(BF16)
    mix = jnp.dot(mixed, wo_ref[...], preferred_element_type=F32)
    h2 = _layer_norm(ALPHA * h_ref[...] + mix, g2_ref[...], b2_ref[...])
    o_ref[...] = _swiglu_ln(h2, wg_ref, wu_ref, wd_ref, g3_ref, b3_ref)


def _merge_ffn(h, hm, ha, gate, w_bm, w_ba, w_out, g2, b2, w_gate, w_up, w_down, g3, b3):
    m = h.shape[0]
    tm = min(ROW_TILE, m)
    row = lambda n: pl.BlockSpec((tm, n), lambda i: (i, 0))
    return pl.pallas_call(
        _merge_ffn_kernel,
        out_shape=jax.ShapeDtypeStruct((m, D_MODEL), F32),
        grid=(m // tm,),
        in_specs=[row(D_MODEL), row(512), row(512), row(2048),
                  _resident((512, D_MODEL)), _resident((512, D_MODEL)), _resident((D_MODEL, D_MODEL)),
                  _resident((1, D_MODEL)), _resident((1, D_MODEL)),
                  _resident((D_MODEL, D_FF)), _resident((D_MODEL, D_FF)), _resident((D_FF, D_MODEL)),
                  _resident((1, D_MODEL)), _resident((1, D_MODEL))],
        out_specs=row(D_MODEL),
        compiler_params=pltpu.CompilerParams(
            dimension_semantics=("arbitrary",), vmem_limit_bytes=VMEM_LIMIT),
    )(h, hm, ha, gate, w_bm, w_ba, w_out, g2, b2, w_gate, w_up, w_down, g3, b3)


def _ffn_weights(w_gate, w_up, w_down):
    return w_gate.astype(BF16), w_up.astype(BF16), w_down.astype(BF16)


def _proj_weights(w_in, b_if, b_gate):
    w = w_in.astype(BF16)
    w_all = jnp.concatenate(
        [w[:, :2048], jnp.pad(w[:, 2048:2056], ((0, 0), (0, 120))), w[:, 2056:]], axis=1)
    return w_all, jnp.pad(b_if, (0, 120)).reshape(1, 128), b_gate.reshape(1, 2 * D_MODEL)


def kernel(x, meta, ffn1_w_gate, ffn1_w_up, ffn1_w_down, ln1_g, ln1_b, w_in, conv_w, conv_b, b_if, m_norm_g, lam_q1, lam_k1, lam_q2, lam_k2, a_norm_g, w_bm, w_ba, b_gate, w_out, ln2_g, ln2_b, ffn2_w_gate, ffn2_w_up, ffn2_w_down, ln3_g, ln3_b):
    batch, seq, _ = x.shape
    rows = batch * seq
    hr = x.reshape(rows, D_MODEL)
    hm = jnp.pad(meta.astype(x.dtype), ((META_ROWS - N_META, 0), (0, 0)))

    slopes_log2 = np.array([LOG2E * 2.0 ** (-8.0 * (h + 1) / A_HEADS) for h in range(A_HEADS)], np.float32)
    slopes = jnp.asarray(np.repeat(slopes_log2, 128).reshape(1, 512))
    alibi_cols = _alibi_columns(slopes_log2, min(ATTN_BLOCK, seq))
    is_meta = (jnp.arange(META_ROWS) >= META_ROWS - N_META)[:, None]
    null_gate = jnp.where(jnp.arange(128) < M_HEADS, NEG, 0.0).astype(F32)[None, :]
    state0 = (jnp.zeros((1, M_HEADS, M_DK, 2 * M_DK), F32), jnp.full((1, 8, 128), NEG, F32),
              jnp.zeros((1, 8, 1024), F32))
    vec = lambda a: a.reshape(1, -1)

    for i in range(DEPTH):
        lam_init = 0.8 - 0.6 * math.exp(-0.3 * i)
        f1 = _ffn_weights(ffn1_w_gate[i], ffn1_w_up[i], ffn1_w_down[i])
        f2 = _ffn_weights(ffn2_w_gate[i], ffn2_w_up[i], ffn2_w_down[i])
        w_all, bif, bg = _proj_weights(w_in[i], b_if[i], b_gate[i])
        cw = jnp.pad(conv_w[i], ((0, 8 - CONV_W), (0, 0)))
        cb = vec(conv_b[i])
        lam_rows = jnp.pad(jnp.stack([lam_q1[i], lam_k1[i], lam_q2[i], lam_k2[i]]).astype(F32),
                           ((0, 4), (0, 128 - A_DK)))
        wbm, wba, wo = w_bm[i].astype(BF16), w_ba[i].astype(BF16), w_out[i].astype(BF16)

        hr = _ffn_ln(hr, *f1, vec(ln1_g[i]), vec(ln1_b[i]))
        hm = _ffn_ln(hm, *f1, vec(ln1_g[i]), vec(ln1_b[i]))

        qk_r, v_r, og_r, gt_r, aq_r, ak_r, av_r, gate_r = _proj(hr, w_all, bif, bg)
        qk_m, v_m, og_m, gt_m, aq_m, ak_m, av_m, gate_m = _proj(hm, w_all, bif, bg)
        qk_m = jnp.where(is_meta, qk_m, 0.0)
        gt_m = jnp.where(is_meta, gt_m, null_gate)

        xm_m, *state = _mlstm(qk_m, v_m, og_m, gt_m, cw, cb, vec(m_norm_g[i]), state0,
                              batch=1, chunk=META_ROWS)
        xm_r = _mlstm(qk_r, v_r, og_r, gt_r, cw, cb, vec(m_norm_g[i]), tuple(state),
                      batch=batch, chunk=min(MLSTM_CHUNK, seq))[0]

        xa_m = _attn_meta(aq_m, ak_m, av_m, lam_rows, slopes, vec(a_norm_g[i]), lam_init=lam_init)
        xa_r = _attn(aq_r, ak_r, av_r, ak_m, av_m, alibi_cols, lam_rows, slopes, vec(a_norm_g[i]),
                     batch=batch, lam_init=lam_init)

        tail_params = (wbm, wba, wo, vec(ln2_g[i]), vec(ln2_b[i]), *f2, vec(ln3_g[i]), vec(ln3_b[i]))
        hr = _merge_ffn(hr, xm_r, xa_r, gate_r, *tail_params)
        hm = _merge_ffn(hm, xm_m, xa_m, gate_m, *tail_params)

    return hr.reshape(batch, seq, D_MODEL)
```

```python
import functools
import math

import jax
import jax.numpy as jnp
import numpy as np
from jax import lax
from jax.experimental import pallas as pl
from jax.experimental.pallas import tpu as pltpu

F32 = jnp.float32
BF16 = jnp.bfloat16

D_MODEL = 1024
N_META = 16
META_ROWS = 128
M_HEADS = 4
M_DK = 128
M_WIDTH = 512
CONV_W = 4
A_HEADS = 4
A_DK = 64
A_DV = 128
A_WIDTH = 512
D_FF = 2816
FF_CHUNK = 256
N_FF_CHUNKS = D_FF // FF_CHUNK
DEPTH = 2
ALPHA = (2 * DEPTH) ** 0.25
LN_EPS = 1e-5
NEG = -1e30

SEG_QK = (0, 1024)
SEG_V = (1024, 1536)
SEG_O = (1536, 2048)
SEG_IF = (2048, 2176)
SEG_AQ = (2176, 2688)
SEG_AK = (2688, 3200)
SEG_AV = (3200, 3712)
SEG_G = (3712, 5760)
PROJ_COLS = 5760

ROW_TILE = 512
FFN_ROW_TILE = 1024
MLSTM_CHUNK = 256
MLSTM_GROUP = 8
ATTN_BLOCK = 256
ATTN_HEAD_GROUP = 4
LOG2E = math.log2(math.e)
VMEM_LIMIT = 56 * 1024 * 1024


def _nt_dot(a, b):
    return lax.dot_general(a, b, (((1,), (1,)), ((), ())), preferred_element_type=F32)


def _layer_norm(y, g, b):
    mu = jnp.mean(y, -1, keepdims=True)
    d = y - mu
    var = jnp.mean(d * d, -1, keepdims=True)
    return d * lax.rsqrt(var + LN_EPS) * g + b


def _swiglu_ln(x, wg_ref, wu_ref, wd_ref, g_ref, b_ref):
    xb = x.astype(BF16)
    acc = jnp.zeros(x.shape, F32)
    for j in range(N_FF_CHUNKS):
        cols = slice(j * FF_CHUNK, (j + 1) * FF_CHUNK)
        g = jnp.dot(xb, wg_ref[:, cols], preferred_element_type=F32)
        u = jnp.dot(xb, wu_ref[:, cols], preferred_element_type=F32)
        a = (g * jax.nn.sigmoid(g) * u).astype(BF16)
        acc = acc + jnp.dot(a, wd_ref[cols, :], preferred_element_type=F32)
    return _layer_norm(ALPHA * x + 0.5 * acc, g_ref[...], b_ref[...])


def _ffn_ln_kernel(x_ref, wg_ref, wu_ref, wd_ref, g_ref, b_ref, o_ref):
    half = x_ref.shape[0] // 2
    for rows in (slice(0, half), slice(half, 2 * half)):
        o_ref[rows, :] = _swiglu_ln(x_ref[rows, :], wg_ref, wu_ref, wd_ref, g_ref, b_ref)


def _resident(shape):
    return pl.BlockSpec(shape, lambda *_: (0,) * len(shape), pipeline_mode=pl.Buffered(1))


def _ffn_ln(x, w_gate, w_up, w_down, g, b):
    m = x.shape[0]
    tm = min(FFN_ROW_TILE, m)
    return pl.pallas_call(
        _ffn_ln_kernel,
        out_shape=jax.ShapeDtypeStruct((m, D_MODEL), F32),
        grid=(m // tm,),
        in_specs=[
            pl.BlockSpec((tm, D_MODEL), lambda i: (i, 0)),
            _resident((D_MODEL, D_FF)), _resident((D_MODEL, D_FF)), _resident((D_FF, D_MODEL)),
            _resident((1, D_MODEL)), _resident((1, D_MODEL)),
        ],
        out_specs=pl.BlockSpec((tm, D_MODEL), lambda i: (i, 0)),
        compiler_params=pltpu.CompilerParams(
            dimension_semantics=("arbitrary",), vmem_limit_bytes=VMEM_LIMIT),
    )(x, w_gate, w_up, w_down, g, b)


def _proj_kernel(h_ref, w_ref, bif_ref, bg_ref,
                 qk_ref, v_ref, og_ref, gt_ref, aq_ref, ak_ref, av_ref, gate_ref):
    hb = h_ref[...].astype(BF16)

    def seg(s):
        return jnp.dot(hb, w_ref[:, s[0]:s[1]], preferred_element_type=F32)

    gate_ref[...] = jax.nn.sigmoid(seg(SEG_G) + bg_ref[...]).astype(BF16)
    og_ref[...] = jax.nn.sigmoid(seg(SEG_O)).astype(BF16)
    z = seg(SEG_IF) + bif_ref[...]
    col = lax.broadcasted_iota(jnp.int32, z.shape, 1)
    log_f = jnp.minimum(z, 0.0) - jnp.log1p(jnp.exp(-jnp.abs(z)))
    gt_ref[...] = jnp.where(col < M_HEADS, z, jnp.where(col < 2 * M_HEADS, log_f, 0.0))
    aq_ref[...] = (seg(SEG_AQ) * (LOG2E * A_DK ** -0.5)).astype(BF16)
    qk_ref[...] = seg(SEG_QK)
    v_ref[...] = seg(SEG_V).astype(BF16)
    ak_ref[...] = seg(SEG_AK).astype(BF16)
    av_ref[...] = seg(SEG_AV).astype(BF16)


def _proj(h, w_all, b_if, b_gate):
    m = h.shape[0]
    tm = min(ROW_TILE, m)
    row = lambda n: pl.BlockSpec((tm, n), lambda i: (i, 0))
    const = lambda r, n: pl.BlockSpec((r, n), lambda i: (0, 0))
    out_shape = (
        jax.ShapeDtypeStruct((m, 1024), F32),
        jax.ShapeDtypeStruct((m, 512), BF16),
        jax.ShapeDtypeStruct((m, 512), BF16),
        jax.ShapeDtypeStruct((m, 128), F32),
        jax.ShapeDtypeStruct((m, 512), BF16),
        jax.ShapeDtypeStruct((m, 512), BF16),
        jax.ShapeDtypeStruct((m, 512), BF16),
        jax.ShapeDtypeStruct((m, 2048), BF16),
    )
    return pl.pallas_call(
        _proj_kernel,
        out_shape=out_shape,
        grid=(m // tm,),
        in_specs=[row(D_MODEL), const(D_MODEL, PROJ_COLS), const(1, 128), const(1, 2048)],
        out_specs=(row(1024), row(512), row(512), row(128), row(512), row(512), row(512), row(2048)),
        compiler_params=pltpu.CompilerParams(
            dimension_semantics=("arbitrary",), vmem_limit_bytes=VMEM_LIMIT),
    )(h, w_all, b_if, b_gate)


def _mlstm_kernel(qk_ref, v_ref, og_ref, gt_ref, cw_ref, cb_ref, ng_ref,
                  c0_ref, m0_ref, t0_ref,
                  h_ref, c_ref, m_ref, tail_ref, *, T, G):
    @pl.when(pl.program_id(1) == 0)
    def _():
        for g in range(G):
            c_ref[g] = c0_ref[0]
            m_ref[g] = m0_ref[0]
            tail_ref[g] = t0_ref[0]

    r = lax.broadcasted_iota(jnp.int32, (T, T), 0)
    c = lax.broadcasted_iota(jnp.int32, (T, T), 1)
    causal = c <= r
    eye = c == r
    tri = jnp.where(causal, 1.0, 0.0).astype(F32)
    ones = jnp.ones((T, M_DK), BF16)

    for g in range(G):
        u = jnp.concatenate([tail_ref[g], qk_ref[g]], axis=0)
        tail_ref[g] = u[T:T + 8, :]
        y = cb_ref[...] + cw_ref[CONV_W - 1:CONV_W, :] * u
        for j in range(1, CONV_W):
            y = y + cw_ref[CONV_W - 1 - j:CONV_W - j, :] * pltpu.roll(u, j, 0)
        y = y[8:8 + T, :]
        qk = y * jax.nn.sigmoid(y)

        gates = gt_ref[g]
        cums = jnp.dot(tri, gates, preferred_element_type=F32, precision=lax.Precision.HIGHEST)
        u_rows = (gates.T)[0:M_HEADS, :] - (cums.T)[M_HEADS:2 * M_HEADS, :]

        for h in range(M_HEADS):
            sl = slice(h * M_DK, (h + 1) * M_DK)
            qb = qk[:, sl].astype(BF16)
            k_h = qk[:, M_WIDTH + h * M_DK:M_WIDTH + (h + 1) * M_DK] * (M_DK ** -0.5)
            kb = k_h.astype(BF16)
            v_aug = jnp.concatenate([v_ref[g, :, sl], ones], axis=1)
            b_col = cums[:, M_HEADS + h:M_HEADS + h + 1]
            m_prev = m_ref[g, h:h + 1, 0:1]
            cn_prev = c_ref[g, h]

            um = jnp.where(causal, u_rows[h:h + 1, :], NEG)
            big_m = jnp.maximum(jnp.max(um, -1, keepdims=True), m_prev)
            p = jnp.exp(um - big_m)
            s = (_nt_dot(qb, kb) * p).astype(BF16)
            w_state = jnp.exp(m_prev - big_m)
            tot = (jnp.dot(s, v_aug, preferred_element_type=F32)
                   + w_state * jnp.dot(qb, cn_prev.astype(BF16), preferred_element_type=F32))
            num = tot[:, 0:M_DK]
            den = tot[:, M_DK:2 * M_DK]
            hh = num / jnp.maximum(jnp.abs(den), jnp.exp(-(b_col + big_m)))
            mu = jnp.mean(hh, -1, keepdims=True)
            d = hh - mu
            hn = d * lax.rsqrt(jnp.mean(d * d, -1, keepdims=True) + LN_EPS)
            h_ref[g, :, sl] = (hn * ng_ref[:, sl] * og_ref[g, :, sl].astype(F32)).astype(BF16)

            w_diag = jnp.where(eye, p[T - 1:T, :], 0.0).astype(BF16)
            vw = jnp.dot(w_diag, v_aug, preferred_element_type=F32)
            c_ref[g, h] = (w_state[T - 1:T, :] * cn_prev
                           + jnp.dot(k_h.T.astype(BF16), vw.astype(BF16), preferred_element_type=F32))
            m_ref[g, h:h + 1, :] = jnp.broadcast_to(b_col[T - 1:T, :] + big_m[T - 1:T, :], (1, 128))


def _mlstm(qk, v, og, gates, conv_w, conv_b, norm_g, state, *, batch, chunk):
    seq = qk.shape[0] // batch
    nc = seq // chunk
    group = min(MLSTM_GROUP, batch)
    c0, m0, t0 = state
    seq3 = lambda a: a.reshape(batch, seq, a.shape[-1])
    row = lambda n: pl.BlockSpec((group, chunk, n), lambda b, c: (b, c, 0))
    const2 = lambda r, n: pl.BlockSpec((r, n), lambda b, c: (0, 0))
    init3 = lambda r, n: pl.BlockSpec((1, r, n), lambda b, c: (0, 0, 0))
    out3 = lambda r, n: pl.BlockSpec((group, r, n), lambda b, c: (b, 0, 0))
    out_shape = (
        jax.ShapeDtypeStruct((batch, seq, M_WIDTH), BF16),
        jax.ShapeDtypeStruct((batch, M_HEADS, M_DK, 2 * M_DK), F32),
        jax.ShapeDtypeStruct((batch, 8, 128), F32),
        jax.ShapeDtypeStruct((batch, 8, 1024), F32),
    )
    out, c1, m1, t1 = pl.pallas_call(
        functools.partial(_mlstm_kernel, T=chunk, G=group),
        out_shape=out_shape,
        grid=(batch // group, nc),
        in_specs=[
            row(1024), row(512), row(512), row(128),
            const2(8, 1024), const2(1, 1024), const2(1, 512),
            pl.BlockSpec((1, M_HEADS, M_DK, 2 * M_DK), lambda b, c: (0, 0, 0, 0)),
            init3(8, 128), init3(8, 1024),
        ],
        out_specs=(
            row(512),
            pl.BlockSpec((group, M_HEADS, M_DK, 2 * M_DK), lambda b, c: (b, 0, 0, 0)),
            out3(8, 128), out3(8, 1024),
        ),
        compiler_params=pltpu.CompilerParams(
            dimension_semantics=("arbitrary", "arbitrary"), vmem_limit_bytes=VMEM_LIMIT),
    )(seq3(qk), seq3(v), seq3(og), seq3(gates), conv_w, conv_b, norm_g, c0, m0, t0)
    return out.reshape(batch * seq, M_WIDTH), c1, m1, t1


def _stack_maps(q):
    lane = lax.broadcasted_iota(jnp.int32, q.shape, 1)
    zero = jnp.zeros_like(q)
    return jnp.concatenate([jnp.where(lane < A_DK, q, zero), jnp.where(lane >= A_DK, q, zero)], axis=0)


def _diff_finish(acc, l, lam_ref, ng, lam_init, t):
    lam = (jnp.exp(jnp.sum(lam_ref[0:1, :] * lam_ref[1:2, :], -1, keepdims=True))
           - jnp.exp(jnp.sum(lam_ref[2:3, :] * lam_ref[3:4, :], -1, keepdims=True)) + lam_init)
    o = acc[:t] / l[:t] - lam * (acc[t:] / l[t:])
    o = o * lax.rsqrt(jnp.mean(o * o, -1, keepdims=True) + LN_EPS) * (1.0 - lam_init)
    return (o * ng).astype(BF16)


ACC_ROWS = A_DV + 16


def _fold_keys(s_t, offset, v_t, m_s, acc_s):
    keys, width = s_t.shape
    m_old = m_s[...]
    m_new = jnp.maximum(m_old, jnp.max(s_t, axis=0, keepdims=True) + offset)
    alpha = jnp.exp2(m_old - m_new)
    p3 = jnp.exp2(s_t.reshape(keys // 8, 8, width) - (m_new - offset)[None])
    p = p3.reshape(keys, width).astype(BF16)
    acc3 = acc_s[...].reshape(ACC_ROWS // 8, 8, width)
    acc_s[...] = ((acc3 * alpha[None]).reshape(acc_s.shape)
                  + jnp.dot(v_t, p, preferred_element_type=F32))
    m_s[...] = m_new


def _attn_kernel(q_ref, k_ref, v_ref, km_ref, vm_ref, ab_ref, lam_ref, sl_ref, ngt_ref, o_ref,
                 m_s, acc_s, *, T, HG, lam_init):
    i = pl.program_id(2)
    q0 = i * T
    R = 2 * T
    cols = [slice(h * 128, (h + 1) * 128) for h in range(HG)]
    slope = [sl_ref[:, h * 128:h * 128 + 1] for h in range(HG)]
    row = lax.broadcasted_iota(jnp.int32, (128, R), 0)
    ones_rows = jnp.where(row < 3, 1.0, 0.0).astype(BF16)
    qs_t = [jnp.concatenate([_stack_maps(q_ref[:, cols[h]]).astype(F32).T.astype(BF16), ones_rows], axis=0)
            for h in range(HG)]
    stats = [(m_s.at[h], acc_s.at[h]) for h in range(HG)]

    m_s[...] = jnp.full(m_s.shape, NEG, F32)
    acc_s[...] = jnp.zeros(acc_s.shape, F32)

    def rows(j):
        return pl.ds(pl.multiple_of(j * T, T), T)

    def values_t(v):
        v_t = v.astype(F32).T.astype(BF16)
        return jnp.concatenate([v_t, jnp.ones((ACC_ROWS - A_DV, v_t.shape[1]), BF16)], axis=0)

    zero_off = jnp.zeros((1, 1), F32)

    km_pos = lax.broadcasted_iota(jnp.int32, (META_ROWS, 128), 0)
    s_meta = []
    for h in range(HG):
        bias = jnp.where(km_pos >= META_ROWS - N_META,
                         slope[h] * (km_pos - (META_ROWS + q0)).astype(F32), NEG)
        s_meta.append(jnp.dot(km_ref[:, cols[h]], qs_t[h][0:128], preferred_element_type=F32)
                      + jnp.concatenate([bias] * (R // 128), axis=1))
    for h in range(HG):
        _fold_keys(s_meta[h], zero_off, values_t(vm_ref[:, cols[h]]), *stats[h])

    def logits(h, j):
        k_aug = jnp.concatenate([k_ref[rows(j), cols[h]], ab_ref[:, cols[h]]], axis=1)
        return jnp.dot(k_aug, qs_t[h], preferred_element_type=F32)

    def offset(h, j):
        return slope[h] * (j * T - q0).astype(F32)

    def fold_blocks(h, first, n):
        scores = [logits(h, first + k) for k in range(n)]
        for k in range(n):
            _fold_keys(scores[k], offset(h, first + k), values_t(v_ref[rows(first + k), cols[h]]), *stats[h])

    def fold_pair_of_heads(h0, first, n):
        scores = [[logits(h, first + k) for k in range(n)] for h in (h0, h0 + 1)]
        for k in range(n):
            for d in (0, 1):
                _fold_keys(scores[d][k], offset(h0 + d, first + k),
                           values_t(v_ref[rows(first + k), cols[h0 + d]]), *stats[h0 + d])

    for h0 in range(0, HG, 2):
        lax.fori_loop(0, i // 4, lambda t, carry, h0=h0: fold_pair_of_heads(h0, 4 * t, 4) or carry, 0)

    @pl.when(i % 4 >= 2)
    def _():
        for h in range(HG):
            fold_blocks(h, i // 4 * 4, 2)

    @pl.when(i % 2 == 1)
    def _():
        scores = [logits(h, i - 1) for h in range(HG)]
        for h in range(HG):
            _fold_keys(scores[h], offset(h, i - 1), values_t(v_ref[rows(i - 1), cols[h]]), *stats[h])

    key = lax.broadcasted_iota(jnp.int32, (T, T), 0)
    qry = lax.broadcasted_iota(jnp.int32, (T, T), 1)
    keep = key <= qry
    keep2 = jnp.concatenate([keep, keep], axis=1)
    s_diag = [jnp.where(keep2, logits(h, i), NEG) for h in range(HG)]
    for h in range(HG):
        _fold_keys(s_diag[h], zero_off, values_t(v_ref[rows(i), cols[h]]), *stats[h])
    lam = (jnp.exp(jnp.sum(lam_ref[0:1, :] * lam_ref[1:2, :], -1, keepdims=True))
           - jnp.exp(jnp.sum(lam_ref[2:3, :] * lam_ref[3:4, :], -1, keepdims=True)) + lam_init)
    for h in range(HG):
        acc = acc_s[h, 0:A_DV, :]
        l = acc_s[h, A_DV:A_DV + 1, :]
        o_t = acc[:, 0:T] / l[:, 0:T] - lam * (acc[:, T:R] / l[:, T:R])
        o_t = o_t * lax.rsqrt(jnp.mean(o_t * o_t, axis=0, keepdims=True) + LN_EPS) * (1.0 - lam_init)
        o_t = jnp.concatenate([o_t[:, t * 128:(t + 1) * 128] * ngt_ref[cols[h], :] for t in range(T // 128)], axis=1)
        o_ref[:, cols[h]] = o_t.T.astype(BF16)


def _alibi_columns(slopes_log2, t):
    x = np.asarray(slopes_log2, np.float32)[None, :] * np.arange(t, dtype=np.float32)[:, None]
    hi = x.astype(BF16)
    mid = (x - hi.astype(np.float32)).astype(BF16)
    lo = (x - hi.astype(np.float32) - mid.astype(np.float32)).astype(BF16)
    cols = np.zeros((t, A_HEADS, 128), BF16)
    cols[:, :, 0], cols[:, :, 1], cols[:, :, 2] = hi, mid, lo
    return jnp.asarray(cols.reshape(t, A_HEADS * 128))


def _attn(aq, ak, av, ak_meta, av_meta, alibi_cols, lam_rows, slopes, norm_g, *, batch, lam_init):
    m = aq.shape[0]
    seq = m // batch
    t = alibi_cols.shape[0]
    nq = seq // t
    hg = ATTN_HEAD_GROUP
    w = hg * 128
    qspec = pl.BlockSpec((t, w), lambda b, h, i: (b * nq + i, h))
    kvspec = pl.BlockSpec((seq, w), lambda b, h, i: (b, h), pipeline_mode=pl.Buffered(1))
    mspec = pl.BlockSpec((META_ROWS, w), lambda b, h, i: (0, h))
    hrow = pl.BlockSpec((1, w), lambda b, h, i: (0, h))
    norm_g_t = jnp.broadcast_to(norm_g.reshape(A_WIDTH, 1), (A_WIDTH, 128))
    return pl.pallas_call(
        functools.partial(_attn_kernel, T=t, HG=hg, lam_init=lam_init),
        out_shape=jax.ShapeDtypeStruct((m, A_WIDTH), BF16),
        grid=(batch, A_HEADS // hg, nq),
        in_specs=[qspec, kvspec, kvspec, mspec, mspec,
                  pl.BlockSpec((t, w), lambda b, h, i: (0, h)),
                  pl.BlockSpec((8, 128), lambda b, h, i: (0, 0)), hrow,
                  pl.BlockSpec((w, 128), lambda b, h, i: (h, 0))],
        out_specs=qspec,
        scratch_shapes=[pltpu.VMEM((hg, 8, 2 * t), F32), pltpu.VMEM((hg, ACC_ROWS, 2 * t), F32)],
        compiler_params=pltpu.CompilerParams(
            dimension_semantics=("arbitrary", "arbitrary", "arbitrary"),
            vmem_limit_bytes=VMEM_LIMIT),
    )(aq, ak, av, ak_meta, av_meta, alibi_cols, lam_rows, slopes, norm_g_t)


def _attn_meta_kernel(q_ref, k_ref, v_ref, lam_ref, sl_ref, ng_ref, o_ref, *, lam_init):
    t = META_ROWS
    slope = sl_ref[:, 0:1]
    qs = _stack_maps(q_ref[...])
    r = lax.broadcasted_iota(jnp.int32, (t, t), 0)
    c = lax.broadcasted_iota(jnp.int32, (t, t), 1)
    bias = jnp.where((c >= t - N_META) & (c <= r), slope * (c - r).astype(F32), NEG)
    s = _nt_dot(qs, k_ref[...]) + jnp.concatenate([bias, bias], axis=0)
    p = jnp.exp2(s - jnp.max(s, -1, keepdims=True))
    l = jnp.sum(p, -1, keepdims=True)
    acc = jnp.dot(p.astype(BF16), v_ref[...], preferred_element_type=F32)
    o_ref[...] = _diff_finish(acc, l, lam_ref, ng_ref[...], lam_init, t)


def _attn_meta(aq, ak, av, lam_rows, slopes, norm_g, *, lam_init):
    blk = pl.BlockSpec((META_ROWS, 128), lambda h: (0, h))
    hrow = pl.BlockSpec((1, 128), lambda h: (0, h))
    return pl.pallas_call(
        functools.partial(_attn_meta_kernel, lam_init=lam_init),
        out_shape=jax.ShapeDtypeStruct((META_ROWS, A_WIDTH), BF16),
        grid=(A_HEADS,),
        in_specs=[blk, blk, blk, pl.BlockSpec((8, 128), lambda h: (0, 0)), hrow, hrow],
        out_specs=blk,
        compiler_params=pltpu.CompilerParams(dimension_semantics=("arbitrary",)),
    )(aq, ak, av, lam_rows, slopes, norm_g)


def _merge_ffn_kernel(h_ref, hm_ref, ha_ref, gate_ref, wbm_ref, wba_ref, wo_ref, g2_ref, b2_ref,
                      wg_ref, wu_ref, wd_ref, g3_ref, b3_ref, o_ref):
    ym = jnp.dot(hm_ref[...], wbm_ref[...], preferred_element_type=F32)
    ya = jnp.dot(ha_ref[...], wba_ref[...], preferred_element_type=F32)
    gm = gate_ref[:, 0:D_MODEL].astype(F32)
    ga = gate_ref[:, D_MODEL:2 * D_MODEL].astype(F32)
    mixed = (gm * ym + ga * ya).astype(BF16)
    mix = jnp.dot(mixed, wo_ref[...], preferred_element_type=F32)
    h2 = _layer_norm(ALPHA * h_ref[...] + mix, g2_ref[...], b2_ref[...])
    o_ref[...] = _swiglu_ln(h2, wg_ref, wu_ref, wd_ref, g3_ref, b3_ref)


def _merge_ffn(h, hm, ha, gate, w_bm, w_ba, w_out, g2, b2, w_gate, w_up, w_down, g3, b3):
    m = h.shape[0]
    tm = min(ROW_TILE, m)
    row = lambda n: pl.BlockSpec((tm, n), lambda i: (i, 0))
    return pl.pallas_call(
        _merge_ffn_kernel,
        out_shape=jax.ShapeDtypeStruct((m, D_MODEL), F32),
        grid=(m // tm,),
        in_specs=[row(D_MODEL), row(512), row(512), row(2048),
                  _resident((512, D_MODEL)), _resident((512, D_MODEL)), _resident((D_MODEL, D_MODEL)),
                  _resident((1, D_MODEL)), _resident((1, D_MODEL)),
                  _resident((D_MODEL, D_FF)), _resident((D_MODEL, D_FF)), _resident((D_FF, D_MODEL)),
                  _resident((1, D_MODEL)), _resident((1, D_MODEL))],
        out_specs=row(D_MODEL),
        compiler_params=pltpu.CompilerParams(
            dimension_semantics=("arbitrary",), vmem_limit_bytes=VMEM_LIMIT),
    )(h, hm, ha, gate, w_bm, w_ba, w_out, g2, b2, w_gate, w_up, w_down, g3, b3)


def _ffn_weights(w_gate, w_up, w_down):
    return w_gate.astype(BF16), w_up.astype(BF16), w_down.astype(BF16)


def _proj_weights(w_in, b_if, b_gate):
    w = w_in.astype(BF16)
    w_all = jnp.concatenate(
        [w[:, :2048], jnp.pad(w[:, 2048:2056], ((0, 0), (0, 120))), w[:, 2056:]], axis=1)
    return w_all, jnp.pad(b_if, (0, 120)).reshape(1, 128), b_gate.reshape(1, 2 * D_MODEL)


def kernel(x, meta, ffn1_w_gate, ffn1_w_up, ffn1_w_down, ln1_g, ln1_b, w_in, conv_w, conv_b, b_if, m_norm_g, lam_q1, lam_k1, lam_q2, lam_k2, a_norm_g, w_bm, w_ba, b_gate, w_out, ln2_g, ln2_b, ffn2_w_gate, ffn2_w_up, ffn2_w_down, ln3_g, ln3_b):
    batch, seq, _ = x.shape
    rows = batch * seq
    hr = x.reshape(rows, D_MODEL)
    hm = jnp.pad(meta.astype(x.dtype), ((META_ROWS - N_META, 0), (0, 0)))

    slopes_log2 = np.array([LOG2E * 2.0 ** (-8.0 * (h + 1) / A_HEADS) for h in range(A_HEADS)], np.float32)
    slopes = jnp.asarray(np.repeat(slopes_log2, 128).reshape(1, 512))
    alibi_cols = _alibi_columns(slopes_log2, min(ATTN_BLOCK, seq))
    is_meta = (jnp.arange(META_ROWS) >= META_ROWS - N_META)[:, None]
    null_gate = jnp.where(jnp.arange(128) < M_HEADS, NEG, 0.0).astype(F32)[None, :]
    state0 = (jnp.zeros((1, M_HEADS, M_DK, 2 * M_DK), F32), jnp.full((1, 8, 128), NEG, F32),
              jnp.zeros((1, 8, 1024), F32))
    vec = lambda a: a.reshape(1, -1)

    for i in range(DEPTH):
        lam_init = 0.8 - 0.6 * math.exp(-0.3 * i)
        f1 = _ffn_weights(ffn1_w_gate[i], ffn1_w_up[i], ffn1_w_down[i])
        f2 = _ffn_weights(ffn2_w_gate[i], ffn2_w_up[i], ffn2_w_down[i])
        w_all, bif, bg = _proj_weights(w_in[i], b_if[i], b_gate[i])
        cw = jnp.pad(conv_w[i], ((0, 8 - CONV_W), (0, 0)))
        cb = vec(conv_b[i])
        lam_rows = jnp.pad(jnp.stack([lam_q1[i], lam_k1[i], lam_q2[i], lam_k2[i]]).astype(F32),
                           ((0, 4), (0, 128 - A_DK)))
        wbm, wba, wo = w_bm[i].astype(BF16), w_ba[i].astype(BF16), w_out[i].astype(BF16)

        hr = _ffn_ln(hr, *f1, vec(ln1_g[i]), vec(ln1_b[i]))
        hm = _ffn_ln(hm, *f1, vec(ln1_g[i]), vec(ln1_b[i]))

        qk_r, v_r, og_r, gt_r, aq_r, ak_r, av_r, gate_r = _proj(hr, w_all, bif, bg)
        qk_m, v_m, og_m, gt_m, aq_m, ak_m, av_m, gate_m = _proj(hm, w_all, bif, bg)
        qk_m = jnp.where(is_meta, qk_m, 0.0)
        gt_m = jnp.where(is_meta, gt_m, null_gate)

        xm_m, *state = _mlstm(qk_m, v_m, og_m, gt_m, cw, cb, vec(m_norm_g[i]), state0,
                              batch=1, chunk=META_ROWS)
        xm_r = _mlstm(qk_r, v_r, og_r, gt_r, cw, cb, vec(m_norm_g[i]), tuple(state),
                      batch=batch, chunk=min(MLSTM_CHUNK, seq))[0]

        xa_m = _attn_meta(aq_m, ak_m, av_m, lam_rows, slopes, vec(a_norm_g[i]), lam_init=lam_init)
        xa_r = _attn(aq_r, ak_r, av_r, ak_m, av_m, alibi_cols, lam_rows, slopes, vec(a_norm_g[i]),
                     batch=batch, lam_init=lam_init)

        tail_params = (wbm, wba, wo, vec(ln2_g[i]), vec(ln2_b[i]), *f2, vec(ln3_g[i]), vec(ln3_b[i]))
        hr = _merge_ffn(hr, xm_r, xa_r, gate_r, *tail_params)
        hm = _merge_ffn(hm, xm_m, xa_m, gate_m, *tail_params)

    return hr.reshape(batch, seq, D_MODEL)
```

```python
import functools
import math

import jax
import jax.numpy as jnp
import numpy as np
from jax import lax
from jax.experimental import pallas as pl
from jax.experimental.pallas import tpu as pltpu

F32 = jnp.float32
BF16 = jnp.bfloat16

D_MODEL = 1024
N_META = 16
META_ROWS = 128
M_HEADS = 4
M_DK = 128
M_WIDTH = 512
CONV_W = 4
A_HEADS = 4
A_DK = 64
A_DV = 128
A_WIDTH = 512
D_FF = 2816
FF_CHUNK = 256
N_FF_CHUNKS = D_FF // FF_CHUNK
DEPTH = 2
ALPHA = (2 * DEPTH) ** 0.25
LN_EPS = 1e-5
NEG = -1e30

SEG_QK = (0, 1024)
SEG_V = (1024, 1536)
SEG_O = (1536, 2048)
SEG_IF = (2048, 2176)
SEG_AQ = (2176, 2688)
SEG_AK = (2688, 3200)
SEG_AV = (3200, 3712)
SEG_G = (3712, 5760)
PROJ_COLS = 5760

ROW_TILE = 512
FFN_ROW_TILE = 1024
MLSTM_CHUNK = 256
MLSTM_GROUP = 8
ATTN_BLOCK = 512
ATTN_HEAD_GROUP = 4
LOG2E = math.log2(math.e)
VMEM_LIMIT = 56 * 1024 * 1024


def _nt_dot(a, b):
    return lax.dot_general(a, b, (((1,), (1,)), ((), ())), preferred_element_type=F32)


def _layer_norm(y, g, b):
    mu = jnp.mean(y, -1, keepdims=True)
    d = y - mu
    var = jnp.mean(d * d, -1, keepdims=True)
    return d * lax.rsqrt(var + LN_EPS) * g + b


def _swiglu_ln(x, wg_ref, wu_ref, wd_ref, g_ref, b_ref):
    xb = x.astype(BF16)
    acc = jnp.zeros(x.shape, F32)
    for j in range(N_FF_CHUNKS):
        cols = slice(j * FF_CHUNK, (j + 1) * FF_CHUNK)
        g = jnp.dot(xb, wg_ref[:, cols], preferred_element_type=F32)
        u = jnp.dot(xb, wu_ref[:, cols], preferred_element_type=F32)
        a = (g * jax.nn.sigmoid(g) * u).astype(BF16)
        acc = acc + jnp.dot(a, wd_ref[cols, :], preferred_element_type=F32)
    return _layer_norm(ALPHA * x + 0.5 * acc, g_ref[...], b_ref[...])


def _ffn_ln_kernel(x_ref, wg_ref, wu_ref, wd_ref, g_ref, b_ref, o_ref):
    half = x_ref.shape[0] // 2
    for rows in (slice(0, half), slice(half, 2 * half)):
        o_ref[rows, :] = _swiglu_ln(x_ref[rows, :], wg_ref, wu_ref, wd_ref, g_ref, b_ref)


def _resident(shape):
    return pl.BlockSpec(shape, lambda *_: (0,) * len(shape), pipeline_mode=pl.Buffered(1))


def _ffn_ln(x, w_gate, w_up, w_down, g, b):
    m = x.shape[0]
    tm = min(FFN_ROW_TILE, m)
    return pl.pallas_call(
        _ffn_ln_kernel,
        out_shape=jax.ShapeDtypeStruct((m, D_MODEL), F32),
        grid=(m // tm,),
        in_specs=[
            pl.BlockSpec((tm, D_MODEL), lambda i: (i, 0)),
            _resident((D_MODEL, D_FF)), _resident((D_MODEL, D_FF)), _resident((D_FF, D_MODEL)),
            _resident((1, D_MODEL)), _resident((1, D_MODEL)),
        ],
        out_specs=pl.BlockSpec((tm, D_MODEL), lambda i: (i, 0)),
        compiler_params=pltpu.CompilerParams(
            dimension_semantics=("arbitrary",), vmem_limit_bytes=VMEM_LIMIT),
    )(x, w_gate, w_up, w_down, g, b)


def _proj_kernel(h_ref, w_ref, bif_ref, bg_ref,
                 qk_ref, v_ref, og_ref, gt_ref, aq_ref, ak_ref, av_ref, gate_ref):
    hb = h_ref[...].astype(BF16)

    def seg(s):
        return jnp.dot(hb, w_ref[:, s[0]:s[1]], preferred_element_type=F32)

    gate_ref[...] = jax.nn.sigmoid(seg(SEG_G) + bg_ref[...]).astype(BF16)
    og_ref[...] = jax.nn.sigmoid(seg(SEG_O)).astype(BF16)
    z = seg(SEG_IF) + bif_ref[...]
    col = lax.broadcasted_iota(jnp.int32, z.shape, 1)
    log_f = jnp.minimum(z, 0.0) - jnp.log1p(jnp.exp(-jnp.abs(z)))
    gt_ref[...] = jnp.where(col < M_HEADS, z, jnp.where(col < 2 * M_HEADS, log_f, 0.0))
    aq_ref[...] = (seg(SEG_AQ) * (LOG2E * A_DK ** -0.5)).astype(BF16)
    qk_ref[...] = seg(SEG_QK)
    v_ref[...] = seg(SEG_V).astype(BF16)
    ak_ref[...] = seg(SEG_AK).astype(BF16)
    av_ref[...] = seg(SEG_AV).astype(BF16)


def _proj(h, w_all, b_if, b_gate):
    m = h.shape[0]
    tm = min(ROW_TILE, m)
    row = lambda n: pl.BlockSpec((tm, n), lambda i: (i, 0))
    const = lambda r, n: pl.BlockSpec((r, n), lambda i: (0, 0))
    out_shape = (
        jax.ShapeDtypeStruct((m, 1024), F32),
        jax.ShapeDtypeStruct((m, 512), BF16),
        jax.ShapeDtypeStruct((m, 512), BF16),
        jax.ShapeDtypeStruct((m, 128), F32),
        jax.ShapeDtypeStruct((m, 512), BF16),
        jax.ShapeDtypeStruct((m, 512), BF16),
        jax.ShapeDtypeStruct((m, 512), BF16),
        jax.ShapeDtypeStruct((m, 2048), BF16),
    )
    return pl.pallas_call(
        _proj_kernel,
        out_shape=out_shape,
        grid=(m // tm,),
        in_specs=[row(D_MODEL), const(D_MODEL, PROJ_COLS), const(1, 128), const(1, 2048)],
        out_specs=(row(1024), row(512), row(512), row(128), row(512), row(512), row(512), row(2048)),
        compiler_params=pltpu.CompilerParams(
            dimension_semantics=("arbitrary",), vmem_limit_bytes=VMEM_LIMIT),
    )(h, w_all, b_if, b_gate)


def _mlstm_kernel(qk_ref, v_ref, og_ref, gt_ref, cw_ref, cb_ref, ng_ref,
                  c0_ref, m0_ref, t0_ref,
                  h_ref, c_ref, m_ref, tail_ref, *, T, G):
    @pl.when(pl.program_id(1) == 0)
    def _():
        for g in range(G):
            c_ref[g] = c0_ref[0]
            m_ref[g] = m0_ref[0]
            tail_ref[g] = t0_ref[0]

    r = lax.broadcasted_iota(jnp.int32, (T, T), 0)
    c = lax.broadcasted_iota(jnp.int32, (T, T), 1)
    causal = c <= r
    eye = c == r
    tri = jnp.where(causal, 1.0, 0.0).astype(F32)
    ones = jnp.ones((T, M_DK), BF16)

    for g in range(G):
        u = jnp.concatenate([tail_ref[g], qk_ref[g]], axis=0)
        tail_ref[g] = u[T:T + 8, :]
        y = cb_ref[...] + cw_ref[CONV_W - 1:CONV_W, :] * u
        for j in range(1, CONV_W):
            y = y + cw_ref[CONV_W - 1 - j:CONV_W - j, :] * pltpu.roll(u, j, 0)
        y = y[8:8 + T, :]
        qk = y * jax.nn.sigmoid(y)

        gates = gt_ref[g]
        cums = jnp.dot(tri, gates, preferred_element_type=F32, precision=lax.Precision.HIGHEST)
        u_rows = (gates.T)[0:M_HEADS, :] - (cums.T)[M_HEADS:2 * M_HEADS, :]

        for h in range(M_HEADS):
            sl = slice(h * M_DK, (h + 1) * M_DK)
            qb = qk[:, sl].astype(BF16)
            k_h = qk[:, M_WIDTH + h * M_DK:M_WIDTH + (h + 1) * M_DK] * (M_DK ** -0.5)
            kb = k_h.astype(BF16)
            v_aug = jnp.concatenate([v_ref[g, :, sl], ones], axis=1)
            b_col = cums[:, M_HEADS + h:M_HEADS + h + 1]
            m_prev = m_ref[g, h:h + 1, 0:1]
            cn_prev = c_ref[g, h]

            um = jnp.where(causal, u_rows[h:h + 1, :], NEG)
            big_m = jnp.maximum(jnp.max(um, -1, keepdims=True), m_prev)
            p = jnp.exp(um - big_m)
            s = (_nt_dot(qb, kb) * p).astype(BF16)
            w_state = jnp.exp(m_prev - big_m)
            tot = (jnp.dot(s, v_aug, preferred_element_type=F32)
                   + w_state * jnp.dot(qb, cn_prev.astype(BF16), preferred_element_type=F32))
            num = tot[:, 0:M_DK]
            den = tot[:, M_DK:2 * M_DK]
            hh = num / jnp.maximum(jnp.abs(den), jnp.exp(-(b_col + big_m)))
            mu = jnp.mean(hh, -1, keepdims=True)
            d = hh - mu
            hn = d * lax.rsqrt(jnp.mean(d * d, -1, keepdims=True) + LN_EPS)
            h_ref[g, :, sl] = (hn * ng_ref[:, sl] * og_ref[g, :, sl].astype(F32)).astype(BF16)

            w_diag = jnp.where(eye, p[T - 1:T, :], 0.0).astype(BF16)
            vw = jnp.dot(w_diag, v_aug, preferred_element_type=F32)
            c_ref[g, h] = (w_state[T - 1:T, :] * cn_prev
                           + jnp.dot(k_h.T.astype(BF16), vw.astype(BF16), preferred_element_type=F32))
            m_ref[g, h:h + 1, :] = jnp.broadcast_to(b_col[T - 1:T, :] + big_m[T - 1:T, :], (1, 128))


def _mlstm(qk, v, og, gates, conv_w, conv_b, norm_g, state, *, batch, chunk):
    seq = qk.shape[0] // batch
    nc = seq // chunk
    group = min(MLSTM_GROUP, batch)
    c0, m0, t0 = state
    seq3 = lambda a: a.reshape(batch, seq, a.shape[-1])
    row = lambda n: pl.BlockSpec((group, chunk, n), lambda b, c: (b, c, 0))
    const2 = lambda r, n: pl.BlockSpec((r, n), lambda b, c: (0, 0))
    init3 = lambda r, n: pl.BlockSpec((1, r, n), lambda b, c: (0, 0, 0))
    out3 = lambda r, n: pl.BlockSpec((group, r, n), lambda b, c: (b, 0, 0))
    out_shape = (
        jax.ShapeDtypeStruct((batch, seq, M_WIDTH), BF16),
        jax.ShapeDtypeStruct((batch, M_HEADS, M_DK, 2 * M_DK), F32),
        jax.ShapeDtypeStruct((batch, 8, 128), F32),
        jax.ShapeDtypeStruct((batch, 8, 1024), F32),
    )
    out, c1, m1, t1 = pl.pallas_call(
        functools.partial(_mlstm_kernel, T=chunk, G=group),
        out_shape=out_shape,
        grid=(batch // group, nc),
        in_specs=[
            row(1024), row(512), row(512), row(128),
            const2(8, 1024), const2(1, 1024), const2(1, 512),
            pl.BlockSpec((1, M_HEADS, M_DK, 2 * M_DK), lambda b, c: (0, 0, 0, 0)),
            init3(8, 128), init3(8, 1024),
        ],
        out_specs=(
            row(512),
            pl.BlockSpec((group, M_HEADS, M_DK, 2 * M_DK), lambda b, c: (b, 0, 0, 0)),
            out3(8, 128), out3(8, 1024),
        ),
        compiler_params=pltpu.CompilerParams(
            dimension_semantics=("arbitrary", "arbitrary"), vmem_limit_bytes=VMEM_LIMIT),
    )(seq3(qk), seq3(v), seq3(og), seq3(gates), conv_w, conv_b, norm_g, c0, m0, t0)
    return out.reshape(batch * seq, M_WIDTH), c1, m1, t1


def _stack_maps(q):
    lane = lax.broadcasted_iota(jnp.int32, q.shape, 1)
    zero = jnp.zeros_like(q)
    return jnp.concatenate([jnp.where(lane < A_DK, q, zero), jnp.where(lane >= A_DK, q, zero)], axis=0)


def _diff_finish(acc, l, lam_ref, ng, lam_init, t):
    lam = (jnp.exp(jnp.sum(lam_ref[0:1, :] * lam_ref[1:2, :], -1, keepdims=True))
           - jnp.exp(jnp.sum(lam_ref[2:3, :] * lam_ref[3:4, :], -1, keepdims=True)) + lam_init)
    o = acc[:t] / l[:t] - lam * (acc[t:] / l[t:])
    o = o * lax.rsqrt(jnp.mean(o * o, -1, keepdims=True) + LN_EPS) * (1.0 - lam_init)
    return (o * ng).astype(BF16)


ACC_ROWS = A_DV + 16


def _fold_keys(s_t, offset, v_t, m_s, acc_s, first=False):
    keys, width = s_t.shape
    m_cur = jnp.max(s_t, axis=0, keepdims=True) + offset
    if first:
        m_new = jnp.broadcast_to(m_cur, m_s.shape)
    else:
        m_old = m_s[...]
        m_new = jnp.maximum(m_old, m_cur)
        alpha = jnp.exp2(m_old - m_new)
    p3 = jnp.exp2(s_t.reshape(keys // 8, 8, width) - (m_new - offset)[None])
    p = p3.reshape(keys, width).astype(BF16)
    pv = jnp.dot(v_t, p, preferred_element_type=F32)
    if first:
        acc_s[...] = pv
    else:
        acc3 = acc_s[...].reshape(ACC_ROWS // 8, 8, width)
        acc_s[...] = (acc3 * alpha[None]).reshape(acc_s.shape) + pv
    m_s[...] = m_new


def _attn_kernel(q_ref, k_ref, v_ref, km_ref, vm_ref, ab_ref, lam_ref, sl_ref, ngt_ref, o_ref,
                 m_s, acc_s, *, T, HG, lam_init):
    i = pl.program_id(2)
    q0 = i * T
    R = 2 * T
    cols = [slice(h * 128, (h + 1) * 128) for h in range(HG)]
    slope = [sl_ref[:, h * 128:h * 128 + 1] for h in range(HG)]
    row = lax.broadcasted_iota(jnp.int32, (128, R), 0)
    ones_rows = jnp.where(row < 3, 1.0, 0.0).astype(BF16)
    qs_t = [jnp.concatenate([_stack_maps(q_ref[:, cols[h]]).astype(F32).T.astype(BF16), ones_rows], axis=0)
            for h in range(HG)]
    stats = [(m_s.at[h], acc_s.at[h]) for h in range(HG)]

    def rows(j):
        return pl.ds(pl.multiple_of(j * T, T), T)

    def values_t(v):
        v_t = v.astype(F32).T.astype(BF16)
        return jnp.concatenate([v_t, jnp.ones((ACC_ROWS - A_DV, v_t.shape[1]), BF16)], axis=0)

    zero_off = jnp.zeros((1, 1), F32)

    km_pos = lax.broadcasted_iota(jnp.int32, (META_ROWS, 128), 0)
    s_meta = []
    for h in range(HG):
        bias = jnp.where(km_pos >= META_ROWS - N_META,
                         slope[h] * (km_pos - (META_ROWS + q0)).astype(F32), NEG)
        s_meta.append(jnp.dot(km_ref[:, cols[h]], qs_t[h][0:128], preferred_element_type=F32)
                      + jnp.concatenate([bias] * (R // 128), axis=1))
    for h in range(HG):
        _fold_keys(s_meta[h], zero_off, values_t(vm_ref[:, cols[h]]), *stats[h], first=True)

    def logits(h, j):
        k_aug = jnp.concatenate([k_ref[rows(j), cols[h]], ab_ref[:, cols[h]]], axis=1)
        return jnp.dot(k_aug, qs_t[h], preferred_element_type=F32)

    def offset(h, j):
        return slope[h] * (j * T - q0).astype(F32)

    def fold_blocks(h, first, n):
        scores = [logits(h, first + k) for k in range(n)]
        for k in range(n):
            _fold_keys(scores[k], offset(h, first + k), values_t(v_ref[rows(first + k), cols[h]]), *stats[h])

    def fold_pair_of_heads(h0, first, n):
        scores = [[logits(h, first + k) for k in range(n)] for h in (h0, h0 + 1)]
        for k in range(n):
            for d in (0, 1):
                _fold_keys(scores[d][k], offset(h0 + d, first + k),
                           values_t(v_ref[rows(first + k), cols[h0 + d]]), *stats[h0 + d])

    for h0 in range(0, HG, 2):
        lax.fori_loop(0, i // 4, lambda t, carry, h0=h0: fold_pair_of_heads(h0, 4 * t, 4) or carry, 0)

    @pl.when(i % 4 >= 2)
    def _():
        for h in range(HG):
            fold_blocks(h, i // 4 * 4, 2)

    @pl.when(i % 2 == 1)
    def _():
        scores = [logits(h, i - 1) for h in range(HG)]
        for h in range(HG):
            _fold_keys(scores[h], offset(h, i - 1), values_t(v_ref[rows(i - 1), cols[h]]), *stats[h])

    key = lax.broadcasted_iota(jnp.int32, (T, T), 0)
    qry = lax.broadcasted_iota(jnp.int32, (T, T), 1)
    keep = key <= qry
    keep2 = jnp.concatenate([keep, keep], axis=1)
    s_diag = [jnp.where(keep2, logits(h, i), NEG) for h in range(HG)]
    for h in range(HG):
        _fold_keys(s_diag[h], zero_off, values_t(v_ref[rows(i), cols[h]]), *stats[h])
    lam = (jnp.exp(jnp.sum(lam_ref[0:1, :] * lam_ref[1:2, :], -1, keepdims=True))
           - jnp.exp(jnp.sum(lam_ref[2:3, :] * lam_ref[3:4, :], -1, keepdims=True)) + lam_init)
    for h in range(HG):
        acc = acc_s[h, 0:A_DV, :]
        l = acc_s[h, A_DV:A_DV + 1, :]
        o_t = acc[:, 0:T] / l[:, 0:T] - lam * (acc[:, T:R] / l[:, T:R])
        o_t = o_t * lax.rsqrt(jnp.mean(o_t * o_t, axis=0, keepdims=True) + LN_EPS) * (1.0 - lam_init)
        o_t = jnp.concatenate([o_t[:, t * 128:(t + 1) * 128] * ngt_ref[cols[h], :] for t in range(T // 128)], axis=1)
        o_ref[:, cols[h]] = o_t.T.astype(BF16)


def _alibi_columns(slopes_log2, t):
    x = np.asarray(slopes_log2, np.float32)[None, :] * np.arange(t, dtype=np.float32)[:, None]
    hi = x.astype(BF16)
    mid = (x - hi.astype(np.float32)).astype(BF16)
    lo = (x - hi.astype(np.float32) - mid.astype(np.float32)).astype(BF16)
    cols = np.zeros((t, A_HEADS, 128), BF16)
    cols[:, :, 0], cols[:, :, 1], cols[:, :, 2] = hi, mid, lo
    return jnp.asarray(cols.reshape(t, A_HEADS * 128))


def _attn(aq, ak, av, ak_meta, av_meta, alibi_cols, lam_rows, slopes, norm_g, *, batch, lam_init):
    m = aq.shape[0]
    seq = m // batch
    t = alibi_cols.shape[0]
    nq = seq // t
    hg = ATTN_HEAD_GROUP
    w = hg * 128
    qspec = pl.BlockSpec((t, w), lambda b, h, i: (b * nq + i, h))
    kvspec = pl.BlockSpec((seq, w), lambda b, h, i: (b, h), pipeline_mode=pl.Buffered(1))
    mspec = pl.BlockSpec((META_ROWS, w), lambda b, h, i: (0, h))
    hrow = pl.BlockSpec((1, w), lambda b, h, i: (0, h))
    norm_g_t = jnp.broadcast_to(norm_g.reshape(A_WIDTH, 1), (A_WIDTH, 128))
    return pl.pallas_call(
        functools.partial(_attn_kernel, T=t, HG=hg, lam_init=lam_init),
        out_shape=jax.ShapeDtypeStruct((m, A_WIDTH), BF16),
        grid=(batch, A_HEADS // hg, nq),
        in_specs=[qspec, kvspec, kvspec, mspec, mspec,
                  pl.BlockSpec((t, w), lambda b, h, i: (0, h)),
                  pl.BlockSpec((8, 128), lambda b, h, i: (0, 0)), hrow,
                  pl.BlockSpec((w, 128), lambda b, h, i: (h, 0))],
        out_specs=qspec,
        scratch_shapes=[pltpu.VMEM((hg, 8, 2 * t), F32), pltpu.VMEM((hg, ACC_ROWS, 2 * t), F32)],
        compiler_params=pltpu.CompilerParams(
            dimension_semantics=("arbitrary", "arbitrary", "arbitrary"),
            vmem_limit_bytes=VMEM_LIMIT),
    )(aq, ak, av, ak_meta, av_meta, alibi_cols, lam_rows, slopes, norm_g_t)


def _attn_meta_kernel(q_ref, k_ref, v_ref, lam_ref, sl_ref, ng_ref, o_ref, *, lam_init):
    t = META_ROWS
    slope = sl_ref[:, 0:1]
    qs = _stack_maps(q_ref[...])
    r = lax.broadcasted_iota(jnp.int32, (t, t), 0)
    c = lax.broadcasted_iota(jnp.int32, (t, t), 1)
    bias = jnp.where((c >= t - N_META) & (c <= r), slope * (c - r).astype(F32), NEG)
    s = _nt_dot(qs, k_ref[...]) + jnp.concatenate([bias, bias], axis=0)
    p = jnp.exp2(s - jnp.max(s, -1, keepdims=True))
    l = jnp.sum(p, -1, keepdims=True)
    acc = jnp.dot(p.astype(BF16), v_ref[...], preferred_element_type=F32)
    o_ref[...] = _diff_finish(acc, l, lam_ref, ng_ref[...], lam_init, t)


def _attn_meta(aq, ak, av, lam_rows, slopes, norm_g, *, lam_init):
    blk = pl.BlockSpec((META_ROWS, 128), lambda h: (0, h))
    hrow = pl.BlockSpec((1, 128), lambda h: (0, h))
    return pl.pallas_call(
        functools.partial(_attn_meta_kernel, lam_init=lam_init),
        out_shape=jax.ShapeDtypeStruct((META_ROWS, A_WIDTH), BF16),
        grid=(A_HEADS,),
        in_specs=[blk, blk, blk, pl.BlockSpec((8, 128), lambda h: (0, 0)), hrow, hrow],
        out_specs=blk,
        compiler_params=pltpu.CompilerParams(dimension_semantics=("arbitrary",)),
    )(aq, ak, av, lam_rows, slopes, norm_g)


def _merge_ffn_kernel(h_ref, hm_ref, ha_ref, gate_ref, wbm_ref, wba_ref, wo_ref, g2_ref, b2_ref,
                      wg_ref, wu_ref, wd_ref, g3_ref, b3_ref, o_ref):
    ym = jnp.dot(hm_ref[...], wbm_ref[...], preferred_element_type=F32)
    ya = jnp.dot(ha_ref[...], wba_ref[...], preferred_element_type=F32)
    gm = gate_ref[:, 0:D_MODEL].astype(F32)
    ga = gate_ref[:, D_MODEL:2 * D_MODEL].astype(F32)
    mixed = (gm * ym + ga * ya).astype(BF16)
    mix = jnp.dot(mixed, wo_ref[...], preferred_element_type=F32)
    h2 = _layer_norm(ALPHA * h_ref[...] + mix, g2_ref[...], b2_ref[...])
    o_ref[...] = _swiglu_ln(h2, wg_ref, wu_ref, wd_ref, g3_ref, b3_ref)


def _merge_ffn(h, hm, ha, gate, w_bm, w_ba, w_out, g2, b2, w_gate, w_up, w_down, g3, b3):
    m = h.shape[0]
    tm = min(ROW_TILE, m)
    row = lambda n: pl.BlockSpec((tm, n), lambda i: (i, 0))
    return pl.pallas_call(
        _merge_ffn_kernel,
        out_shape=jax.ShapeDtypeStruct((m, D_MODEL), F32),
        grid=(m // tm,),
        in_specs=[row(D_MODEL), row(512), row(512), row(2048),
                  _resident((512, D_MODEL)), _resident((512, D_MODEL)), _resident((D_MODEL, D_MODEL)),
                  _resident((1, D_MODEL)), _resident((1, D_MODEL)),
                  _resident((D_MODEL, D_FF)), _resident((D_MODEL, D_FF)), _resident((D_FF, D_MODEL)),
                  _resident((1, D_MODEL)), _resident((1, D_MODEL))],
        out_specs=row(D_MODEL),
        compiler_params=pltpu.CompilerParams(
            dimension_semantics=("arbitrary",), vmem_limit_bytes=VMEM_LIMIT),
    )(h, hm, ha, gate, w_bm, w_ba, w_out, g2, b2, w_gate, w_up, w_down, g3, b3)


def _ffn_weights(w_gate, w_up, w_down):
    return w_gate.astype(BF16), w_up.astype(BF16), w_down.astype(BF16)


def _proj_weights(w_in, b_if, b_gate):
    w = w_in.astype(BF16)
    w_all = jnp.concatenate(
        [w[:, :2048], jnp.pad(w[:, 2048:2056], ((0, 0), (0, 120))), w[:, 2056:]], axis=1)
    return w_all, jnp.pad(b_if, (0, 120)).reshape(1, 128), b_gate.reshape(1, 2 * D_MODEL)


def kernel(x, meta, ffn1_w_gate, ffn1_w_up, ffn1_w_down, ln1_g, ln1_b, w_in, conv_w, conv_b, b_if, m_norm_g, lam_q1, lam_k1, lam_q2, lam_k2, a_norm_g, w_bm, w_ba, b_gate, w_out, ln2_g, ln2_b, ffn2_w_gate, ffn2_w_up, ffn2_w_down, ln3_g, ln3_b):
    batch, seq, _ = x.shape
    rows = batch * seq
    hr = x.reshape(rows, D_MODEL)
    hm = jnp.pad(meta.astype(x.dtype), ((META_ROWS - N_META, 0), (0, 0)))

    slopes_log2 = np.array([LOG2E * 2.0 ** (-8.0 * (h + 1) / A_HEADS) for h in range(A_HEADS)], np.float32)
    slopes = jnp.asarray(np.repeat(slopes_log2, 128).reshape(1, 512))
    alibi_cols = _alibi_columns(slopes_log2, min(ATTN_BLOCK, seq))
    is_meta = (jnp.arange(META_ROWS) >= META_ROWS - N_META)[:, None]
    null_gate = jnp.where(jnp.arange(128) < M_HEADS, NEG, 0.0).astype(F32)[None, :]
    state0 = (jnp.zeros((1, M_HEADS, M_DK, 2 * M_DK), F32), jnp.full((1, 8, 128), NEG, F32),
              jnp.zeros((1, 8, 1024), F32))
    vec = lambda a: a.reshape(1, -1)

    for i in range(DEPTH):
        lam_init = 0.8 - 0.6 * math.exp(-0.3 * i)
        f1 = _ffn_weights(ffn1_w_gate[i], ffn1_w_up[i], ffn1_w_down[i])
        f2 = _ffn_weights(ffn2_w_gate[i], ffn2_w_up[i], ffn2_w_down[i])
        w_all, bif, bg = _proj_weights(w_in[i], b_if[i], b_gate[i])
        cw = jnp.pad(conv_w[i], ((0, 8 - CONV_W), (0, 0)))
        cb = vec(conv_b[i])
        lam_rows = jnp.pad(jnp.stack([lam_q1[i], lam_k1[i], lam_q2[i], lam_k2[i]]).astype(F32),
                           ((0, 4), (0, 128 - A_DK)))
        wbm, wba, wo = w_bm[i].astype(BF16), w_ba[i].astype(BF16), w_out[i].astype(BF16)

        hr = _ffn_ln(hr, *f1, vec(ln1_g[i]), vec(ln1_b[i]))
        hm = _ffn_ln(hm, *f1, vec(ln1_g[i]), vec(ln1_b[i]))

        qk_r, v_r, og_r, gt_r, aq_r, ak_r, av_r, gate_r = _proj(hr, w_all, bif, bg)
        qk_m, v_m, og_m, gt_m, aq_m, ak_m, av_m, gate_m = _proj(hm, w_all, bif, bg)
        qk_m = jnp.where(is_meta, qk_m, 0.0)
        gt_m = jnp.where(is_meta, gt_m, null_gate)

        xm_m, *state = _mlstm(qk_m, v_m, og_m, gt_m, cw, cb, vec(m_norm_g[i]), state0,
                              batch=1, chunk=META_ROWS)
        xm_r = _mlstm(qk_r, v_r, og_r, gt_r, cw, cb, vec(m_norm_g[i]), tuple(state),
                      batch=batch, chunk=min(MLSTM_CHUNK, seq))[0]

        xa_m = _attn_meta(aq_m, ak_m, av_m, lam_rows, slopes, vec(a_norm_g[i]), lam_init=lam_init)
        xa_r = _attn(aq_r, ak_r, av_r, ak_m, av_m, alibi_cols, lam_rows, slopes, vec(a_norm_g[i]),
                     batch=batch, lam_init=lam_init)

        tail_params = (wbm, wba, wo, vec(ln2_g[i]), vec(ln2_b[i]), *f2, vec(ln3_g[i]), vec(ln3_b[i]))
        hr = _merge_ffn(hr, xm_r, xa_r, gate_r, *tail_params)
        hm = _merge_ffn(hm, xm_m, xa_m, gate_m, *tail_params)

    return hr.reshape(batch, seq, D_MODEL)
```

```python
import functools
import math

import jax
import jax.numpy as jnp
import numpy as np
from jax import lax
from jax.experimental import pallas as pl
from jax.experimental.pallas import tpu as pltpu

F32 = jnp.float32
BF16 = jnp.bfloat16

D_MODEL = 1024
N_META = 16
META_ROWS = 128
M_HEADS = 4
M_DK = 128
M_WIDTH = 512
CONV_W = 4
A_HEADS = 4
A_DK = 64
A_DV = 128
A_WIDTH = 512
D_FF = 2816
FF_CHUNK = 256
N_FF_CHUNKS = D_FF // FF_CHUNK
DEPTH = 2
ALPHA = (2 * DEPTH) ** 0.25
LN_EPS = 1e-5
NEG = -1e30

SEG_QK = (0, 1024)
SEG_V = (1024, 1536)
SEG_O = (1536, 2048)
SEG_IF = (2048, 2176)
SEG_AQ = (2176, 2688)
SEG_AK = (2688, 3200)
SEG_AV = (3200, 3712)
SEG_G = (3712, 5760)
PROJ_COLS = 5760

ROW_TILE = 512
FFN_ROW_TILE = 1024
MLSTM_CHUNK = 256
MLSTM_GROUP = 8
ATTN_BLOCK = 512
ATTN_HEAD_GROUP = 4
LOG2E = math.log2(math.e)
VMEM_LIMIT = 56 * 1024 * 1024


def _nt_dot(a, b):
    return lax.dot_general(a, b, (((1,), (1,)), ((), ())), preferred_element_type=F32)


def _layer_norm(y, g, b):
    mu = jnp.mean(y, -1, keepdims=True)
    d = y - mu
    var = jnp.mean(d * d, -1, keepdims=True)
    return d * lax.rsqrt(var + LN_EPS) * g + b


def _swiglu_ln(x, wg_ref, wu_ref, wd_ref, g_ref, b_ref):
    xb = x.astype(BF16)
    acc = jnp.zeros(x.shape, F32)
    for j in range(N_FF_CHUNKS):
        cols = slice(j * FF_CHUNK, (j + 1) * FF_CHUNK)
        g = jnp.dot(xb, wg_ref[:, cols], preferred_element_type=F32)
        u = jnp.dot(xb, wu_ref[:, cols], preferred_element_type=F32)
        a = (g * jax.nn.sigmoid(g) * u).astype(BF16)
        acc = acc + jnp.dot(a, wd_ref[cols, :], preferred_element_type=F32)
    return _layer_norm(ALPHA * x + 0.5 * acc, g_ref[...], b_ref[...])


def _ffn_ln_kernel(x_ref, wg_ref, wu_ref, wd_ref, g_ref, b_ref, o_ref):
    half = x_ref.shape[0] // 2
    for rows in (slice(0, half), slice(half, 2 * half)):
        o_ref[rows, :] = _swiglu_ln(x_ref[rows, :], wg_ref, wu_ref, wd_ref, g_ref, b_ref)


def _resident(shape):
    return pl.BlockSpec(shape, lambda *_: (0,) * len(shape), pipeline_mode=pl.Buffered(1))


def _ffn_ln(x, w_gate, w_up, w_down, g, b):
    m = x.shape[0]
    tm = min(FFN_ROW_TILE, m)
    return pl.pallas_call(
        _ffn_ln_kernel,
        out_shape=jax.ShapeDtypeStruct((m, D_MODEL), F32),
        grid=(m // tm,),
        in_specs=[
            pl.BlockSpec((tm, D_MODEL), lambda i: (i, 0)),
            _resident((D_MODEL, D_FF)), _resident((D_MODEL, D_FF)), _resident((D_FF, D_MODEL)),
            _resident((1, D_MODEL)), _resident((1, D_MODEL)),
        ],
        out_specs=pl.BlockSpec((tm, D_MODEL), lambda i: (i, 0)),
        compiler_params=pltpu.CompilerParams(
            dimension_semantics=("arbitrary",), vmem_limit_bytes=VMEM_LIMIT),
    )(x, w_gate, w_up, w_down, g, b)


def _proj_kernel(h_ref, w_ref, bif_ref, bg_ref,
                 qk_ref, v_ref, og_ref, gt_ref, aq_ref, ak_ref, av_ref, gate_ref):
    hb = h_ref[...].astype(BF16)

    def seg(s):
        return jnp.dot(hb, w_ref[:, s[0]:s[1]], preferred_element_type=F32)

    gate_ref[...] = jax.nn.sigmoid(seg(SEG_G) + bg_ref[...]).astype(BF16)
    og_ref[...] = jax.nn.sigmoid(seg(SEG_O)).astype(BF16)
    z = seg(SEG_IF) + bif_ref[...]
    col = lax.broadcasted_iota(jnp.int32, z.shape, 1)
    log_f = jnp.minimum(z, 0.0) - jnp.log1p(jnp.exp(-jnp.abs(z)))
    gt_ref[...] = jnp.where(col < M_HEADS, z, jnp.where(col < 2 * M_HEADS, log_f, 0.0))
    aq_ref[...] = (seg(SEG_AQ) * (LOG2E * A_DK ** -0.5)).astype(BF16)
    qk_ref[...] = seg(SEG_QK)
    v_ref[...] = seg(SEG_V).astype(BF16)
    ak_ref[...] = seg(SEG_AK).astype(BF16)
    av_ref[...] = seg(SEG_AV).astype(BF16)


def _proj(h, w_all, b_if, b_gate):
    m = h.shape[0]
    tm = min(ROW_TILE, m)
    row = lambda n: pl.BlockSpec((tm, n), lambda i: (i, 0))
    const = lambda r, n: pl.BlockSpec((r, n), lambda i: (0, 0))
    out_shape = (
        jax.ShapeDtypeStruct((m, 1024), F32),
        jax.ShapeDtypeStruct((m, 512), BF16),
        jax.ShapeDtypeStruct((m, 512), BF16),
        jax.ShapeDtypeStruct((m, 128), F32),
        jax.ShapeDtypeStruct((m, 512), BF16),
        jax.ShapeDtypeStruct((m, 512), BF16),
        jax.ShapeDtypeStruct((m, 512), BF16),
        jax.ShapeDtypeStruct((m, 2048), BF16),
    )
    return pl.pallas_call(
        _proj_kernel,
        out_shape=out_shape,
        grid=(m // tm,),
        in_specs=[row(D_MODEL), const(D_MODEL, PROJ_COLS), const(1, 128), const(1, 2048)],
        out_specs=(row(1024), row(512), row(512), row(128), row(512), row(512), row(512), row(2048)),
        compiler_params=pltpu.CompilerParams(
            dimension_semantics=("arbitrary",), vmem_limit_bytes=VMEM_LIMIT),
    )(h, w_all, b_if, b_gate)


def _mlstm_kernel(qk_ref, v_ref, og_ref, gt_ref, cw_ref, cb_ref, ng_ref,
                  c0_ref, m0_ref, t0_ref,
                  h_ref, c_ref, m_ref, tail_ref, *, T, G):
    @pl.when(pl.program_id(1) == 0)
    def _():
        for g in range(G):
            c_ref[g] = c0_ref[0]
            m_ref[g] = m0_ref[0]
            tail_ref[g] = t0_ref[0]

    r = lax.broadcasted_iota(jnp.int32, (T, T), 0)
    c = lax.broadcasted_iota(jnp.int32, (T, T), 1)
    causal = c <= r
    eye = c == r
    tri = jnp.where(causal, 1.0, 0.0).astype(F32)
    ones = jnp.ones((T, M_DK), BF16)

    for g in range(G):
        u = jnp.concatenate([tail_ref[g], qk_ref[g]], axis=0)
        tail_ref[g] = u[T:T + 8, :]
        y = cb_ref[...] + cw_ref[CONV_W - 1:CONV_W, :] * u
        for j in range(1, CONV_W):
            y = y + cw_ref[CONV_W - 1 - j:CONV_W - j, :] * pltpu.roll(u, j, 0)
        y = y[8:8 + T, :]
        qk = y * jax.nn.sigmoid(y)

        gates = gt_ref[g]
        cums = jnp.dot(tri, gates, preferred_element_type=F32, precision=lax.Precision.HIGHEST)
        u_rows = (gates.T)[0:M_HEADS, :] - (cums.T)[M_HEADS:2 * M_HEADS, :]

        for h in range(M_HEADS):
            sl = slice(h * M_DK, (h + 1) * M_DK)
            qb = qk[:, sl].astype(BF16)
            k_h = qk[:, M_WIDTH + h * M_DK:M_WIDTH + (h + 1) * M_DK] * (M_DK ** -0.5)
            kb = k_h.astype(BF16)
            v_aug = jnp.concatenate([v_ref[g, :, sl], ones], axis=1)
            b_col = cums[:, M_HEADS + h:M_HEADS + h + 1]
            m_prev = m_ref[g, h:h + 1, 0:1]
            cn_prev = c_ref[g, h]

            um = jnp.where(causal, u_rows[h:h + 1, :], NEG)
            big_m = jnp.maximum(jnp.max(um, -1, keepdims=True), m_prev)
            p = jnp.exp(um - big_m)
            s = (_nt_dot(qb, kb) * p).astype(BF16)
            w_state = jnp.exp(m_prev - big_m)
            tot = (jnp.dot(s, v_aug, preferred_element_type=F32)
                   + w_state * jnp.dot(qb, cn_prev.astype(BF16), preferred_element_type=F32))
            num = tot[:, 0:M_DK]
            den = tot[:, M_DK:2 * M_DK]
            hh = num / jnp.maximum(jnp.abs(den), jnp.exp(-(b_col + big_m)))
            mu = jnp.mean(hh, -1, keepdims=True)
            d = hh - mu
            hn = d * lax.rsqrt(jnp.mean(d * d, -1, keepdims=True) + LN_EPS)
            h_ref[g, :, sl] = (hn * ng_ref[:, sl] * og_ref[g, :, sl].astype(F32)).astype(BF16)

            w_diag = jnp.where(eye, p[T - 1:T, :], 0.0).astype(BF16)
            vw = jnp.dot(w_diag, v_aug, preferred_element_type=F32)
            c_ref[g, h] = (w_state[T - 1:T, :] * cn_prev
                           + jnp.dot(k_h.T.astype(BF16), vw.astype(BF16), preferred_element_type=F32))
            m_ref[g, h:h + 1, :] = jnp.broadcast_to(b_col[T - 1:T, :] + big_m[T - 1:T, :], (1, 128))


def _mlstm(qk, v, og, gates, conv_w, conv_b, norm_g, state, *, batch, chunk):
    seq = qk.shape[0] // batch
    nc = seq // chunk
    group = min(MLSTM_GROUP, batch)
    c0, m0, t0 = state
    seq3 = lambda a: a.reshape(batch, seq, a.shape[-1])
    row = lambda n: pl.BlockSpec((group, chunk, n), lambda b, c: (b, c, 0))
    const2 = lambda r, n: pl.BlockSpec((r, n), lambda b, c: (0, 0))
    init3 = lambda r, n: pl.BlockSpec((1, r, n), lambda b, c: (0, 0, 0))
    out3 = lambda r, n: pl.BlockSpec((group, r, n), lambda b, c: (b, 0, 0))
    out_shape = (
        jax.ShapeDtypeStruct((batch, seq, M_WIDTH), BF16),
        jax.ShapeDtypeStruct((batch, M_HEADS, M_DK, 2 * M_DK), F32),
        jax.ShapeDtypeStruct((batch, 8, 128), F32),
        jax.ShapeDtypeStruct((batch, 8, 1024), F32),
    )
    out, c1, m1, t1 = pl.pallas_call(
        functools.partial(_mlstm_kernel, T=chunk, G=group),
        out_shape=out_shape,
        grid=(batch // group, nc),
        in_specs=[
            row(1024), row(512), row(512), row(128),
            const2(8, 1024), const2(1, 1024), const2(1, 512),
            pl.BlockSpec((1, M_HEADS, M_DK, 2 * M_DK), lambda b, c: (0, 0, 0, 0)),
            init3(8, 128), init3(8, 1024),
        ],
        out_specs=(
            row(512),
            pl.BlockSpec((group, M_HEADS, M_DK, 2 * M_DK), lambda b, c: (b, 0, 0, 0)),
            out3(8, 128), out3(8, 1024),
        ),
        compiler_params=pltpu.CompilerParams(
            dimension_semantics=("arbitrary", "arbitrary"), vmem_limit_bytes=VMEM_LIMIT),
    )(seq3(qk), seq3(v), seq3(og), seq3(gates), conv_w, conv_b, norm_g, c0, m0, t0)
    return out.reshape(batch * seq, M_WIDTH), c1, m1, t1


def _stack_maps(q):
    lane = lax.broadcasted_iota(jnp.int32, q.shape, 1)
    zero = jnp.zeros_like(q)
    return jnp.concatenate([jnp.where(lane < A_DK, q, zero), jnp.where(lane >= A_DK, q, zero)], axis=0)


def _diff_finish(acc, l, lam_ref, ng, lam_init, t):
    lam = (jnp.exp(jnp.sum(lam_ref[0:1, :] * lam_ref[1:2, :], -1, keepdims=True))
           - jnp.exp(jnp.sum(lam_ref[2:3, :] * lam_ref[3:4, :], -1, keepdims=True)) + lam_init)
    o = acc[:t] / l[:t] - lam * (acc[t:] / l[t:])
    o = o * lax.rsqrt(jnp.mean(o * o, -1, keepdims=True) + LN_EPS) * (1.0 - lam_init)
    return (o * ng).astype(BF16)


ACC_ROWS = A_DV + 16


def _fold_keys(s_t, offset, v_t, m_s, acc_s):
    keys, width = s_t.shape
    half = width // 2
    alphas, ps = [], []
    for lanes in (slice(0, half), slice(half, width)):
        s_h = s_t[:, lanes]
        m_old = m_s[:, lanes]
        m_new = jnp.maximum(m_old, jnp.max(s_h, axis=0, keepdims=True) + offset)
        alphas.append(jnp.exp2(m_old - m_new))
        p3 = jnp.exp2(s_h.reshape(keys // 8, 8, half) - (m_new - offset)[None])
        ps.append(p3.reshape(keys, half).astype(BF16))
        m_s[:, lanes] = m_new
    alpha = jnp.concatenate(alphas, axis=1)
    p = jnp.concatenate(ps, axis=1)
    acc3 = acc_s[...].reshape(ACC_ROWS // 8, 8, width)
    acc_s[...] = ((acc3 * alpha[None]).reshape(acc_s.shape)
                  + jnp.dot(v_t, p, preferred_element_type=F32))


def _attn_kernel(q_ref, k_ref, v_ref, km_ref, vm_ref, ab_ref, lam_ref, sl_ref, ngt_ref, o_ref,
                 m_s, acc_s, *, T, HG, lam_init):
    i = pl.program_id(2)
    q0 = i * T
    R = 2 * T
    cols = [slice(h * 128, (h + 1) * 128) for h in range(HG)]
    slope = [sl_ref[:, h * 128:h * 128 + 1] for h in range(HG)]
    row = lax.broadcasted_iota(jnp.int32, (128, R), 0)
    ones_rows = jnp.where(row < 3, 1.0, 0.0).astype(BF16)
    qs_t = [jnp.concatenate([_stack_maps(q_ref[:, cols[h]]).astype(F32).T.astype(BF16), ones_rows], axis=0)
            for h in range(HG)]
    stats = [(m_s.at[h], acc_s.at[h]) for h in range(HG)]

    m_s[...] = jnp.full(m_s.shape, NEG, F32)
    acc_s[...] = jnp.zeros(acc_s.shape, F32)

    def rows(j):
        return pl.ds(pl.multiple_of(j * T, T), T)

    def values_t(v):
        v_t = v.astype(F32).T.astype(BF16)
        return jnp.concatenate([v_t, jnp.ones((ACC_ROWS - A_DV, v_t.shape[1]), BF16)], axis=0)

    zero_off = jnp.zeros((1, 1), F32)

    km_pos = lax.broadcasted_iota(jnp.int32, (META_ROWS, 128), 0)
    s_meta = []
    for h in range(HG):
        bias = jnp.where(km_pos >= META_ROWS - N_META,
                         slope[h] * (km_pos - (META_ROWS + q0)).astype(F32), NEG)
        s_meta.append(jnp.dot(km_ref[:, cols[h]], qs_t[h][0:128], preferred_element_type=F32)
                      + jnp.concatenate([bias] * (R // 128), axis=1))
    for h in range(HG):
        _fold_keys(s_meta[h], zero_off, values_t(vm_ref[:, cols[h]]), *stats[h])

    def logits(h, j):
        k_aug = jnp.concatenate([k_ref[rows(j), cols[h]], ab_ref[:, cols[h]]], axis=1)
        return jnp.dot(k_aug, qs_t[h], preferred_element_type=F32)

    def offset(h, j):
        return slope[h] * (j * T - q0).astype(F32)

    def fold_blocks(h, first, n):
        scores = [logits(h, first + k) for k in range(n)]
        for k in range(n):
            _fold_keys(scores[k], offset(h, first + k), values_t(v_ref[rows(first + k), cols[h]]), *stats[h])

    def fold_pair_of_heads(h0, first, n):
        scores = [[logits(h, first + k) for k in range(n)] for h in (h0, h0 + 1)]
        for k in range(n):
            for d in (0, 1):
                _fold_keys(scores[d][k], offset(h0 + d, first + k),
                           values_t(v_ref[rows(first + k), cols[h0 + d]]), *stats[h0 + d])

    for h0 in range(0, HG, 2):
        lax.fori_loop(0, i // 4, lambda t, carry, h0=h0: fold_pair_of_heads(h0, 4 * t, 4) or carry, 0)

    @pl.when(i % 4 >= 2)
    def _():
        for h in range(HG):
            fold_blocks(h, i // 4 * 4, 2)

    @pl.when(i % 2 == 1)
    def _():
        scores = [logits(h, i - 1) for h in range(HG)]
        for h in range(HG):
            _fold_keys(scores[h], offset(h, i - 1), values_t(v_ref[rows(i - 1), cols[h]]), *stats[h])

    key = lax.broadcasted_iota(jnp.int32, (T, T), 0)
    qry = lax.broadcasted_iota(jnp.int32, (T, T), 1)
    keep = key <= qry
    keep2 = jnp.concatenate([keep, keep], axis=1)
    s_diag = [jnp.where(keep2, logits(h, i), NEG) for h in range(HG)]
    for h in range(HG):
        _fold_keys(s_diag[h], zero_off, values_t(v_ref[rows(i), cols[h]]), *stats[h])
    lam = (jnp.exp(jnp.sum(lam_ref[0:1, :] * lam_ref[1:2, :], -1, keepdims=True))
           - jnp.exp(jnp.sum(lam_ref[2:3, :] * lam_ref[3:4, :], -1, keepdims=True)) + lam_init)
    for h in range(HG):
        acc = acc_s[h, 0:A_DV, :]
        l = acc_s[h, A_DV:A_DV + 1, :]
        o_t = acc[:, 0:T] / l[:, 0:T] - lam * (acc[:, T:R] / l[:, T:R])
        o_t = o_t * lax.rsqrt(jnp.mean(o_t * o_t, axis=0, keepdims=True) + LN_EPS) * (1.0 - lam_init)
        o_t = jnp.concatenate([o_t[:, t * 128:(t + 1) * 128] * ngt_ref[cols[h], :] for t in range(T // 128)], axis=1)
        o_ref[:, cols[h]] = o_t.T.astype(BF16)


def _alibi_columns(slopes_log2, t):
    x = np.asarray(slopes_log2, np.float32)[None, :] * np.arange(t, dtype=np.float32)[:, None]
    hi = x.astype(BF16)
    mid = (x - hi.astype(np.float32)).astype(BF16)
    lo = (x - hi.astype(np.float32) - mid.astype(np.float32)).astype(BF16)
    cols = np.zeros((t, A_HEADS, 128), BF16)
    cols[:, :, 0], cols[:, :, 1], cols[:, :, 2] = hi, mid, lo
    return jnp.asarray(cols.reshape(t, A_HEADS * 128))


def _attn(aq, ak, av, ak_meta, av_meta, alibi_cols, lam_rows, slopes, norm_g, *, batch, lam_init):
    m = aq.shape[0]
    seq = m // batch
    t = alibi_cols.shape[0]
    nq = seq // t
    hg = ATTN_HEAD_GROUP
    w = hg * 128
    qspec = pl.BlockSpec((t, w), lambda b, h, i: (b * nq + i, h))
    kvspec = pl.BlockSpec((seq, w), lambda b, h, i: (b, h), pipeline_mode=pl.Buffered(1))
    mspec = pl.BlockSpec((META_ROWS, w), lambda b, h, i: (0, h))
    hrow = pl.BlockSpec((1, w), lambda b, h, i: (0, h))
    norm_g_t = jnp.broadcast_to(norm_g.reshape(A_WIDTH, 1), (A_WIDTH, 128))
    return pl.pallas_call(
        functools.partial(_attn_kernel, T=t, HG=hg, lam_init=lam_init),
        out_shape=jax.ShapeDtypeStruct((m, A_WIDTH), BF16),
        grid=(batch, A_HEADS // hg, nq),
        in_specs=[qspec, kvspec, kvspec, mspec, mspec,
                  pl.BlockSpec((t, w), lambda b, h, i: (0, h)),
                  pl.BlockSpec((8, 128), lambda b, h, i: (0, 0)), hrow,
                  pl.BlockSpec((w, 128), lambda b, h, i: (h, 0))],
        out_specs=qspec,
        scratch_shapes=[pltpu.VMEM((hg, 8, 2 * t), F32), pltpu.VMEM((hg, ACC_ROWS, 2 * t), F32)],
        compiler_params=pltpu.CompilerParams(
            dimension_semantics=("arbitrary", "arbitrary", "arbitrary"),
            vmem_limit_bytes=VMEM_LIMIT),
    )(aq, ak, av, ak_meta, av_meta, alibi_cols, lam_rows, slopes, norm_g_t)


def _attn_meta_kernel(q_ref, k_ref, v_ref, lam_ref, sl_ref, ng_ref, o_ref, *, lam_init):
    t = META_ROWS
    slope = sl_ref[:, 0:1]
    qs = _stack_maps(q_ref[...])
    r = lax.broadcasted_iota(jnp.int32, (t, t), 0)
    c = lax.broadcasted_iota(jnp.int32, (t, t), 1)
    bias = jnp.where((c >= t - N_META) & (c <= r), slope * (c - r).astype(F32), NEG)
    s = _nt_dot(qs, k_ref[...]) + jnp.concatenate([bias, bias], axis=0)
    p = jnp.exp2(s - jnp.max(s, -1, keepdims=True))
    l = jnp.sum(p, -1, keepdims=True)
    acc = jnp.dot(p.astype(BF16), v_ref[...], preferred_element_type=F32)
    o_ref[...] = _diff_finish(acc, l, lam_ref, ng_ref[...], lam_init, t)


def _attn_meta(aq, ak, av, lam_rows, slopes, norm_g, *, lam_init):
    blk = pl.BlockSpec((META_ROWS, 128), lambda h: (0, h))
    hrow = pl.BlockSpec((1, 128), lambda h: (0, h))
    return pl.pallas_call(
        functools.partial(_attn_meta_kernel, lam_init=lam_init),
        out_shape=jax.ShapeDtypeStruct((META_ROWS, A_WIDTH), BF16),
        grid=(A_HEADS,),
        in_specs=[blk, blk, blk, pl.BlockSpec((8, 128), lambda h: (0, 0)), hrow, hrow],
        out_specs=blk,
        compiler_params=pltpu.CompilerParams(dimension_semantics=("arbitrary",)),
    )(aq, ak, av, lam_rows, slopes, norm_g)


def _merge_ffn_kernel(h_ref, hm_ref, ha_ref, gate_ref, wbm_ref, wba_ref, wo_ref, g2_ref, b2_ref,
                      wg_ref, wu_ref, wd_ref, g3_ref, b3_ref, o_ref):
    ym = jnp.dot(hm_ref[...], wbm_ref[...], preferred_element_type=F32)
    ya = jnp.dot(ha_ref[...], wba_ref[...], preferred_element_type=F32)
    gm = gate_ref[:, 0:D_MODEL].astype(F32)
    ga = gate_ref[:, D_MODEL:2 * D_MODEL].astype(F32)
    mixed = (gm * ym + ga * ya).astype(BF16)
    mix = jnp.dot(mixed, wo_ref[...], preferred_element_type=F32)
    h2 = _layer_norm(ALPHA * h_ref[...] + mix, g2_ref[...], b2_ref[...])
    o_ref[...] = _swiglu_ln(h2, wg_ref, wu_ref, wd_ref, g3_ref, b3_ref)


def _merge_ffn(h, hm, ha, gate, w_bm, w_ba, w_out, g2, b2, w_gate, w_up, w_down, g3, b3):
    m = h.shape[0]
    tm = min(ROW_TILE, m)
    row = lambda n: pl.BlockSpec((tm, n), lambda i: (i, 0))
    return pl.pallas_call(
        _merge_ffn_kernel,
        out_shape=jax.ShapeDtypeStruct((m, D_MODEL), F32),
        grid=(m // tm,),
        in_specs=[row(D_MODEL), row(512), row(512), row(2048),
                  _resident((512, D_MODEL)), _resident((512, D_MODEL)), _resident((D_MODEL, D_MODEL)),
                  _resident((1, D_MODEL)), _resident((1, D_MODEL)),
                  _resident((D_MODEL, D_FF)), _resident((D_MODEL, D_FF)), _resident((D_FF, D_MODEL)),
                  _resident((1, D_MODEL)), _resident((1, D_MODEL))],
        out_specs=row(D_MODEL),
        compiler_params=pltpu.CompilerParams(
            dimension_semantics=("arbitrary",), vmem_limit_bytes=VMEM_LIMIT),
    )(h, hm, ha, gate, w_bm, w_ba, w_out, g2, b2, w_gate, w_up, w_down, g3, b3)


def _ffn_weights(w_gate, w_up, w_down):
    return w_gate.astype(BF16), w_up.astype(BF16), w_down.astype(BF16)


def _proj_weights(w_in, b_if, b_gate):
    w = w_in.astype(BF16)
    w_all = jnp.concatenate(
        [w[:, :2048], jnp.pad(w[:, 2048:2056], ((0, 0), (0, 120))), w[:, 2056:]], axis=1)
    return w_all, jnp.pad(b_if, (0, 120)).reshape(1, 128), b_gate.reshape(1, 2 * D_MODEL)


def kernel(x, meta, ffn1_w_gate, ffn1_w_up, ffn1_w_down, ln1_g, ln1_b, w_in, conv_w, conv_b, b_if, m_norm_g, lam_q1, lam_k1, lam_q2, lam_k2, a_norm_g, w_bm, w_ba, b_gate, w_out, ln2_g, ln2_b, ffn2_w_gate, ffn2_w_up, ffn2_w_down, ln3_g, ln3_b):
    batch, seq, _ = x.shape
    rows = batch * seq
    hr = x.reshape(rows, D_MODEL)
    hm = jnp.pad(meta.astype(x.dtype), ((META_ROWS - N_META, 0), (0, 0)))

    slopes_log2 = np.array([LOG2E * 2.0 ** (-8.0 * (h + 1) / A_HEADS) for h in range(A_HEADS)], np.float32)
    slopes = jnp.asarray(np.repeat(slopes_log2, 128).reshape(1, 512))
    alibi_cols = _alibi_columns(slopes_log2, min(ATTN_BLOCK, seq))
    is_meta = (jnp.arange(META_ROWS) >= META_ROWS - N_META)[:, None]
    null_gate = jnp.where(jnp.arange(128) < M_HEADS, NEG, 0.0).astype(F32)[None, :]
    state0 = (jnp.zeros((1, M_HEADS, M_DK, 2 * M_DK), F32), jnp.full((1, 8, 128), NEG, F32),
              jnp.zeros((1, 8, 1024), F32))
    vec = lambda a: a.reshape(1, -1)

    for i in range(DEPTH):
        lam_init = 0.8 - 0.6 * math.exp(-0.3 * i)
        f1 = _ffn_weights(ffn1_w_gate[i], ffn1_w_up[i], ffn1_w_down[i])
        f2 = _ffn_weights(ffn2_w_gate[i], ffn2_w_up[i], ffn2_w_down[i])
        w_all, bif, bg = _proj_weights(w_in[i], b_if[i], b_gate[i])
        cw = jnp.pad(conv_w[i], ((0, 8 - CONV_W), (0, 0)))
        cb = vec(conv_b[i])
        lam_rows = jnp.pad(jnp.stack([lam_q1[i], lam_k1[i], lam_q2[i], lam_k2[i]]).astype(F32),
                           ((0, 4), (0, 128 - A_DK)))
        wbm, wba, wo = w_bm[i].astype(BF16), w_ba[i].astype(BF16), w_out[i].astype(BF16)

        hr = _ffn_ln(hr, *f1, vec(ln1_g[i]), vec(ln1_b[i]))
        hm = _ffn_ln(hm, *f1, vec(ln1_g[i]), vec(ln1_b[i]))

        qk_r, v_r, og_r, gt_r, aq_r, ak_r, av_r, gate_r = _proj(hr, w_all, bif, bg)
        qk_m, v_m, og_m, gt_m, aq_m, ak_m, av_m, gate_m = _proj(hm, w_all, bif, bg)
        qk_m = jnp.where(is_meta, qk_m, 0.0)
        gt_m = jnp.where(is_meta, gt_m, null_gate)

        xm_m, *state = _mlstm(qk_m, v_m, og_m, gt_m, cw, cb, vec(m_norm_g[i]), state0,
                              batch=1, chunk=META_ROWS)
        xm_r = _mlstm(qk_r, v_r, og_r, gt_r, cw, cb, vec(m_norm_g[i]), tuple(state),
                      batch=batch, chunk=min(MLSTM_CHUNK, seq))[0]

        xa_m = _attn_meta(aq_m, ak_m, av_m, lam_rows, slopes, vec(a_norm_g[i]), lam_init=lam_init)
        xa_r = _attn(aq_r, ak_r, av_r, ak_m, av_m, alibi_cols, lam_rows, slopes, vec(a_norm_g[i]),
                     batch=batch, lam_init=lam_init)

        tail_params = (wbm, wba, wo, vec(ln2_g[i]), vec(ln2_b[i]), *f2, vec(ln3_g[i]), vec(ln3_b[i]))
        hr = _merge_ffn(hr, xm_r, xa_r, gate_r, *tail_params)
        hm = _merge_ffn(hm, xm_m, xa_m, gate_m, *tail_params)

    return hr.reshape(batch, seq, D_MODEL)
```

```python
import functools
import math

import jax
import jax.numpy as jnp
import numpy as np
from jax import lax
from jax.experimental import pallas as pl
from jax.experimental.pallas import tpu as pltpu

F32 = jnp.float32
BF16 = jnp.bfloat16

D_MODEL = 1024
N_META = 16
META_ROWS = 128
M_HEADS = 4
M_DK = 128
M_WIDTH = 512
CONV_W = 4
A_HEADS = 4
A_DK = 64
A_DV = 128
A_WIDTH = 512
D_FF = 2816
FF_CHUNK = 256
N_FF_CHUNKS = D_FF // FF_CHUNK
DEPTH = 2
ALPHA = (2 * DEPTH) ** 0.25
LN_EPS = 1e-5
NEG = -1e30

SEG_QK = (0, 1024)
SEG_V = (1024, 1536)
SEG_O = (1536, 2048)
SEG_IF = (2048, 2176)
SEG_AQ = (2176, 2688)
SEG_AK = (2688, 3200)
SEG_AV = (3200, 3712)
SEG_G = (3712, 5760)
PROJ_COLS = 5760

ROW_TILE = 512
FFN_ROW_TILE = 1024
MLSTM_CHUNK = 256
MLSTM_GROUP = 8
ATTN_BLOCK = 512
ATTN_HEAD_GROUP = 4
LOG2E = math.log2(math.e)
VMEM_LIMIT = 56 * 1024 * 1024


def _nt_dot(a, b):
    return lax.dot_general(a, b, (((1,), (1,)), ((), ())), preferred_element_type=F32)


def _layer_norm(y, g, b):
    mu = jnp.mean(y, -1, keepdims=True)
    d = y - mu
    var = jnp.mean(d * d, -1, keepdims=True)
    return d * lax.rsqrt(var + LN_EPS) * g + b


def _swiglu_ln(x, wg_ref, wu_ref, wd_ref, g_ref, b_ref):
    xb = x.astype(BF16)
    acc = jnp.zeros(x.shape, F32)
    for j in range(N_FF_CHUNKS):
        cols = slice(j * FF_CHUNK, (j + 1) * FF_CHUNK)
        g = jnp.dot(xb, wg_ref[:, cols], preferred_element_type=F32)
        u = jnp.dot(xb, wu_ref[:, cols], preferred_element_type=F32)
        a = (g * jax.nn.sigmoid(g) * u).astype(BF16)
        acc = acc + jnp.dot(a, wd_ref[cols, :], preferred_element_type=F32)
    return _layer_norm(ALPHA * x + 0.5 * acc, g_ref[...], b_ref[...])


def _ffn_ln_kernel(x_ref, wg_ref, wu_ref, wd_ref, g_ref, b_ref, o_ref):
    half = x_ref.shape[0] // 2
    for rows in (slice(0, half), slice(half, 2 * half)):
        o_ref[rows, :] = _swiglu_ln(x_ref[rows, :], wg_ref, wu_ref, wd_ref, g_ref, b_ref)


def _resident(shape):
    return pl.BlockSpec(shape, lambda *_: (0,) * len(shape), pipeline_mode=pl.Buffered(1))


def _ffn_ln(x, w_gate, w_up, w_down, g, b):
    m = x.shape[0]
    tm = min(FFN_ROW_TILE, m)
    return pl.pallas_call(
        _ffn_ln_kernel,
        out_shape=jax.ShapeDtypeStruct((m, D_MODEL), F32),
        grid=(m // tm,),
        in_specs=[
            pl.BlockSpec((tm, D_MODEL), lambda i: (i, 0)),
            _resident((D_MODEL, D_FF)), _resident((D_MODEL, D_FF)), _resident((D_FF, D_MODEL)),
            _resident((1, D_MODEL)), _resident((1, D_MODEL)),
        ],
        out_specs=pl.BlockSpec((tm, D_MODEL), lambda i: (i, 0)),
        compiler_params=pltpu.CompilerParams(
            dimension_semantics=("arbitrary",), vmem_limit_bytes=VMEM_LIMIT),
    )(x, w_gate, w_up, w_down, g, b)


def _proj_kernel(h_ref, w_ref, bif_ref, bg_ref,
                 qk_ref, v_ref, og_ref, gt_ref, aq_ref, ak_ref, av_ref, gate_ref):
    hb = h_ref[...].astype(BF16)

    def seg(s):
        return jnp.dot(hb, w_ref[:, s[0]:s[1]], preferred_element_type=F32)

    gate_ref[...] = jax.nn.sigmoid(seg(SEG_G) + bg_ref[...]).astype(BF16)
    og_ref[...] = jax.nn.sigmoid(seg(SEG_O)).astype(BF16)
    z = seg(SEG_IF) + bif_ref[...]
    col = lax.broadcasted_iota(jnp.int32, z.shape, 1)
    log_f = jnp.minimum(z, 0.0) - jnp.log1p(jnp.exp(-jnp.abs(z)))
    gt_ref[...] = jnp.where(col < M_HEADS, z, jnp.where(col < 2 * M_HEADS, log_f, 0.0))
    aq_ref[...] = (seg(SEG_AQ) * (LOG2E * A_DK ** -0.5)).astype(BF16)
    qk_ref[...] = seg(SEG_QK)
    v_ref[...] = seg(SEG_V).astype(BF16)
    ak_ref[...] = seg(SEG_AK).astype(BF16)
    av_ref[...] = seg(SEG_AV).astype(BF16)


def _proj(h, w_all, b_if, b_gate):
    m = h.shape[0]
    tm = min(ROW_TILE, m)
    row = lambda n: pl.BlockSpec((tm, n), lambda i: (i, 0))
    const = lambda r, n: pl.BlockSpec((r, n), lambda i: (0, 0))
    out_shape = (
        jax.ShapeDtypeStruct((m, 1024), F32),
        jax.ShapeDtypeStruct((m, 512), BF16),
        jax.ShapeDtypeStruct((m, 512), BF16),
        jax.ShapeDtypeStruct((m, 128), F32),
        jax.ShapeDtypeStruct((m, 512), BF16),
        jax.ShapeDtypeStruct((m, 512), BF16),
        jax.ShapeDtypeStruct((m, 512), BF16),
        jax.ShapeDtypeStruct((m, 2048), BF16),
    )
    return pl.pallas_call(
        _proj_kernel,
        out_shape=out_shape,
        grid=(m // tm,),
        in_specs=[row(D_MODEL), const(D_MODEL, PROJ_COLS), const(1, 128), const(1, 2048)],
        out_specs=(row(1024), row(512), row(512), row(128), row(512), row(512), row(512), row(2048)),
        compiler_params=pltpu.CompilerParams(
            dimension_semantics=("arbitrary",), vmem_limit_bytes=VMEM_LIMIT),
    )(h, w_all, b_if, b_gate)


def _mlstm_kernel(qk_ref, v_ref, og_ref, gt_ref, cw_ref, cb_ref, ng_ref,
                  c0_ref, m0_ref, t0_ref,
                  h_ref, c_ref, m_ref, tail_ref, *, T, G):
    @pl.when(pl.program_id(1) == 0)
    def _():
        for g in range(G):
            c_ref[g] = c0_ref[0]
            m_ref[g] = m0_ref[0]
            tail_ref[g] = t0_ref[0]

    r = lax.broadcasted_iota(jnp.int32, (T, T), 0)
    c = lax.broadcasted_iota(jnp.int32, (T, T), 1)
    causal = c <= r
    eye = c == r
    tri = jnp.where(causal, 1.0, 0.0).astype(F32)
    ones = jnp.ones((T, M_DK), BF16)

    for g in range(G):
        u = jnp.concatenate([tail_ref[g], qk_ref[g]], axis=0)
        tail_ref[g] = u[T:T + 8, :]
        y = cb_ref[...] + cw_ref[CONV_W - 1:CONV_W, :] * u
        for j in range(1, CONV_W):
            y = y + cw_ref[CONV_W - 1 - j:CONV_W - j, :] * pltpu.roll(u, j, 0)
        y = y[8:8 + T, :]
        qk = y * jax.nn.sigmoid(y)

        gates = gt_ref[g]
        cums = jnp.dot(tri, gates, preferred_element_type=F32, precision=lax.Precision.HIGHEST)
        u_rows = (gates.T)[0:M_HEADS, :] - (cums.T)[M_HEADS:2 * M_HEADS, :]

        for h in range(M_HEADS):
            sl = slice(h * M_DK, (h + 1) * M_DK)
            qb = qk[:, sl].astype(BF16)
            k_h = qk[:, M_WIDTH + h * M_DK:M_WIDTH + (h + 1) * M_DK] * (M_DK ** -0.5)
            kb = k_h.astype(BF16)
            v_aug = jnp.concatenate([v_ref[g, :, sl], ones], axis=1)
            b_col = cums[:, M_HEADS + h:M_HEADS + h + 1]
            m_prev = m_ref[g, h:h + 1, 0:1]
            cn_prev = c_ref[g, h]

            um = jnp.where(causal, u_rows[h:h + 1, :], NEG)
            big_m = jnp.maximum(jnp.max(um, -1, keepdims=True), m_prev)
            p = jnp.exp(um - big_m)
            s = (_nt_dot(qb, kb) * p).astype(BF16)
            w_state = jnp.exp(m_prev - big_m)
            tot = (jnp.dot(s, v_aug, preferred_element_type=F32)
                   + w_state * jnp.dot(qb, cn_prev.astype(BF16), preferred_element_type=F32))
            num = tot[:, 0:M_DK]
            den = tot[:, M_DK:2 * M_DK]
            hh = num / jnp.maximum(jnp.abs(den), jnp.exp(-(b_col + big_m)))
            mu = jnp.mean(hh, -1, keepdims=True)
            d = hh - mu
            hn = d * lax.rsqrt(jnp.mean(d * d, -1, keepdims=True) + LN_EPS)
            h_ref[g, :, sl] = (hn * ng_ref[:, sl] * og_ref[g, :, sl].astype(F32)).astype(BF16)

            w_diag = jnp.where(eye, p[T - 1:T, :], 0.0).astype(BF16)
            vw = jnp.dot(w_diag, v_aug, preferred_element_type=F32)
            c_ref[g, h] = (w_state[T - 1:T, :] * cn_prev
                           + jnp.dot(k_h.T.astype(BF16), vw.astype(BF16), preferred_element_type=F32))
            m_ref[g, h:h + 1, :] = jnp.broadcast_to(b_col[T - 1:T, :] + big_m[T - 1:T, :], (1, 128))


def _mlstm(qk, v, og, gates, conv_w, conv_b, norm_g, state, *, batch, chunk):
    seq = qk.shape[0] // batch
    nc = seq // chunk
    group = min(MLSTM_GROUP, batch)
    c0, m0, t0 = state
    seq3 = lambda a: a.reshape(batch, seq, a.shape[-1])
    row = lambda n: pl.BlockSpec((group, chunk, n), lambda b, c: (b, c, 0))
    const2 = lambda r, n: pl.BlockSpec((r, n), lambda b, c: (0, 0))
    init3 = lambda r, n: pl.BlockSpec((1, r, n), lambda b, c: (0, 0, 0))
    out3 = lambda r, n: pl.BlockSpec((group, r, n), lambda b, c: (b, 0, 0))
    out_shape = (
        jax.ShapeDtypeStruct((batch, seq, M_WIDTH), BF16),
        jax.ShapeDtypeStruct((batch, M_HEADS, M_DK, 2 * M_DK), F32),
        jax.ShapeDtypeStruct((batch, 8, 128), F32),
        jax.ShapeDtypeStruct((batch, 8, 1024), F32),
    )
    out, c1, m1, t1 = pl.pallas_call(
        functools.partial(_mlstm_kernel, T=chunk, G=group),
        out_shape=out_shape,
        grid=(batch // group, nc),
        in_specs=[
            row(1024), row(512), row(512), row(128),
            const2(8, 1024), const2(1, 1024), const2(1, 512),
            pl.BlockSpec((1, M_HEADS, M_DK, 2 * M_DK), lambda b, c: (0, 0, 0, 0)),
            init3(8, 128), init3(8, 1024),
        ],
        out_specs=(
            row(512),
            pl.BlockSpec((group, M_HEADS, M_DK, 2 * M_DK), lambda b, c: (b, 0, 0, 0)),
            out3(8, 128), out3(8, 1024),
        ),
        compiler_params=pltpu.CompilerParams(
            dimension_semantics=("arbitrary", "arbitrary"), vmem_limit_bytes=VMEM_LIMIT),
    )(seq3(qk), seq3(v), seq3(og), seq3(gates), conv_w, conv_b, norm_g, c0, m0, t0)
    return out.reshape(batch * seq, M_WIDTH), c1, m1, t1


def _stack_maps(q):
    lane = lax.broadcasted_iota(jnp.int32, q.shape, 1)
    zero = jnp.zeros_like(q)
    return jnp.concatenate([jnp.where(lane < A_DK, q, zero), jnp.where(lane >= A_DK, q, zero)], axis=0)


def _diff_finish(acc, l, lam_ref, ng, lam_init, t):
    lam = (jnp.exp(jnp.sum(lam_ref[0:1, :] * lam_ref[1:2, :], -1, keepdims=True))
           - jnp.exp(jnp.sum(lam_ref[2:3, :] * lam_ref[3:4, :], -1, keepdims=True)) + lam_init)
    o = acc[:t] / l[:t] - lam * (acc[t:] / l[t:])
    o = o * lax.rsqrt(jnp.mean(o * o, -1, keepdims=True) + LN_EPS) * (1.0 - lam_init)
    return (o * ng).astype(BF16)


ACC_ROWS = A_DV + 16


def _fold_keys(s_t, offset, v_t, m_s, acc_s, pad_keys=0):
    keys, width = s_t.shape
    m_old = m_s[...]
    m_new = jnp.maximum(m_old, jnp.max(s_t, axis=0, keepdims=True) + offset)
    alpha = jnp.exp2(m_old - m_new)
    p3 = jnp.exp2(s_t.reshape(keys // 8, 8, width) - (m_new - offset)[None])
    p = p3.reshape(keys, width).astype(BF16)
    if pad_keys:
        p = jnp.concatenate([jnp.zeros((pad_keys, width), BF16), p], axis=0)
    acc3 = acc_s[...].reshape(ACC_ROWS // 8, 8, width)
    acc_s[...] = ((acc3 * alpha[None]).reshape(acc_s.shape)
                  + jnp.dot(v_t, p, preferred_element_type=F32))
    m_s[...] = m_new


def _attn_kernel(q_ref, k_ref, v_ref, km_ref, vm_ref, ab_ref, lam_ref, sl_ref, ngt_ref, o_ref,
                 m_s, acc_s, *, T, HG, lam_init):
    i = pl.program_id(2)
    q0 = i * T
    R = 2 * T
    cols = [slice(h * 128, (h + 1) * 128) for h in range(HG)]
    slope = [sl_ref[:, h * 128:h * 128 + 1] for h in range(HG)]
    row = lax.broadcasted_iota(jnp.int32, (128, R), 0)
    ones_rows = jnp.where(row < 3, 1.0, 0.0).astype(BF16)
    qs_t = [jnp.concatenate([_stack_maps(q_ref[:, cols[h]]).astype(F32).T.astype(BF16), ones_rows], axis=0)
            for h in range(HG)]
    stats = [(m_s.at[h], acc_s.at[h]) for h in range(HG)]

    m_s[...] = jnp.full(m_s.shape, NEG, F32)
    acc_s[...] = jnp.zeros(acc_s.shape, F32)

    def rows(j):
        return pl.ds(pl.multiple_of(j * T, T), T)

    def values_t(v):
        v_t = v.astype(F32).T.astype(BF16)
        return jnp.concatenate([v_t, jnp.ones((ACC_ROWS - A_DV, v_t.shape[1]), BF16)], axis=0)

    zero_off = jnp.zeros((1, 1), F32)

    km_pos = lax.broadcasted_iota(jnp.int32, (N_META, 128), 0)
    s_meta = []
    for h in range(HG):
        bias = slope[h] * (km_pos - (N_META + q0)).astype(F32)
        s_meta.append(jnp.dot(km_ref[META_ROWS - N_META:META_ROWS, cols[h]], qs_t[h][0:128],
                              preferred_element_type=F32)
                      + jnp.concatenate([bias] * (R // 128), axis=1))
    for h in range(HG):
        _fold_keys(s_meta[h], zero_off, values_t(vm_ref[:, cols[h]]), *stats[h],
                   pad_keys=META_ROWS - N_META)

    def logits(h, j):
        k_aug = jnp.concatenate([k_ref[rows(j), cols[h]], ab_ref[:, cols[h]]], axis=1)
        return jnp.dot(k_aug, qs_t[h], preferred_element_type=F32)

    def offset(h, j):
        return slope[h] * (j * T - q0).astype(F32)

    def fold_blocks(h, first, n):
        scores = [logits(h, first + k) for k in range(n)]
        for k in range(n):
            _fold_keys(scores[k], offset(h, first + k), values_t(v_ref[rows(first + k), cols[h]]), *stats[h])

    def fold_pair_of_heads(h0, first, n):
        scores = [[logits(h, first + k) for k in range(n)] for h in (h0, h0 + 1)]
        for k in range(n):
            for d in (0, 1):
                _fold_keys(scores[d][k], offset(h0 + d, first + k),
                           values_t(v_ref[rows(first + k), cols[h0 + d]]), *stats[h0 + d])

    for h0 in range(0, HG, 2):
        lax.fori_loop(0, i // 4, lambda t, carry, h0=h0: fold_pair_of_heads(h0, 4 * t, 4) or carry, 0)

    @pl.when(i % 4 >= 2)
    def _():
        for h in range(HG):
            fold_blocks(h, i // 4 * 4, 2)

    @pl.when(i % 2 == 1)
    def _():
        scores = [logits(h, i - 1) for h in range(HG)]
        for h in range(HG):
            _fold_keys(scores[h], offset(h, i - 1), values_t(v_ref[rows(i - 1), cols[h]]), *stats[h])

    key = lax.broadcasted_iota(jnp.int32, (T, T), 0)
    qry = lax.broadcasted_iota(jnp.int32, (T, T), 1)
    keep = key <= qry
    keep2 = jnp.concatenate([keep, keep], axis=1)
    s_diag = [jnp.where(keep2, logits(h, i), NEG) for h in range(HG)]
    for h in range(HG):
        _fold_keys(s_diag[h], zero_off, values_t(v_ref[rows(i), cols[h]]), *stats[h])
    lam = (jnp.exp(jnp.sum(lam_ref[0:1, :] * lam_ref[1:2, :], -1, keepdims=True))
           - jnp.exp(jnp.sum(lam_ref[2:3, :] * lam_ref[3:4, :], -1, keepdims=True)) + lam_init)
    for h in range(HG):
        acc = acc_s[h, 0:A_DV, :]
        l = acc_s[h, A_DV:A_DV + 1, :]
        o_t = acc[:, 0:T] / l[:, 0:T] - lam * (acc[:, T:R] / l[:, T:R])
        o_t = o_t * lax.rsqrt(jnp.mean(o_t * o_t, axis=0, keepdims=True) + LN_EPS) * (1.0 - lam_init)
        o_t = jnp.concatenate([o_t[:, t * 128:(t + 1) * 128] * ngt_ref[cols[h], :] for t in range(T // 128)], axis=1)
        o_ref[:, cols[h]] = o_t.T.astype(BF16)


def _alibi_columns(slopes_log2, t):
    x = np.asarray(slopes_log2, np.float32)[None, :] * np.arange(t, dtype=np.float32)[:, None]
    hi = x.astype(BF16)
    mid = (x - hi.astype(np.float32)).astype(BF16)
    lo = (x - hi.astype(np.float32) - mid.astype(np.float32)).astype(BF16)
    cols = np.zeros((t, A_HEADS, 128), BF16)
    cols[:, :, 0], cols[:, :, 1], cols[:, :, 2] = hi, mid, lo
    return jnp.asarray(cols.reshape(t, A_HEADS * 128))


def _attn(aq, ak, av, ak_meta, av_meta, alibi_cols, lam_rows, slopes, norm_g, *, batch, lam_init):
    m = aq.shape[0]
    seq = m // batch
    t = alibi_cols.shape[0]
    nq = seq // t
    hg = ATTN_HEAD_GROUP
    w = hg * 128
    qspec = pl.BlockSpec((t, w), lambda b, h, i: (b * nq + i, h))
    kvspec = pl.BlockSpec((seq, w), lambda b, h, i: (b, h), pipeline_mode=pl.Buffered(1))
    mspec = pl.BlockSpec((META_ROWS, w), lambda b, h, i: (0, h))
    hrow = pl.BlockSpec((1, w), lambda b, h, i: (0, h))
    norm_g_t = jnp.broadcast_to(norm_g.reshape(A_WIDTH, 1), (A_WIDTH, 128))
    return pl.pallas_call(
        functools.partial(_attn_kernel, T=t, HG=hg, lam_init=lam_init),
        out_shape=jax.ShapeDtypeStruct((m, A_WIDTH), BF16),
        grid=(batch, A_HEADS // hg, nq),
        in_specs=[qspec, kvspec, kvspec, mspec, mspec,
                  pl.BlockSpec((t, w), lambda b, h, i: (0, h)),
                  pl.BlockSpec((8, 128), lambda b, h, i: (0, 0)), hrow,
                  pl.BlockSpec((w, 128), lambda b, h, i: (h, 0))],
        out_specs=qspec,
        scratch_shapes=[pltpu.VMEM((hg, 8, 2 * t), F32), pltpu.VMEM((hg, ACC_ROWS, 2 * t), F32)],
        compiler_params=pltpu.CompilerParams(
            dimension_semantics=("arbitrary", "arbitrary", "arbitrary"),
            vmem_limit_bytes=VMEM_LIMIT),
    )(aq, ak, av, ak_meta, av_meta, alibi_cols, lam_rows, slopes, norm_g_t)


def _attn_meta_kernel(q_ref, k_ref, v_ref, lam_ref, sl_ref, ng_ref, o_ref, *, lam_init):
    t = META_ROWS
    slope = sl_ref[:, 0:1]
    qs = _stack_maps(q_ref[...])
    r = lax.broadcasted_iota(jnp.int32, (t, t), 0)
    c = lax.broadcasted_iota(jnp.int32, (t, t), 1)
    bias = jnp.where((c >= t - N_META) & (c <= r), slope * (c - r).astype(F32), NEG)
    s = _nt_dot(qs, k_ref[...]) + jnp.concatenate([bias, bias], axis=0)
    p = jnp.exp2(s - jnp.max(s, -1, keepdims=True))
    l = jnp.sum(p, -1, keepdims=True)
    acc = jnp.dot(p.astype(BF16), v_ref[...], preferred_element_type=F32)
    o_ref[...] = _diff_finish(acc, l, lam_ref, ng_ref[...], lam_init, t)


def _attn_meta(aq, ak, av, lam_rows, slopes, norm_g, *, lam_init):
    blk = pl.BlockSpec((META_ROWS, 128), lambda h: (0, h))
    hrow = pl.BlockSpec((1, 128), lambda h: (0, h))
    return pl.pallas_call(
        functools.partial(_attn_meta_kernel, lam_init=lam_init),
        out_shape=jax.ShapeDtypeStruct((META_ROWS, A_WIDTH), BF16),
        grid=(A_HEADS,),
        in_specs=[blk, blk, blk, pl.BlockSpec((8, 128), lambda h: (0, 0)), hrow, hrow],
        out_specs=blk,
        compiler_params=pltpu.CompilerParams(dimension_semantics=("arbitrary",)),
    )(aq, ak, av, lam_rows, slopes, norm_g)


def _merge_ffn_kernel(h_ref, hm_ref, ha_ref, gate_ref, wbm_ref, wba_ref, wo_ref, g2_ref, b2_ref,
                      wg_ref, wu_ref, wd_ref, g3_ref, b3_ref, o_ref):
    ym = jnp.dot(hm_ref[...], wbm_ref[...], preferred_element_type=F32)
    ya = jnp.dot(ha_ref[...], wba_ref[...], preferred_element_type=F32)
    gm = gate_ref[:, 0:D_MODEL].astype(F32)
    ga = gate_ref[:, D_MODEL:2 * D_MODEL].astype(F32)
    mixed = (gm * ym + ga * ya).astype(BF16)
    mix = jnp.dot(mixed, wo_ref[...], preferred_element_type=F32)
    h2 = _layer_norm(ALPHA * h_ref[...] + mix, g2_ref[...], b2_ref[...])
    o_ref[...] = _swiglu_ln(h2, wg_ref, wu_ref, wd_ref, g3_ref, b3_ref)


def _merge_ffn(h, hm, ha, gate, w_bm, w_ba, w_out, g2, b2, w_gate, w_up, w_down, g3, b3):
    m = h.shape[0]
    tm = min(ROW_TILE, m)
    row = lambda n: pl.BlockSpec((tm, n), lambda i: (i, 0))
    return pl.pallas_call(
        _merge_ffn_kernel,
        out_shape=jax.ShapeDtypeStruct((m, D_MODEL), F32),
        grid=(m // tm,),
        in_specs=[row(D_MODEL), row(512), row(512), row(2048),
                  _resident((512, D_MODEL)), _resident((512, D_MODEL)), _resident((D_MODEL, D_MODEL)),
                  _resident((1, D_MODEL)), _resident((1, D_MODEL)),
                  _resident((D_MODEL, D_FF)), _resident((D_MODEL, D_FF)), _resident((D_FF, D_MODEL)),
                  _resident((1, D_MODEL)), _resident((1, D_MODEL))],
        out_specs=row(D_MODEL),
        compiler_params=pltpu.CompilerParams(
            dimension_semantics=("arbitrary",), vmem_limit_bytes=VMEM_LIMIT),
    )(h, hm, ha, gate, w_bm, w_ba, w_out, g2, b2, w_gate, w_up, w_down, g3, b3)


def _ffn_weights(w_gate, w_up, w_down):
    return w_gate.astype(BF16), w_up.astype(BF16), w_down.astype(BF16)


def _proj_weights(w_in, b_if, b_gate):
    w = w_in.astype(BF16)
    w_all = jnp.concatenate(
        [w[:, :2048], jnp.pad(w[:, 2048:2056], ((0, 0), (0, 120))), w[:, 2056:]], axis=1)
    return w_all, jnp.pad(b_if, (0, 120)).reshape(1, 128), b_gate.reshape(1, 2 * D_MODEL)


def kernel(x, meta, ffn1_w_gate, ffn1_w_up, ffn1_w_down, ln1_g, ln1_b, w_in, conv_w, conv_b, b_if, m_norm_g, lam_q1, lam_k1, lam_q2, lam_k2, a_norm_g, w_bm, w_ba, b_gate, w_out, ln2_g, ln2_b, ffn2_w_gate, ffn2_w_up, ffn2_w_down, ln3_g, ln3_b):
    batch, seq, _ = x.shape
    rows = batch * seq
    hr = x.reshape(rows, D_MODEL)
    hm = jnp.pad(meta.astype(x.dtype), ((META_ROWS - N_META, 0), (0, 0)))

    slopes_log2 = np.array([LOG2E * 2.0 ** (-8.0 * (h + 1) / A_HEADS) for h in range(A_HEADS)], np.float32)
    slopes = jnp.asarray(np.repeat(slopes_log2, 128).reshape(1, 512))
    alibi_cols = _alibi_columns(slopes_log2, min(ATTN_BLOCK, seq))
    is_meta = (jnp.arange(META_ROWS) >= META_ROWS - N_META)[:, None]
    null_gate = jnp.where(jnp.arange(128) < M_HEADS, NEG, 0.0).astype(F32)[None, :]
    state0 = (jnp.zeros((1, M_HEADS, M_DK, 2 * M_DK), F32), jnp.full((1, 8, 128), NEG, F32),
              jnp.zeros((1, 8, 1024), F32))
    vec = lambda a: a.reshape(1, -1)

    for i in range(DEPTH):
        lam_init = 0.8 - 0.6 * math.exp(-0.3 * i)
        f1 = _ffn_weights(ffn1_w_gate[i], ffn1_w_up[i], ffn1_w_down[i])
        f2 = _ffn_weights(ffn2_w_gate[i], ffn2_w_up[i], ffn2_w_down[i])
        w_all, bif, bg = _proj_weights(w_in[i], b_if[i], b_gate[i])
        cw = jnp.pad(conv_w[i], ((0, 8 - CONV_W), (0, 0)))
        cb = vec(conv_b[i])
        lam_rows = jnp.pad(jnp.stack([lam_q1[i], lam_k1[i], lam_q2[i], lam_k2[i]]).astype(F32),
                           ((0, 4), (0, 128 - A_DK)))
        wbm, wba, wo = w_bm[i].astype(BF16), w_ba[i].astype(BF16), w_out[i].astype(BF16)

        hr = _ffn_ln(hr, *f1, vec(ln1_g[i]), vec(ln1_b[i]))
        hm = _ffn_ln(hm, *f1, vec(ln1_g[i]), vec(ln1_b[i]))

        qk_r, v_r, og_r, gt_r, aq_r, ak_r, av_r, gate_r = _proj(hr, w_all, bif, bg)
        qk_m, v_m, og_m, gt_m, aq_m, ak_m, av_m, gate_m = _proj(hm, w_all, bif, bg)
        qk_m = jnp.where(is_meta, qk_m, 0.0)
        gt_m = jnp.where(is_meta, gt_m, null_gate)

        xm_m, *state = _mlstm(qk_m, v_m, og_m, gt_m, cw, cb, vec(m_norm_g[i]), state0,
                              batch=1, chunk=META_ROWS)
        xm_r = _mlstm(qk_r, v_r, og_r, gt_r, cw, cb, vec(m_norm_g[i]), tuple(state),
                      batch=batch, chunk=min(MLSTM_CHUNK, seq))[0]

        xa_m = _attn_meta(aq_m, ak_m, av_m, lam_rows, slopes, vec(a_norm_g[i]), lam_init=lam_init)
        xa_r = _attn(aq_r, ak_r, av_r, ak_m, av_m, alibi_cols, lam_rows, slopes, vec(a_norm_g[i]),
                     batch=batch, lam_init=lam_init)

        tail_params = (wbm, wba, wo, vec(ln2_g[i]), vec(ln2_b[i]), *f2, vec(ln3_g[i]), vec(ln3_b[i]))
        hr = _merge_ffn(hr, xm_r, xa_r, gate_r, *tail_params)
        hm = _merge_ffn(hm, xm_m, xa_m, gate_m, *tail_params)

    return hr.reshape(batch, seq, D_MODEL)
```
